```python
import jax, jax.numpy as jnp
from jax import lax
import numpy as np

D_MODEL = 1024
BATCH = 2
SEQ = 8192
DEPTH = 1

N_META = 16
D_MIX = D_MODEL
LRU_WIDTH = D_MIX // 2
LRU_HEADS = 8
LRU_HEAD_DIM = LRU_WIDTH // LRU_HEADS
LRU_CONV = 4
LRU_C = 8.0
MLA_HEADS = 8
QK_NOPE = 64
QK_ROPE = 32
QK_DIM = QK_NOPE + QK_ROPE
V_DIM = (D_MIX - LRU_WIDTH) // MLA_HEADS
Q_LORA = 256
KV_LORA = 128
ROPE_THETA = 10000.0
Q_BLOCK = 128
D_FF = 2816
FFN_CONV = 3
EPS = 1e-6
NEG_INF = -1e30
IN_COLS = 2 * LRU_WIDTH + Q_LORA + KV_LORA + QK_ROPE

kernel_name = "hymba_rglru_mla_convglu_encoder"


def rmsnorm(x, g):
    xf = x.astype(jnp.float32)
    y = xf * lax.rsqrt(jnp.mean(xf * xf, axis=-1, keepdims=True) + EPS)
    return (y * g.astype(jnp.float32)).astype(x.dtype)


def dwconv(x, w, b, left):
    K = w.shape[0]
    T = x.shape[1]
    xp = jnp.pad(x, ((0, 0), (left, K - 1 - left), (0, 0)))
    y = xp[:, 0:T] * w[0]
    for k in range(1, K):
        y = y + xp[:, k:k + T] * w[k]
    return y + b


def rope(x, pos):
    R = x.shape[-1]
    inv_freq = ROPE_THETA ** (-jnp.arange(0, R, 2, dtype=jnp.float32) / R)
    ang = pos[:, None] * inv_freq[None, :]
    cos = jnp.cos(ang)[:, None, :]
    sin = jnp.sin(ang)[:, None, :]
    xf = x.astype(jnp.float32)
    x1, x2 = xf[..., :R // 2], xf[..., R // 2:]
    out = jnp.concatenate([x1 * cos - x2 * sin, x1 * sin + x2 * cos], axis=-1)
    return out.astype(x.dtype)


def _lin_combine(left, right):
    a1, b1 = left
    a2, b2 = right
    return a1 * a2, a2 * b1 + b2


def rg_lru_bidir(xc, w_a, b_a, w_x, b_x, lam):
    B, T, W = xc.shape
    xf = xc.astype(jnp.float32)
    xh = xf.reshape(B, T, LRU_HEADS, LRU_HEAD_DIM)
    r = jax.nn.sigmoid(jnp.einsum('bthi,nhij->nbthj', xh, w_a.astype(jnp.float32)).reshape(2, B, T, W)
                       + b_a.astype(jnp.float32)[:, None, None, :])
    i = jax.nn.sigmoid(jnp.einsum('bthi,nhij->nbthj', xh, w_x.astype(jnp.float32)).reshape(2, B, T, W)
                       + b_x.astype(jnp.float32)[:, None, None, :])
    log_a = -LRU_C * r * jax.nn.softplus(-lam.astype(jnp.float32))[:, None, None, :]
    a = jnp.exp(log_a)
    u = jnp.sqrt(-jnp.expm1(2.0 * log_a)) * (i * xf[None])
    _, h_fwd = lax.associative_scan(_lin_combine, (a[0], u[0]), axis=1)
    _, h_bwd = lax.associative_scan(_lin_combine, (a[1], u[1]), axis=1, reverse=True)
    return (h_fwd + h_bwd).astype(xc.dtype)


def block_attention(q, k, v):
    B, T, H, Dk = q.shape
    n_blocks = -(-T // Q_BLOCK)
    T_pad = n_blocks * Q_BLOCK
    pad = ((0, 0), (0, T_pad - T), (0, 0), (0, 0))
    q, k, v = jnp.pad(q, pad), jnp.pad(k, pad), jnp.pad(v, pad)
    key_valid = jnp.arange(T_pad) < T
    scale = 1.0 / np.sqrt(Dk).astype(np.float32)
    qb = q.reshape(B, n_blocks, Q_BLOCK, H, Dk).transpose(1, 0, 2, 3, 4)

    def one_block(qblk):
        s = jnp.einsum('bqhd,bkhd->bhqk', qblk, k).astype(jnp.float32) * scale
        s = jnp.where(key_valid[None, None, None, :], s, NEG_INF)
        p = jax.nn.softmax(s, axis=-1).astype(v.dtype)
        return jnp.einsum('bhqk,bkhd->bqhd', p, v)

    o = lax.map(one_block, qb)
    o = o.transpose(1, 0, 2, 3, 4).reshape(B, T_pad, H, v.shape[-1])
    return o[:, :T]


def setup_inputs(seed: int = 0) -> dict:
    key = jax.random.key(seed)
    ks = jax.random.split(key, 32)
    f32 = jnp.float32

    def nrm(k, shape, fan_in):
        return jax.random.normal(k, shape, f32) * (fan_in ** -0.5)

    def gain(k, shape):
        return 1.0 + 0.02 * jax.random.normal(k, shape, f32)

    L = DEPTH
    a0 = jax.random.uniform(ks[10], (L, 2, LRU_WIDTH), f32, 0.9, 0.999)
    return {
        "x": jax.random.normal(ks[0], (BATCH, SEQ, D_MODEL), f32),
        "meta_tokens": jax.random.normal(ks[1], (N_META, D_MODEL), f32),
        "norm_mix_g": gain(ks[2], (L, D_MODEL)),
        "w_in": nrm(ks[3], (L, D_MODEL, IN_COLS), D_MODEL),
        "conv_lru_w": nrm(ks[4], (L, LRU_CONV, LRU_WIDTH), LRU_CONV),
        "conv_lru_b": 0.01 * jax.random.normal(ks[5], (L, LRU_WIDTH), f32),
        "lru_w_a": nrm(ks[6], (L, 2, LRU_HEADS, LRU_HEAD_DIM, LRU_HEAD_DIM), LRU_HEAD_DIM),
        "lru_b_a": 0.1 * jax.random.normal(ks[7], (L, 2, LRU_WIDTH), f32),
        "lru_w_x": nrm(ks[8], (L, 2, LRU_HEADS, LRU_HEAD_DIM, LRU_HEAD_DIM), LRU_HEAD_DIM),
        "lru_b_x": 0.1 * jax.random.normal(ks[9], (L, 2, LRU_WIDTH), f32),
        "lru_lambda": jnp.log(a0) - jnp.log1p(-a0),
        "lru_gate_g": gain(ks[11], (L, LRU_WIDTH)),
        "q_latent_g": gain(ks[12], (L, Q_LORA)),
        "w_uq": nrm(ks[13], (L, Q_LORA, MLA_HEADS * QK_DIM), Q_LORA),
        "kv_latent_g": gain(ks[14], (L, KV_LORA)),
        "w_ukv": nrm(ks[15], (L, KV_LORA, MLA_HEADS * (QK_NOPE + V_DIM)), KV_LORA),
        "q_norm_g": gain(ks[16], (L, QK_DIM)),
        "k_norm_g": gain(ks[17], (L, QK_DIM)),
        "mla_out_g": gain(ks[18], (L, MLA_HEADS * V_DIM)),
        "w_out": nrm(ks[19], (L, D_MIX, D_MODEL), D_MIX),
        "norm_ffn_g": gain(ks[20], (L, D_MODEL)),
        "w_ffn_up": nrm(ks[21], (L, D_MODEL, 2 * D_FF), D_MODEL),
        "conv_ffn_w": nrm(ks[22], (L, FFN_CONV, D_FF), FFN_CONV),
        "conv_ffn_b": 0.01 * jax.random.normal(ks[23], (L, D_FF), f32),
        "w_ffn_down": nrm(ks[24], (L, D_FF, D_MODEL), D_FF),
    }


def reference(x, meta_tokens, norm_mix_g, w_in, conv_lru_w, conv_lru_b, lru_w_a, lru_b_a,
              lru_w_x, lru_b_x, lru_lambda, lru_gate_g, q_latent_g, w_uq, kv_latent_g, w_ukv,
              q_norm_g, k_norm_g, mla_out_g, w_out, norm_ffn_g, w_ffn_up, conv_ffn_w,
              conv_ffn_b, w_ffn_down):
    B = x.shape[0]
    meta = jnp.broadcast_to(meta_tokens.astype(x.dtype)[None], (B, N_META, D_MODEL))
    h = jnp.concatenate([meta, x], axis=1)
    T = h.shape[1]
    pos = jnp.arange(T, dtype=jnp.float32)
    splits = np.cumsum([LRU_WIDTH, LRU_WIDTH, Q_LORA, KV_LORA]).tolist()

    for l in range(DEPTH):
        hn = rmsnorm(h, norm_mix_g[l])
        proj = hn @ w_in[l]
        x_lru, g_lru, c_q, c_kv, k_pe = jnp.split(proj, splits, axis=-1)

        xc = dwconv(x_lru, conv_lru_w[l], conv_lru_b[l], left=LRU_CONV // 2)
        y_lru = rg_lru_bidir(xc, lru_w_a[l], lru_b_a[l], lru_w_x[l], lru_b_x[l], lru_lambda[l])
        y_lru = y_lru * jax.nn.gelu(g_lru)

        q = (rmsnorm(c_q, q_latent_g[l]) @ w_uq[l]).reshape(B, T, MLA_HEADS, QK_DIM)
        kv = (rmsnorm(c_kv, kv_latent_g[l]) @ w_ukv[l]).reshape(B, T, MLA_HEADS, QK_NOPE + V_DIM)
        k_nope, v = kv[..., :QK_NOPE], kv[..., QK_NOPE:]
        k_rope = jnp.broadcast_to(k_pe[:, :, None, :], (B, T, MLA_HEADS, QK_ROPE))
        k = jnp.concatenate([k_nope, k_rope], axis=-1)
        q = rmsnorm(q, q_norm_g[l])
        k = rmsnorm(k, k_norm_g[l])
        q = jnp.concatenate([q[..., :QK_NOPE], rope(q[..., QK_NOPE:], pos)], axis=-1)
        k = jnp.concatenate([k[..., :QK_NOPE], rope(k[..., QK_NOPE:], pos)], axis=-1)
        y_mla = block_attention(q, k, v).reshape(B, T, MLA_HEADS * V_DIM)

        y = jnp.concatenate([rmsnorm(y_lru, lru_gate_g[l]), rmsnorm(y_mla, mla_out_g[l])], axis=-1)
        h = h + y @ w_out[l]

        hn = rmsnorm(h, norm_ffn_g[l])
        gate, up = jnp.split(hn @ w_ffn_up[l], 2, axis=-1)
        gate = dwconv(gate, conv_ffn_w[l], conv_ffn_b[l], left=FFN_CONV // 2)
        h = h + (jax.nn.silu(gate) * up) @ w_ffn_down[l]

    return h[:, N_META:]
```

```python
import functools
import math

import jax
import jax.numpy as jnp
from jax import lax
from jax.experimental import pallas as pl
from jax.experimental.pallas import tpu as pltpu

D_MODEL = 1024
N_META = 16
LRU_WIDTH = 512
LRU_HEADS = 8
LRU_HEAD_DIM = 64
LRU_CONV = 4
LRU_C = 8.0
MLA_HEADS = 8
QK_NOPE = 64
QK_ROPE = 32
QK_DIM = 96
V_DIM = 64
Q_LORA = 256
KV_LORA = 128
ROPE_THETA = 10000.0
D_FF = 2816
FFN_CONV = 3
EPS = 1e-6
NEG_INF = -1e30

LANE = 128
SUB = 8
X0 = 256
PAD = X0 - N_META
HP = LANE
MASK_LANE = QK_DIM
IN_COLS_P = 2 * LRU_WIDTH + Q_LORA + KV_LORA + LANE
VMEM_LIMIT = 56 * 1024 * 1024


def _rms(x, g):
    return x * lax.rsqrt(jnp.mean(x * x, axis=-1, keepdims=True) + EPS) * g


def _bdot(a, b):
    return jnp.dot(a.astype(jnp.bfloat16), b, preferred_element_type=jnp.float32)


def _inproj_kernel(h_ref, gmix_ref, win_ref, qlg_ref, wuq_ref, kvlg_ref, wuk_ref, wuv_ref,
                   qg_ref, kg_ref, c_ref, s1_ref, s2_ref,
                   xlru_ref, glru_ref, q_ref, k_ref, v_ref, *, tm):
    i = pl.program_id(1)
    hn = _rms(h_ref[...], gmix_ref[...])
    proj = _bdot(hn, win_ref[...])
    xlru_ref[...] = proj[:, :LRU_WIDTH]
    glru_ref[...] = proj[:, LRU_WIDTH:2 * LRU_WIDTH]
    o = 2 * LRU_WIDTH
    cq = proj[:, o:o + Q_LORA]
    ckv = proj[:, o + Q_LORA:o + Q_LORA + KV_LORA]
    kpe = proj[:, o + Q_LORA + KV_LORA:]
    qraw = _bdot(_rms(cq, qlg_ref[...]), wuq_ref[...])
    ckvn = _rms(ckv, kvlg_ref[...]).astype(jnp.bfloat16)
    kraw = jnp.dot(ckvn, wuk_ref[...], preferred_element_type=jnp.float32)
    vraw = jnp.dot(ckvn, wuv_ref[...], preferred_element_type=jnp.float32)

    cosv, sin1, sin2 = c_ref[...], s1_ref[...], s2_ref[...]
    lane = lax.broadcasted_iota(jnp.int32, (tm, HP), 1)
    row = lax.broadcasted_iota(jnp.int32, (tm, HP), 0) + i * tm
    key_mask = jnp.where(row >= PAD, 0.0, NEG_INF)
    scale = 1.0 / math.sqrt(QK_DIM)
    half = QK_ROPE // 2

    def head_norm_rope(xb, g):
        ss = jnp.sum(xb * xb, axis=-1, keepdims=True) * (1.0 / QK_DIM)
        xb = xb * lax.rsqrt(ss + EPS) * g
        return (xb * cosv + pltpu.roll(xb, HP - half, axis=1) * sin1
                + pltpu.roll(xb, half, axis=1) * sin2)

    for hd in range(MLA_HEADS):
        sl = slice(hd * HP, (hd + 1) * HP)
        qb = head_norm_rope(qraw[:, sl], qg_ref[...]) * scale
        q_ref[:, sl] = jnp.where(lane == MASK_LANE, 1.0, qb).astype(jnp.bfloat16)
        kb = head_norm_rope(kraw[:, sl] + kpe, kg_ref[...])
        k_ref[:, sl] = jnp.where(lane == MASK_LANE, key_mask, kb).astype(jnp.bfloat16)
        ones_lane = V_DIM if hd % 2 == 0 else 0
        v_ref[:, sl] = jnp.where(lane == ones_lane, 1.0, vraw[:, sl]).astype(jnp.bfloat16)


def _inproj(hp, gmix, win, qlg, wuq, kvlg, wuk, wuv, qg, kg, ctab, s1tab, s2tab, *, tm):
    B, Tp, D = hp.shape
    nt = Tp // tm
    row = lambda w: pl.BlockSpec((None, tm, w), lambda b, i: (b, i, 0))
    full = lambda a: pl.BlockSpec(a.shape, lambda b, i: (0,) * a.ndim)
    tab = pl.BlockSpec((tm, HP), lambda b, i: (i, 0))
    f32, bf16 = jnp.float32, jnp.bfloat16
    return pl.pallas_call(
        functools.partial(_inproj_kernel, tm=tm),
        grid=(B, nt),
        in_specs=[row(D), full(gmix), full(win), full(qlg), full(wuq), full(kvlg), full(wuk),
                  full(wuv), full(qg), full(kg), tab, tab, tab],
        out_specs=[row(LRU_WIDTH), row(LRU_WIDTH), row(MLA_HEADS * HP), row(MLA_HEADS * HP),
                   row(MLA_HEADS * HP)],
        out_shape=[jax.ShapeDtypeStruct((B, Tp, LRU_WIDTH), f32),
                   jax.ShapeDtypeStruct((B, Tp, LRU_WIDTH), f32),
                   jax.ShapeDtypeStruct((B, Tp, MLA_HEADS * HP), bf16),
                   jax.ShapeDtypeStruct((B, Tp, MLA_HEADS * HP), bf16),
                   jax.ShapeDtypeStruct((B, Tp, MLA_HEADS * HP), bf16)],
        compiler_params=pltpu.CompilerParams(
            dimension_semantics=("arbitrary", "arbitrary"), vmem_limit_bytes=VMEM_LIMIT),
        name="inproj",
    )(hp, gmix, win, qlg, wuq, kvlg, wuk, wuv, qg, kg, ctab, s1tab, s2tab)


def _lru_gates(x_ref, xprev_ref, xnext_ref, cw_ref, cb_ref, wg_ref, ba_ref, bx_ref, lam_ref,
               xw_ref, a_ref, u_ref, *, tt, first, last, t0):
    W = LRU_WIDTH
    xw_ref[0:SUB, :] = jnp.where(first, 0.0, xprev_ref[...])
    xw_ref[SUB:SUB + tt, :] = x_ref[...]
    xw_ref[SUB + tt:, :] = jnp.where(last, 0.0, xnext_ref[...])
    left = LRU_CONV // 2
    xc = cb_ref[...] + xw_ref[SUB - left:SUB - left + tt, :] * cw_ref[0:1, :]
    for k in range(1, LRU_CONV):
        xc = xc + xw_ref[SUB - left + k:SUB - left + k + tt, :] * cw_ref[k:k + 1, :]
    xcb = xc.astype(jnp.bfloat16)
    half = W // 2
    g0 = jnp.dot(xcb[:, :half], wg_ref[0], preferred_element_type=jnp.float32)
    g1 = jnp.dot(xcb[:, half:], wg_ref[1], preferred_element_type=jnp.float32)
    ra = jnp.concatenate([g0[:, :half], g1[:, :half]], axis=1) + ba_ref[...]
    ri = jnp.concatenate([g0[:, half:], g1[:, half:]], axis=1) + bx_ref[...]
    r = jax.nn.sigmoid(ra)
    ig = jax.nn.sigmoid(ri)
    lam = lam_ref[...]
    sp = jnp.maximum(-lam, 0.0) + jnp.log(1.0 + jnp.exp(-jnp.abs(lam)))
    log_a = -LRU_C * r * sp
    a = jnp.exp(log_a)
    u = jnp.sqrt(1.0 - a * a) * (ig * xc)
    rows = lax.broadcasted_iota(jnp.int32, (tt, W), 0) + t0
    a_ref[...] = a
    u_ref[...] = jnp.where(rows >= PAD, u, 0.0)


def _scan_groups(a_ref, u_ref, h_ref, carry_ref, *, tt, reverse):
    W = LRU_WIDTH
    ng = tt // SUB
    sub = lax.broadcasted_iota(jnp.int32, (SUB, W), 0)

    def body(g, carry):
        gi = (ng - 1 - g) if reverse else g
        r0 = pl.multiple_of(gi * SUB, SUB)
        a = a_ref[pl.ds(r0, SUB), :]
        u = u_ref[pl.ds(r0, SUB), :]
        for d in (1, 2, 4):
            if reverse:
                keep = sub < SUB - d
                sh = SUB - d
            else:
                keep = sub >= d
                sh = d
            a_sh = jnp.where(keep, pltpu.roll(a, sh, axis=0), 1.0)
            u_sh = jnp.where(keep, pltpu.roll(u, sh, axis=0), 0.0)
            u = u + a * u_sh
            a = a * a_sh
        h = u + a * carry
        h_ref[pl.ds(r0, SUB), :] = h
        return h[0:1, :] if reverse else h[SUB - 1:SUB, :]

    carry_ref[...] = lax.fori_loop(0, ng, body, carry_ref[...])


def _lru_fwd_kernel(x_ref, xprev_ref, xnext_ref, cw_ref, cb_ref, wg_ref, ba_ref, bx_ref, lam_ref,
                    hf_ref, xw_ref, a_ref, u_ref, carry_ref, *, tt, nt):
    i = pl.program_id(1)

    @pl.when(i == 0)
    def _():
        carry_ref[...] = jnp.zeros_like(carry_ref)

    _lru_gates(x_ref, xprev_ref, xnext_ref, cw_ref, cb_ref, wg_ref, ba_ref, bx_ref, lam_ref,
               xw_ref, a_ref, u_ref, tt=tt, first=i == 0, last=i == nt - 1, t0=i * tt)
    _scan_groups(a_ref, u_ref, hf_ref, carry_ref, tt=tt, reverse=False)


def _lru_bwd_kernel(x_ref, xprev_ref, xnext_ref, cw_ref, cb_ref, wg_ref, ba_ref, bx_ref, lam_ref,
                    hf_ref, gl_ref, gg_ref, y_ref, xw_ref, a_ref, u_ref, hb_ref, carry_ref,
                    *, tt, nt):
    j = pl.program_id(1)
    i = nt - 1 - j

    @pl.when(j == 0)
    def _():
        carry_ref[...] = jnp.zeros_like(carry_ref)

    _lru_gates(x_ref, xprev_ref, xnext_ref, cw_ref, cb_ref, wg_ref, ba_ref, bx_ref, lam_ref,
               xw_ref, a_ref, u_ref, tt=tt, first=i == 0, last=i == nt - 1, t0=i * tt)
    _scan_groups(a_ref, u_ref, hb_ref, carry_ref, tt=tt, reverse=True)
    y = (hf_ref[...] + hb_ref[...]) * jax.nn.gelu(gl_ref[...], approximate=True)
    y_ref[...] = _rms(y, gg_ref[...]).astype(y_ref.dtype)


def _lru_specs(tt, nt, Tp, tmap):
    W = LRU_WIDTH
    nsub = Tp // SUB
    per = tt // SUB
    blk = pl.BlockSpec((None, tt, W), lambda b, j: (b, tmap(j), 0))
    prev = pl.BlockSpec((None, SUB, W), lambda b, j: (b, jnp.maximum(tmap(j) * per - 1, 0), 0))
    nxt = pl.BlockSpec((None, SUB, W),
                       lambda b, j: (b, jnp.minimum((tmap(j) + 1) * per, nsub - 1), 0))
    return blk, prev, nxt


def _lru_fwd(xl, cw, cb, wg, ba, bx, lam, *, tt):
    B, Tp, W = xl.shape
    nt = Tp // tt
    blk, prev, nxt = _lru_specs(tt, nt, Tp, lambda j: j)
    full = lambda a: pl.BlockSpec(a.shape, lambda b, j: (0,) * a.ndim)
    return pl.pallas_call(
        functools.partial(_lru_fwd_kernel, tt=tt, nt=nt),
        grid=(B, nt),
        in_specs=[blk, prev, nxt, full(cw), full(cb), full(wg), full(ba), full(bx), full(lam)],
        out_specs=blk,
        out_shape=jax.ShapeDtypeStruct((B, Tp, W), jnp.float32),
        scratch_shapes=[pltpu.VMEM((tt + 2 * SUB, W), jnp.float32),
                        pltpu.VMEM((tt, W), jnp.float32),
                        pltpu.VMEM((tt, W), jnp.float32),
                        pltpu.VMEM((1, W), jnp.float32)],
        compiler_params=pltpu.CompilerParams(
            dimension_semantics=("arbitrary", "arbitrary"), vmem_limit_bytes=VMEM_LIMIT),
        name="lru_fwd",
    )(xl, xl, xl, cw, cb, wg, ba, bx, lam)


def _lru_bwd(xl, cw, cb, wg, ba, bx, lam, hf, gl, gg, *, tt):
    B, Tp, W = xl.shape
    nt = Tp // tt
    blk, prev, nxt = _lru_specs(tt, nt, Tp, lambda j: nt - 1 - j)
    full = lambda a: pl.BlockSpec(a.shape, lambda b, j: (0,) * a.ndim)
    return pl.pallas_call(
        functools.partial(_lru_bwd_kernel, tt=tt, nt=nt),
        grid=(B, nt),
        in_specs=[blk, prev, nxt, full(cw), full(cb), full(wg), full(ba), full(bx), full(lam),
                  blk, blk, full(gg)],
        out_specs=blk,
        out_shape=jax.ShapeDtypeStruct((B, Tp, W), jnp.bfloat16),
        scratch_shapes=[pltpu.VMEM((tt + 2 * SUB, W), jnp.float32),
                        pltpu.VMEM((tt, W), jnp.float32),
                        pltpu.VMEM((tt, W), jnp.float32),
                        pltpu.VMEM((tt, W), jnp.float32),
                        pltpu.VMEM((1, W), jnp.float32)],
        compiler_params=pltpu.CompilerParams(
            dimension_semantics=("arbitrary", "arbitrary"), vmem_limit_bytes=VMEM_LIMIT),
        name="lru_bwd",
    )(xl, xl, xl, cw, cb, wg, ba, bx, lam, hf, gl, gg)


def _attn_kernel(q_ref, k_ref, v_ref, o_ref, *, tq, tk, nk):
    q0 = q_ref[:, :HP]
    q1 = q_ref[:, HP:]

    def step(q, kc, vc, m, acc):
        s = lax.dot_general(q, kc, (((1,), (1,)), ((), ())),
                            preferred_element_type=jnp.float32)
        m_new = jnp.maximum(m, jnp.max(s, axis=-1, keepdims=True))
        p = jnp.exp(s - m_new)
        acc = acc * jnp.exp(m - m_new) + jnp.dot(p.astype(jnp.bfloat16), vc,
                                                 preferred_element_type=jnp.float32)
        return m_new, acc

    def body(j, carry):
        m0, a0, m1, a1 = carry
        r0 = pl.multiple_of(j * tk, tk)
        m0, a0 = step(q0, k_ref[pl.ds(r0, tk), :HP], v_ref[pl.ds(r0, tk), :HP], m0, a0)
        m1, a1 = step(q1, k_ref[pl.ds(r0, tk), HP:], v_ref[pl.ds(r0, tk), HP:], m1, a1)
        return m0, a0, m1, a1

    m_init = jnp.full((tq, 1), NEG_INF, jnp.float32)
    a_init = jnp.zeros((tq, HP), jnp.float32)
    _, a0, _, a1 = lax.fori_loop(0, nk, body, (m_init, a_init, m_init, a_init))
    lane = lax.broadcasted_iota(jnp.int32, (tq, HP), 1)
    o0 = a0 / a0[:, V_DIM:V_DIM + 1]
    o1 = a1 / a1[:, 0:1]
    o_ref[...] = jnp.where(lane < V_DIM, o0, o1)


def _attn(q, k, v, *, tq, tk):
    B, Tp, _ = q.shape
    npair = MLA_HEADS // 2
    return pl.pallas_call(
        functools.partial(_attn_kernel, tq=tq, tk=tk, nk=Tp // tk),
        grid=(B, npair, Tp // tq),
        in_specs=[pl.BlockSpec((None, tq, 2 * HP), lambda b, p, i: (b, i, p)),
                  pl.BlockSpec((None, Tp, 2 * HP), lambda b, p, i: (b, 0, p)),
                  pl.BlockSpec((None, Tp, 2 * HP), lambda b, p, i: (b, 0, p))],
        out_specs=pl.BlockSpec((None, tq, HP), lambda b, p, i: (b, i, p)),
        out_shape=jax.ShapeDtypeStruct((B, Tp, npair * HP), jnp.float32),
        compiler_params=pltpu.CompilerParams(
            dimension_semantics=("arbitrary", "arbitrary", "arbitrary"),
            vmem_limit_bytes=VMEM_LIMIT),
        name="attn",
    )(q, k, v)


def _outproj_kernel(h_ref, yl_ref, ym_ref, mg_ref, wo_ref, o_ref):
    ymn = _rms(ym_ref[...], mg_ref[...]).astype(jnp.bfloat16)
    y = jnp.concatenate([yl_ref[...], ymn], axis=1)
    o_ref[...] = h_ref[...] + jnp.dot(y, wo_ref[...], preferred_element_type=jnp.float32)


def _outproj(hp, yl, ym, mg, wo, *, tm):
    B, Tp, D = hp.shape
    row = lambda w: pl.BlockSpec((None, tm, w), lambda b, i: (b, i, 0))
    full = lambda a: pl.BlockSpec(a.shape, lambda b, i: (0,) * a.ndim)
    return pl.pallas_call(
        _outproj_kernel,
        grid=(B, Tp // tm),
        in_specs=[row(D), row(LRU_WIDTH), row(LRU_WIDTH), full(mg), full(wo)],
        out_specs=row(D),
        out_shape=jax.ShapeDtypeStruct((B, Tp, D), jnp.float32),
        compiler_params=pltpu.CompilerParams(
            dimension_semantics=("arbitrary", "arbitrary"), vmem_limit_bytes=VMEM_LIMIT),
        name="outproj",
    )(hp, yl, ym, mg, wo)


def _ffn_kernel(h_ref, hprev_ref, hnext_ref, g_ref, wg_ref, wu_ref, cw_ref, cb_ref, wd_ref,
                o_ref, gate_ref, *, tm, nt):
    i = pl.program_id(1)
    h = h_ref[...]
    hnext = jnp.where(i == nt - 1, 0.0, hnext_ref[...])
    hw = jnp.concatenate([hprev_ref[...], h, hnext], axis=0)
    hn = _rms(hw, g_ref[...]).astype(jnp.bfloat16)
    gate_ref[...] = jnp.dot(hn, wg_ref[...], preferred_element_type=jnp.float32)
    up = jnp.dot(hn[SUB:SUB + tm], wu_ref[...], preferred_element_type=jnp.float32)
    left = FFN_CONV // 2
    gc = cb_ref[...] + gate_ref[SUB - left:SUB - left + tm, :] * cw_ref[0:1, :]
    for k in range(1, FFN_CONV):
        gc = gc + gate_ref[SUB - left + k:SUB - left + k + tm, :] * cw_ref[k:k + 1, :]
    act = (gc * jax.nn.sigmoid(gc) * up).astype(jnp.bfloat16)
    o_ref[...] = h + jnp.dot(act, wd_ref[...], preferred_element_type=jnp.float32)


def _ffn(h1, g, wg, wu, cw, cb, wd, *, tm, seq):
    B, Tp, D = h1.shape
    nt = seq // tm
    off = X0 // tm
    per = tm // SUB
    nsub = Tp // SUB
    full = lambda a: pl.BlockSpec(a.shape, lambda b, i: (0,) * a.ndim)
    return pl.pallas_call(
        functools.partial(_ffn_kernel, tm=tm, nt=nt),
        grid=(B, nt),
        in_specs=[pl.BlockSpec((None, tm, D), lambda b, i: (b, i + off, 0)),
                  pl.BlockSpec((None, SUB, D), lambda b, i: (b, (i + off) * per - 1, 0)),
                  pl.BlockSpec((None, SUB, D),
                               lambda b, i: (b, jnp.minimum((i + off + 1) * per, nsub - 1), 0)),
                  full(g), full(wg), full(wu), full(cw), full(cb), full(wd)],
        out_specs=pl.BlockSpec((None, tm, D), lambda b, i: (b, i, 0)),
        out_shape=jax.ShapeDtypeStruct((B, seq, D), jnp.float32),
        scratch_shapes=[pltpu.VMEM((tm + 2 * SUB, D_FF), jnp.float32)],
        compiler_params=pltpu.CompilerParams(
            dimension_semantics=("arbitrary", "arbitrary"), vmem_limit_bytes=VMEM_LIMIT),
        name="ffn",
    )(h1, h1, h1, g, wg, wu, cw, cb, wd)


def _pad_heads(w, width, offset=0):
    K = w.shape[0]
    w = w.reshape(K, MLA_HEADS, width)
    w = jnp.pad(w, ((0, 0), (0, 0), (offset, HP - width - offset)))
    return w.reshape(K, MLA_HEADS * HP)


def _block_diag(w):
    H, Dh, _ = w.shape
    eye = jnp.eye(H, dtype=w.dtype)
    return (eye[:, None, :, None] * w[:, :, None, :]).reshape(H * Dh, H * Dh)


def _gate_weights(w_a, w_x):
    half = LRU_WIDTH // 2
    wa = _block_diag(w_a)
    wx = _block_diag(w_x)
    halves = [jnp.concatenate([wa[c * half:(c + 1) * half, c * half:(c + 1) * half],
                               wx[c * half:(c + 1) * half, c * half:(c + 1) * half]], axis=1)
              for c in range(2)]
    return jnp.stack(halves).astype(jnp.bfloat16)


def kernel(x, meta_tokens, norm_mix_g, w_in, conv_lru_w, conv_lru_b, lru_w_a, lru_b_a, lru_w_x,
           lru_b_x, lru_lambda, lru_gate_g, q_latent_g, w_uq, kv_latent_g, w_ukv, q_norm_g,
           k_norm_g, mla_out_g, w_out, norm_ffn_g, w_ffn_up, conv_ffn_w, conv_ffn_b, w_ffn_down):
    B, S, D = x.shape
    f32, bf16 = jnp.float32, jnp.bfloat16
    Tp = X0 + S
    l = 0
    r2 = lambda a: a.reshape(1, -1).astype(f32)

    head = jnp.concatenate([jnp.zeros((PAD, D), x.dtype), meta_tokens.astype(x.dtype)], axis=0)
    hp = jnp.concatenate([jnp.broadcast_to(head[None], (B, X0, D)), x], axis=1)

    o = 2 * LRU_WIDTH + Q_LORA + KV_LORA
    win = jnp.concatenate(
        [w_in[l][:, :o], jnp.pad(w_in[l][:, o:], ((0, 0), (QK_NOPE, LANE - QK_DIM)))],
        axis=1).astype(bf16)
    wuq = _pad_heads(w_uq[l], QK_DIM).astype(bf16)
    wkv = w_ukv[l].reshape(KV_LORA, MLA_HEADS, QK_NOPE + V_DIM)
    wuk = _pad_heads(wkv[:, :, :QK_NOPE].reshape(KV_LORA, -1), QK_NOPE).astype(bf16)
    wv = jnp.pad(wkv[:, :, QK_NOPE:], ((0, 0), (0, 0), (0, HP - V_DIM)))
    wv = jnp.where((jnp.arange(MLA_HEADS) % 2 == 1)[None, :, None],
                   jnp.roll(wv, V_DIM, axis=2), wv)
    wuv = wv.reshape(KV_LORA, MLA_HEADS * HP).astype(bf16)
    qg = jnp.pad(q_norm_g[l], (0, HP - QK_DIM)).reshape(1, HP)
    kg = jnp.pad(k_norm_g[l], (0, HP - QK_DIM)).reshape(1, HP)

    pos = jnp.arange(Tp, dtype=f32) - PAD
    inv_freq = ROPE_THETA ** (-jnp.arange(0, QK_ROPE, 2, dtype=f32) / QK_ROPE)
    ang = pos[:, None] * inv_freq[None, :]
    cos, sin = jnp.cos(ang), jnp.sin(ang)
    half = QK_ROPE // 2
    z = lambda w: jnp.zeros((Tp, w), f32)
    ctab = jnp.concatenate([jnp.ones((Tp, QK_NOPE), f32), cos, cos,
                            jnp.ones((Tp, HP - QK_DIM), f32)], axis=1)
    s1tab = jnp.concatenate([z(QK_NOPE), -sin, z(HP - QK_NOPE - half)], axis=1)
    s2tab = jnp.concatenate([z(QK_NOPE + half), sin, z(HP - QK_DIM)], axis=1)

    xl, gl, q, k, v = _inproj(hp, r2(norm_mix_g[l]), win, r2(q_latent_g[l]), wuq,
                              r2(kv_latent_g[l]), wuk, wuv, qg, kg, ctab, s1tab, s2tab, tm=768)

    cw, cb = conv_lru_w[l], r2(conv_lru_b[l])
    wg = [_gate_weights(lru_w_a[l, d], lru_w_x[l, d]) for d in range(2)]
    hf = _lru_fwd(xl, cw, cb, wg[0], r2(lru_b_a[l, 0]), r2(lru_b_x[l, 0]), r2(lru_lambda[l, 0]),
                  tt=256)
    yl = _lru_bwd(xl, cw, cb, wg[1], r2(lru_b_a[l, 1]), r2(lru_b_x[l, 1]), r2(lru_lambda[l, 1]),
                  hf, gl, r2(lru_gate_g[l]), tt=256)

    ym = _attn(q, k, v, tq=256, tk=768)

    h1 = _outproj(hp, yl, ym, r2(mla_out_g[l]), w_out[l].astype(bf16), tm=768)

    wup = w_ffn_up[l].astype(bf16)
    return _ffn(h1, r2(norm_ffn_g[l]), wup[:, :D_FF], wup[:, D_FF:], conv_ffn_w[l],
                r2(conv_ffn_b[l]), w_ffn_down[l].astype(bf16), tm=256, seq=S)
```

```python
import functools
import math

import jax
import jax.numpy as jnp
from jax import lax
from jax.experimental import pallas as pl
from jax.experimental.pallas import tpu as pltpu

D_MODEL = 1024
N_META = 16
LRU_WIDTH = 512
LRU_HEADS = 8
LRU_HEAD_DIM = 64
LRU_CONV = 4
LRU_C = 8.0
MLA_HEADS = 8
QK_NOPE = 64
QK_ROPE = 32
QK_DIM = 96
V_DIM = 64
Q_LORA = 256
KV_LORA = 128
ROPE_THETA = 10000.0
D_FF = 2816
FFN_CONV = 3
EPS = 1e-6
NEG_INF = -1e30

LANE = 128
SUB = 8
X0 = 256
PAD = X0 - N_META
HP = LANE
MASK_LANE = QK_DIM
IN_COLS_P = 2 * LRU_WIDTH + Q_LORA + KV_LORA + LANE
VMEM_LIMIT = 56 * 1024 * 1024


def _rms(x, g):
    return x * lax.rsqrt(jnp.mean(x * x, axis=-1, keepdims=True) + EPS) * g


def _bdot(a, b):
    return jnp.dot(a.astype(jnp.bfloat16), b, preferred_element_type=jnp.float32)


def _inproj_kernel(h_ref, gmix_ref, win_ref, qlg_ref, wuq_ref, kvlg_ref, wuk_ref, wuv_ref,
                   qg_ref, kg_ref, c_ref, s1_ref, s2_ref,
                   xlru_ref, glru_ref, q_ref, k_ref, v_ref, *, tm):
    i = pl.program_id(1)
    hn = _rms(h_ref[...], gmix_ref[...])
    proj = _bdot(hn, win_ref[...])
    xlru_ref[...] = proj[:, :LRU_WIDTH]
    glru_ref[...] = proj[:, LRU_WIDTH:2 * LRU_WIDTH]
    o = 2 * LRU_WIDTH
    cq = proj[:, o:o + Q_LORA]
    ckv = proj[:, o + Q_LORA:o + Q_LORA + KV_LORA]
    kpe = proj[:, o + Q_LORA + KV_LORA:]
    qraw = _bdot(_rms(cq, qlg_ref[...]), wuq_ref[...])
    ckvn = _rms(ckv, kvlg_ref[...]).astype(jnp.bfloat16)
    kraw = jnp.dot(ckvn, wuk_ref[...], preferred_element_type=jnp.float32)
    vraw = jnp.dot(ckvn, wuv_ref[...], preferred_element_type=jnp.float32)

    cosv, sin1, sin2 = c_ref[...], s1_ref[...], s2_ref[...]
    lane = lax.broadcasted_iota(jnp.int32, (tm, HP), 1)
    row = lax.broadcasted_iota(jnp.int32, (tm, HP), 0) + i * tm
    key_mask = jnp.where(row >= PAD, 0.0, NEG_INF)
    scale = 1.0 / math.sqrt(QK_DIM)
    half = QK_ROPE // 2

    def head_norm_rope(xb, g):
        ss = jnp.sum(xb * xb, axis=-1, keepdims=True) * (1.0 / QK_DIM)
        xb = xb * lax.rsqrt(ss + EPS) * g
        return (xb * cosv + pltpu.roll(xb, HP - half, axis=1) * sin1
                + pltpu.roll(xb, half, axis=1) * sin2)

    for hd in range(MLA_HEADS):
        sl = slice(hd * HP, (hd + 1) * HP)
        qb = head_norm_rope(qraw[:, sl], qg_ref[...]) * scale
        q_ref[:, sl] = jnp.where(lane == MASK_LANE, 1.0, qb).astype(jnp.bfloat16)
        kb = head_norm_rope(kraw[:, sl] + kpe, kg_ref[...])
        k_ref[:, sl] = jnp.where(lane == MASK_LANE, key_mask, kb).astype(jnp.bfloat16)
        ones_lane = V_DIM if hd % 2 == 0 else 0
        v_ref[:, sl] = jnp.where(lane == ones_lane, 1.0, vraw[:, sl]).astype(jnp.bfloat16)


def _inproj(hp, gmix, win, qlg, wuq, kvlg, wuk, wuv, qg, kg, ctab, s1tab, s2tab, *, tm):
    B, Tp, D = hp.shape
    nt = Tp // tm
    row = lambda w: pl.BlockSpec((None, tm, w), lambda b, i: (b, i, 0))
    full = lambda a: pl.BlockSpec(a.shape, lambda b, i: (0,) * a.ndim)
    tab = pl.BlockSpec((tm, HP), lambda b, i: (i, 0))
    f32, bf16 = jnp.float32, jnp.bfloat16
    return pl.pallas_call(
        functools.partial(_inproj_kernel, tm=tm),
        grid=(B, nt),
        in_specs=[row(D), full(gmix), full(win), full(qlg), full(wuq), full(kvlg), full(wuk),
                  full(wuv), full(qg), full(kg), tab, tab, tab],
        out_specs=[row(LRU_WIDTH), row(LRU_WIDTH), row(MLA_HEADS * HP), row(MLA_HEADS * HP),
                   row(MLA_HEADS * HP)],
        out_shape=[jax.ShapeDtypeStruct((B, Tp, LRU_WIDTH), f32),
                   jax.ShapeDtypeStruct((B, Tp, LRU_WIDTH), f32),
                   jax.ShapeDtypeStruct((B, Tp, MLA_HEADS * HP), bf16),
                   jax.ShapeDtypeStruct((B, Tp, MLA_HEADS * HP), bf16),
                   jax.ShapeDtypeStruct((B, Tp, MLA_HEADS * HP), bf16)],
        compiler_params=pltpu.CompilerParams(
            dimension_semantics=("arbitrary", "arbitrary"), vmem_limit_bytes=VMEM_LIMIT),
        name="inproj",
    )(hp, gmix, win, qlg, wuq, kvlg, wuk, wuv, qg, kg, ctab, s1tab, s2tab)


def _lru_gates(x_ref, xprev_ref, xnext_ref, cw_ref, cb_ref, wg_ref, ba_ref, bx_ref, lam_ref,
               xw_ref, a_ref, u_ref, *, tt, first, last, t0):
    W = LRU_WIDTH
    xw_ref[0:SUB, :] = jnp.where(first, 0.0, xprev_ref[...])
    xw_ref[SUB:SUB + tt, :] = x_ref[...]
    xw_ref[SUB + tt:, :] = jnp.where(last, 0.0, xnext_ref[...])
    left = LRU_CONV // 2
    xc = cb_ref[...] + xw_ref[SUB - left:SUB - left + tt, :] * cw_ref[0:1, :]
    for k in range(1, LRU_CONV):
        xc = xc + xw_ref[SUB - left + k:SUB - left + k + tt, :] * cw_ref[k:k + 1, :]
    xcb = xc.astype(jnp.bfloat16)
    half = W // 2
    g0 = jnp.dot(xcb[:, :half], wg_ref[0], preferred_element_type=jnp.float32)
    g1 = jnp.dot(xcb[:, half:], wg_ref[1], preferred_element_type=jnp.float32)
    ra = jnp.concatenate([g0[:, :half], g1[:, :half]], axis=1) + ba_ref[...]
    ri = jnp.concatenate([g0[:, half:], g1[:, half:]], axis=1) + bx_ref[...]
    r = jax.nn.sigmoid(ra)
    ig = jax.nn.sigmoid(ri)
    lam = lam_ref[...]
    sp = jnp.maximum(-lam, 0.0) + jnp.log(1.0 + jnp.exp(-jnp.abs(lam)))
    log_a = -LRU_C * r * sp
    a = jnp.exp(log_a)
    u = jnp.sqrt(1.0 - a * a) * (ig * xc)
    rows = lax.broadcasted_iota(jnp.int32, (tt, W), 0) + t0
    a_ref[...] = a
    u_ref[...] = jnp.where(rows >= PAD, u, 0.0)


def _scan_groups(a_ref, u_ref, h_ref, carry_ref, *, tt, reverse):
    W = LRU_WIDTH
    ng = tt // SUB
    sub = lax.broadcasted_iota(jnp.int32, (SUB, W), 0)

    def body(g, carry):
        gi = (ng - 1 - g) if reverse else g
        r0 = pl.multiple_of(gi * SUB, SUB)
        a = a_ref[pl.ds(r0, SUB), :]
        u = u_ref[pl.ds(r0, SUB), :]
        for d in (1, 2, 4):
            if reverse:
                keep = sub < SUB - d
                sh = SUB - d
            else:
                keep = sub >= d
                sh = d
            a_sh = jnp.where(keep, pltpu.roll(a, sh, axis=0), 1.0)
            u_sh = jnp.where(keep, pltpu.roll(u, sh, axis=0), 0.0)
            u = u + a * u_sh
            a = a * a_sh
        h = u + a * carry
        h_ref[pl.ds(r0, SUB), :] = h
        return h[0:1, :] if reverse else h[SUB - 1:SUB, :]

    carry_ref[...] = lax.fori_loop(0, ng, body, carry_ref[...])


def _lru_fwd_kernel(x_ref, xprev_ref, xnext_ref, cw_ref, cb_ref, wg_ref, ba_ref, bx_ref, lam_ref,
                    hf_ref, xw_ref, a_ref, u_ref, carry_ref, *, tt, nt):
    i = pl.program_id(1)

    @pl.when(i == 0)
    def _():
        carry_ref[...] = jnp.zeros_like(carry_ref)

    _lru_gates(x_ref, xprev_ref, xnext_ref, cw_ref, cb_ref, wg_ref, ba_ref, bx_ref, lam_ref,
               xw_ref, a_ref, u_ref, tt=tt, first=i == 0, last=i == nt - 1, t0=i * tt)
    _scan_groups(a_ref, u_ref, hf_ref, carry_ref, tt=tt, reverse=False)


def _lru_bwd_kernel(x_ref, xprev_ref, xnext_ref, cw_ref, cb_ref, wg_ref, ba_ref, bx_ref, lam_ref,
                    hf_ref, gl_ref, gg_ref, y_ref, xw_ref, a_ref, u_ref, hb_ref, carry_ref,
                    *, tt, nt):
    j = pl.program_id(1)
    i = nt - 1 - j

    @pl.when(j == 0)
    def _():
        carry_ref[...] = jnp.zeros_like(carry_ref)

    _lru_gates(x_ref, xprev_ref, xnext_ref, cw_ref, cb_ref, wg_ref, ba_ref, bx_ref, lam_ref,
               xw_ref, a_ref, u_ref, tt=tt, first=i == 0, last=i == nt - 1, t0=i * tt)
    _scan_groups(a_ref, u_ref, hb_ref, carry_ref, tt=tt, reverse=True)
    y = (hf_ref[...] + hb_ref[...]) * jax.nn.gelu(gl_ref[...], approximate=True)
    y_ref[...] = _rms(y, gg_ref[...]).astype(y_ref.dtype)


def _lru_specs(tt, nt, Tp, tmap):
    W = LRU_WIDTH
    nsub = Tp // SUB
    per = tt // SUB
    blk = pl.BlockSpec((None, tt, W), lambda b, j: (b, tmap(j), 0))
    prev = pl.BlockSpec((None, SUB, W), lambda b, j: (b, jnp.maximum(tmap(j) * per - 1, 0), 0))
    nxt = pl.BlockSpec((None, SUB, W),
                       lambda b, j: (b, jnp.minimum((tmap(j) + 1) * per, nsub - 1), 0))
    return blk, prev, nxt


def _lru_fwd(xl, cw, cb, wg, ba, bx, lam, *, tt):
    B, Tp, W = xl.shape
    nt = Tp // tt
    blk, prev, nxt = _lru_specs(tt, nt, Tp, lambda j: j)
    full = lambda a: pl.BlockSpec(a.shape, lambda b, j: (0,) * a.ndim)
    return pl.pallas_call(
        functools.partial(_lru_fwd_kernel, tt=tt, nt=nt),
        grid=(B, nt),
        in_specs=[blk, prev, nxt, full(cw), full(cb), full(wg), full(ba), full(bx), full(lam)],
        out_specs=blk,
        out_shape=jax.ShapeDtypeStruct((B, Tp, W), jnp.float32),
        scratch_shapes=[pltpu.VMEM((tt + 2 * SUB, W), jnp.float32),
                        pltpu.VMEM((tt, W), jnp.float32),
                        pltpu.VMEM((tt, W), jnp.float32),
                        pltpu.VMEM((1, W), jnp.float32)],
        compiler_params=pltpu.CompilerParams(
            dimension_semantics=("arbitrary", "arbitrary"), vmem_limit_bytes=VMEM_LIMIT),
        name="lru_fwd",
    )(xl, xl, xl, cw, cb, wg, ba, bx, lam)


def _lru_bwd(xl, cw, cb, wg, ba, bx, lam, hf, gl, gg, *, tt):
    B, Tp, W = xl.shape
    nt = Tp // tt
    blk, prev, nxt = _lru_specs(tt, nt, Tp, lambda j: nt - 1 - j)
    full = lambda a: pl.BlockSpec(a.shape, lambda b, j: (0,) * a.ndim)
    return pl.pallas_call(
        functools.partial(_lru_bwd_kernel, tt=tt, nt=nt),
        grid=(B, nt),
        in_specs=[blk, prev, nxt, full(cw), full(cb), full(wg), full(ba), full(bx), full(lam),
                  blk, blk, full(gg)],
        out_specs=blk,
        out_shape=jax.ShapeDtypeStruct((B, Tp, W), jnp.bfloat16),
        scratch_shapes=[pltpu.VMEM((tt + 2 * SUB, W), jnp.float32),
                        pltpu.VMEM((tt, W), jnp.float32),
                        pltpu.VMEM((tt, W), jnp.float32),
                        pltpu.VMEM((tt, W), jnp.float32),
                        pltpu.VMEM((1, W), jnp.float32)],
        compiler_params=pltpu.CompilerParams(
            dimension_semantics=("arbitrary", "arbitrary"), vmem_limit_bytes=VMEM_LIMIT),
        name="lru_bwd",
    )(xl, xl, xl, cw, cb, wg, ba, bx, lam, hf, gl, gg)


def _attn_kernel(q_ref, k_ref, v_ref, o_ref, *, tq, tk, nk):
    accs = []
    for hh in range(2):
        sl = slice(hh * HP, (hh + 1) * HP)
        s = lax.dot_general(q_ref[:, sl], k_ref[:, sl], (((1,), (1,)), ((), ())),
                            preferred_element_type=jnp.float32)
        m = jnp.max(s, axis=-1, keepdims=True)
        p = jnp.exp(s - m).astype(jnp.bfloat16)
        accs.append(jnp.dot(p, v_ref[:, sl], preferred_element_type=jnp.float32))
    a0, a1 = accs
    lane = lax.broadcasted_iota(jnp.int32, (tq, HP), 1)
    o0 = a0 / a0[:, V_DIM:V_DIM + 1]
    o1 = a1 / a1[:, 0:1]
    o_ref[...] = jnp.where(lane < V_DIM, o0, o1)


def _attn(q, k, v, *, tq, tk):
    B, Tp, _ = q.shape
    npair = MLA_HEADS // 2
    return pl.pallas_call(
        functools.partial(_attn_kernel, tq=tq, tk=tk, nk=Tp // tk),
        grid=(B, npair, Tp // tq),
        in_specs=[pl.BlockSpec((None, tq, 2 * HP), lambda b, p, i: (b, i, p)),
                  pl.BlockSpec((None, Tp, 2 * HP), lambda b, p, i: (b, 0, p)),
                  pl.BlockSpec((None, Tp, 2 * HP), lambda b, p, i: (b, 0, p))],
        out_specs=pl.BlockSpec((None, tq, HP), lambda b, p, i: (b, i, p)),
        out_shape=jax.ShapeDtypeStruct((B, Tp, npair * HP), jnp.float32),
        compiler_params=pltpu.CompilerParams(
            dimension_semantics=("arbitrary", "arbitrary", "arbitrary"),
            vmem_limit_bytes=VMEM_LIMIT),
        name="attn",
    )(q, k, v)


def _outproj_kernel(h_ref, yl_ref, ym_ref, mg_ref, wo_ref, o_ref):
    ymn = _rms(ym_ref[...], mg_ref[...]).astype(jnp.bfloat16)
    y = jnp.concatenate([yl_ref[...], ymn], axis=1)
    o_ref[...] = h_ref[...] + jnp.dot(y, wo_ref[...], preferred_element_type=jnp.float32)


def _outproj(hp, yl, ym, mg, wo, *, tm):
    B, Tp, D = hp.shape
    row = lambda w: pl.BlockSpec((None, tm, w), lambda b, i: (b, i, 0))
    full = lambda a: pl.BlockSpec(a.shape, lambda b, i: (0,) * a.ndim)
    return pl.pallas_call(
        _outproj_kernel,
        grid=(B, Tp // tm),
        in_specs=[row(D), row(LRU_WIDTH), row(LRU_WIDTH), full(mg), full(wo)],
        out_specs=row(D),
        out_shape=jax.ShapeDtypeStruct((B, Tp, D), jnp.float32),
        compiler_params=pltpu.CompilerParams(
            dimension_semantics=("arbitrary", "arbitrary"), vmem_limit_bytes=VMEM_LIMIT),
        name="outproj",
    )(hp, yl, ym, mg, wo)


def _ffn_kernel(h_ref, hprev_ref, hnext_ref, g_ref, wg_ref, wu_ref, cw_ref, cb_ref, wd_ref,
                o_ref, gate_ref, *, tm, nt):
    i = pl.program_id(1)
    h = h_ref[...]
    hnext = jnp.where(i == nt - 1, 0.0, hnext_ref[...])
    hw = jnp.concatenate([hprev_ref[...], h, hnext], axis=0)
    hn = _rms(hw, g_ref[...]).astype(jnp.bfloat16)
    gate_ref[...] = jnp.dot(hn, wg_ref[...], preferred_element_type=jnp.float32)
    up = jnp.dot(hn[SUB:SUB + tm], wu_ref[...], preferred_element_type=jnp.float32)
    left = FFN_CONV // 2
    gc = cb_ref[...] + gate_ref[SUB - left:SUB - left + tm, :] * cw_ref[0:1, :]
    for k in range(1, FFN_CONV):
        gc = gc + gate_ref[SUB - left + k:SUB - left + k + tm, :] * cw_ref[k:k + 1, :]
    act = (gc * jax.nn.sigmoid(gc) * up).astype(jnp.bfloat16)
    o_ref[...] = h + jnp.dot(act, wd_ref[...], preferred_element_type=jnp.float32)


def _ffn(h1, g, wg, wu, cw, cb, wd, *, tm, seq):
    B, Tp, D = h1.shape
    nt = seq // tm
    off = X0 // tm
    per = tm // SUB
    nsub = Tp // SUB
    full = lambda a: pl.BlockSpec(a.shape, lambda b, i: (0,) * a.ndim)
    return pl.pallas_call(
        functools.partial(_ffn_kernel, tm=tm, nt=nt),
        grid=(B, nt),
        in_specs=[pl.BlockSpec((None, tm, D), lambda b, i: (b, i + off, 0)),
                  pl.BlockSpec((None, SUB, D), lambda b, i: (b, (i + off) * per - 1, 0)),
                  pl.BlockSpec((None, SUB, D),
                               lambda b, i: (b, jnp.minimum((i + off + 1) * per, nsub - 1), 0)),
                  full(g), full(wg), full(wu), full(cw), full(cb), full(wd)],
        out_specs=pl.BlockSpec((None, tm, D), lambda b, i: (b, i, 0)),
        out_shape=jax.ShapeDtypeStruct((B, seq, D), jnp.float32),
        scratch_shapes=[pltpu.VMEM((tm + 2 * SUB, D_FF), jnp.float32)],
        compiler_params=pltpu.CompilerParams(
            dimension_semantics=("arbitrary", "arbitrary"), vmem_limit_bytes=VMEM_LIMIT),
        name="ffn",
    )(h1, h1, h1, g, wg, wu, cw, cb, wd)


def _pad_heads(w, width, offset=0):
    K = w.shape[0]
    w = w.reshape(K, MLA_HEADS, width)
    w = jnp.pad(w, ((0, 0), (0, 0), (offset, HP - width - offset)))
    return w.reshape(K, MLA_HEADS * HP)


def _block_diag(w):
    H, Dh, _ = w.shape
    eye = jnp.eye(H, dtype=w.dtype)
    return (eye[:, None, :, None] * w[:, :, None, :]).reshape(H * Dh, H * Dh)


def _gate_weights(w_a, w_x):
    half = LRU_WIDTH // 2
    wa = _block_diag(w_a)
    wx = _block_diag(w_x)
    halves = [jnp.concatenate([wa[c * half:(c + 1) * half, c * half:(c + 1) * half],
                               wx[c * half:(c + 1) * half, c * half:(c + 1) * half]], axis=1)
              for c in range(2)]
    return jnp.stack(halves).astype(jnp.bfloat16)


def kernel(x, meta_tokens, norm_mix_g, w_in, conv_lru_w, conv_lru_b, lru_w_a, lru_b_a, lru_w_x,
           lru_b_x, lru_lambda, lru_gate_g, q_latent_g, w_uq, kv_latent_g, w_ukv, q_norm_g,
           k_norm_g, mla_out_g, w_out, norm_ffn_g, w_ffn_up, conv_ffn_w, conv_ffn_b, w_ffn_down):
    B, S, D = x.shape
    f32, bf16 = jnp.float32, jnp.bfloat16
    Tp = X0 + S
    l = 0
    r2 = lambda a: a.reshape(1, -1).astype(f32)

    head = jnp.concatenate([jnp.zeros((PAD, D), x.dtype), meta_tokens.astype(x.dtype)], axis=0)
    hp = jnp.concatenate([jnp.broadcast_to(head[None], (B, X0, D)), x], axis=1)

    o = 2 * LRU_WIDTH + Q_LORA + KV_LORA
    win = jnp.concatenate(
        [w_in[l][:, :o], jnp.pad(w_in[l][:, o:], ((0, 0), (QK_NOPE, LANE - QK_DIM)))],
        axis=1).astype(bf16)
    wuq = _pad_heads(w_uq[l], QK_DIM).astype(bf16)
    wkv = w_ukv[l].reshape(KV_LORA, MLA_HEADS, QK_NOPE + V_DIM)
    wuk = _pad_heads(wkv[:, :, :QK_NOPE].reshape(KV_LORA, -1), QK_NOPE).astype(bf16)
    wv = jnp.pad(wkv[:, :, QK_NOPE:], ((0, 0), (0, 0), (0, HP - V_DIM)))
    wv = jnp.where((jnp.arange(MLA_HEADS) % 2 == 1)[None, :, None],
                   jnp.roll(wv, V_DIM, axis=2), wv)
    wuv = wv.reshape(KV_LORA, MLA_HEADS * HP).astype(bf16)
    qg = jnp.pad(q_norm_g[l], (0, HP - QK_DIM)).reshape(1, HP)
    kg = jnp.pad(k_norm_g[l], (0, HP - QK_DIM)).reshape(1, HP)

    pos = jnp.arange(Tp, dtype=f32) - PAD
    inv_freq = ROPE_THETA ** (-jnp.arange(0, QK_ROPE, 2, dtype=f32) / QK_ROPE)
    ang = pos[:, None] * inv_freq[None, :]
    cos, sin = jnp.cos(ang), jnp.sin(ang)
    half = QK_ROPE // 2
    z = lambda w: jnp.zeros((Tp, w), f32)
    ctab = jnp.concatenate([jnp.ones((Tp, QK_NOPE), f32), cos, cos,
                            jnp.ones((Tp, HP - QK_DIM), f32)], axis=1)
    s1tab = jnp.concatenate([z(QK_NOPE), -sin, z(HP - QK_NOPE - half)], axis=1)
    s2tab = jnp.concatenate([z(QK_NOPE + half), sin, z(HP - QK_DIM)], axis=1)

    xl, gl, q, k, v = _inproj(hp, r2(norm_mix_g[l]), win, r2(q_latent_g[l]), wuq,
                              r2(kv_latent_g[l]), wuk, wuv, qg, kg, ctab, s1tab, s2tab, tm=768)

    cw, cb = conv_lru_w[l], r2(conv_lru_b[l])
    wg = [_gate_weights(lru_w_a[l, d], lru_w_x[l, d]) for d in range(2)]
    hf = _lru_fwd(xl, cw, cb, wg[0], r2(lru_b_a[l, 0]), r2(lru_b_x[l, 0]), r2(lru_lambda[l, 0]),
                  tt=256)
    yl = _lru_bwd(xl, cw, cb, wg[1], r2(lru_b_a[l, 1]), r2(lru_b_x[l, 1]), r2(lru_lambda[l, 1]),
                  hf, gl, r2(lru_gate_g[l]), tt=256)

    ym = _attn(q, k, v, tq=256, tk=768)

    h1 = _outproj(hp, yl, ym, r2(mla_out_g[l]), w_out[l].astype(bf16), tm=768)

    wup = w_ffn_up[l].astype(bf16)
    return _ffn(h1, r2(norm_ffn_g[l]), wup[:, :D_FF], wup[:, D_FF:], conv_ffn_w[l],
                r2(conv_ffn_b[l]), w_ffn_down[l].astype(bf16), tm=256, seq=S)
```

```python
import functools
import math

import jax
import jax.numpy as jnp
from jax import lax
from jax.experimental import pallas as pl
from jax.experimental.pallas import tpu as pltpu

D_MODEL = 1024
N_META = 16
LRU_WIDTH = 512
LRU_HEADS = 8
LRU_HEAD_DIM = 64
LRU_CONV = 4
LRU_C = 8.0
MLA_HEADS = 8
QK_NOPE = 64
QK_ROPE = 32
QK_DIM = 96
V_DIM = 64
Q_LORA = 256
KV_LORA = 128
ROPE_THETA = 10000.0
D_FF = 2816
FFN_CONV = 3
EPS = 1e-6
NEG_INF = -1e30

LANE = 128
SUB = 8
X0 = 256
PAD = X0 - N_META
HP = LANE
MASK_LANE = QK_DIM
IN_COLS_P = 2 * LRU_WIDTH + Q_LORA + KV_LORA + LANE
VMEM_LIMIT = 56 * 1024 * 1024


def _rms(x, g):
    return x * lax.rsqrt(jnp.mean(x * x, axis=-1, keepdims=True) + EPS) * g


def _bdot(a, b):
    return jnp.dot(a.astype(jnp.bfloat16), b, preferred_element_type=jnp.float32)


def _inproj_kernel(h_ref, gmix_ref, win_ref, qlg_ref, wuq_ref, kvlg_ref, wuk_ref, wuv_ref,
                   qg_ref, kg_ref, c_ref, s1_ref, s2_ref,
                   xlru_ref, glru_ref, q_ref, k_ref, v_ref, *, tm):
    i = pl.program_id(1)
    hn = _rms(h_ref[...], gmix_ref[...])
    proj = _bdot(hn, win_ref[...])
    xlru_ref[...] = proj[:, :LRU_WIDTH]
    glru_ref[...] = proj[:, LRU_WIDTH:2 * LRU_WIDTH]
    o = 2 * LRU_WIDTH
    cq = proj[:, o:o + Q_LORA]
    ckv = proj[:, o + Q_LORA:o + Q_LORA + KV_LORA]
    kpe = proj[:, o + Q_LORA + KV_LORA:]
    qraw = _bdot(_rms(cq, qlg_ref[...]), wuq_ref[...])
    ckvn = _rms(ckv, kvlg_ref[...]).astype(jnp.bfloat16)
    kraw = jnp.dot(ckvn, wuk_ref[...], preferred_element_type=jnp.float32)
    vraw = jnp.dot(ckvn, wuv_ref[...], preferred_element_type=jnp.float32)

    cosv, sin1, sin2 = c_ref[...], s1_ref[...], s2_ref[...]
    lane = lax.broadcasted_iota(jnp.int32, (tm, HP), 1)
    row = lax.broadcasted_iota(jnp.int32, (tm, HP), 0) + i * tm
    key_mask = jnp.where(row >= PAD, 0.0, NEG_INF)
    scale = math.log2(math.e) / math.sqrt(QK_DIM)
    half = QK_ROPE // 2

    def head_norm_rope(xb, g):
        ss = jnp.sum(xb * xb, axis=-1, keepdims=True) * (1.0 / QK_DIM)
        xb = xb * lax.rsqrt(ss + EPS) * g
        return (xb * cosv + pltpu.roll(xb, HP - half, axis=1) * sin1
                + pltpu.roll(xb, half, axis=1) * sin2)

    for hd in range(MLA_HEADS):
        sl = slice(hd * HP, (hd + 1) * HP)
        qb = head_norm_rope(qraw[:, sl], qg_ref[...]) * scale
        q_ref[:, sl] = jnp.where(lane == MASK_LANE, 1.0, qb).astype(jnp.bfloat16)
        kb = head_norm_rope(kraw[:, sl] + kpe, kg_ref[...])
        k_ref[:, sl] = jnp.where(lane == MASK_LANE, key_mask, kb).astype(jnp.bfloat16)
        ones_lane = V_DIM if hd % 2 == 0 else 0
        v_ref[:, sl] = jnp.where(lane == ones_lane, 1.0, vraw[:, sl]).astype(jnp.bfloat16)


def _inproj(hp, gmix, win, qlg, wuq, kvlg, wuk, wuv, qg, kg, ctab, s1tab, s2tab, *, tm):
    B, Tp, D = hp.shape
    nt = Tp // tm
    row = lambda w: pl.BlockSpec((None, tm, w), lambda b, i: (b, i, 0))
    full = lambda a: pl.BlockSpec(a.shape, lambda b, i: (0,) * a.ndim)
    tab = pl.BlockSpec((tm, HP), lambda b, i: (i, 0))
    f32, bf16 = jnp.float32, jnp.bfloat16
    return pl.pallas_call(
        functools.partial(_inproj_kernel, tm=tm),
        grid=(B, nt),
        in_specs=[row(D), full(gmix), full(win), full(qlg), full(wuq), full(kvlg), full(wuk),
                  full(wuv), full(qg), full(kg), tab, tab, tab],
        out_specs=[row(LRU_WIDTH), row(LRU_WIDTH), row(MLA_HEADS * HP), row(MLA_HEADS * HP),
                   row(MLA_HEADS * HP)],
        out_shape=[jax.ShapeDtypeStruct((B, Tp, LRU_WIDTH), f32),
                   jax.ShapeDtypeStruct((B, Tp, LRU_WIDTH), f32),
                   jax.ShapeDtypeStruct((B, Tp, MLA_HEADS * HP), bf16),
                   jax.ShapeDtypeStruct((B, Tp, MLA_HEADS * HP), bf16),
                   jax.ShapeDtypeStruct((B, Tp, MLA_HEADS * HP), bf16)],
        compiler_params=pltpu.CompilerParams(
            dimension_semantics=("arbitrary", "arbitrary"), vmem_limit_bytes=VMEM_LIMIT),
        name="inproj",
    )(hp, gmix, win, qlg, wuq, kvlg, wuk, wuv, qg, kg, ctab, s1tab, s2tab)


def _lru_gates(x_ref, xprev_ref, xnext_ref, cw_ref, cb_ref, wg_ref, ba_ref, bx_ref, lam_ref,
               xw_ref, a_ref, u_ref, *, tt, first, last, t0):
    W = LRU_WIDTH
    xw_ref[0:SUB, :] = jnp.where(first, 0.0, xprev_ref[...])
    xw_ref[SUB:SUB + tt, :] = x_ref[...]
    xw_ref[SUB + tt:, :] = jnp.where(last, 0.0, xnext_ref[...])
    left = LRU_CONV // 2
    xc = cb_ref[...] + xw_ref[SUB - left:SUB - left + tt, :] * cw_ref[0:1, :]
    for k in range(1, LRU_CONV):
        xc = xc + xw_ref[SUB - left + k:SUB - left + k + tt, :] * cw_ref[k:k + 1, :]
    xcb = xc.astype(jnp.bfloat16)
    half = W // 2
    g0 = jnp.dot(xcb[:, :half], wg_ref[0], preferred_element_type=jnp.float32)
    g1 = jnp.dot(xcb[:, half:], wg_ref[1], preferred_element_type=jnp.float32)
    ra = jnp.concatenate([g0[:, :half], g1[:, :half]], axis=1) + ba_ref[...]
    ri = jnp.concatenate([g0[:, half:], g1[:, half:]], axis=1) + bx_ref[...]
    r = jax.nn.sigmoid(ra)
    ig = jax.nn.sigmoid(ri)
    lam = lam_ref[...]
    sp = jnp.maximum(-lam, 0.0) + jnp.log(1.0 + jnp.exp(-jnp.abs(lam)))
    log_a = -LRU_C * r * sp
    a = jnp.exp(log_a)
    u = jnp.sqrt(1.0 - a * a) * (ig * xc)
    rows = lax.broadcasted_iota(jnp.int32, (tt, W), 0) + t0
    a_ref[...] = a
    u_ref[...] = jnp.where(rows >= PAD, u, 0.0)


def _scan_groups(a_ref, u_ref, h_ref, carry_ref, *, tt, reverse):
    W = LRU_WIDTH
    ng = tt // SUB
    sub = lax.broadcasted_iota(jnp.int32, (SUB, W), 0)

    def body(g, carry):
        gi = (ng - 1 - g) if reverse else g
        r0 = pl.multiple_of(gi * SUB, SUB)
        a = a_ref[pl.ds(r0, SUB), :]
        u = u_ref[pl.ds(r0, SUB), :]
        for d in (1, 2, 4):
            if reverse:
                keep = sub < SUB - d
                sh = SUB - d
            else:
                keep = sub >= d
                sh = d
            a_sh = jnp.where(keep, pltpu.roll(a, sh, axis=0), 1.0)
            u_sh = jnp.where(keep, pltpu.roll(u, sh, axis=0), 0.0)
            u = u + a * u_sh
            a = a * a_sh
        h = u + a * carry
        h_ref[pl.ds(r0, SUB), :] = h
        return h[0:1, :] if reverse else h[SUB - 1:SUB, :]

    carry_ref[...] = lax.fori_loop(0, ng, body, carry_ref[...])


def _lru_fwd_kernel(x_ref, xprev_ref, xnext_ref, cw_ref, cb_ref, wg_ref, ba_ref, bx_ref, lam_ref,
                    hf_ref, xw_ref, a_ref, u_ref, carry_ref, *, tt, nt):
    i = pl.program_id(1)

    @pl.when(i == 0)
    def _():
        carry_ref[...] = jnp.zeros_like(carry_ref)

    _lru_gates(x_ref, xprev_ref, xnext_ref, cw_ref, cb_ref, wg_ref, ba_ref, bx_ref, lam_ref,
               xw_ref, a_ref, u_ref, tt=tt, first=i == 0, last=i == nt - 1, t0=i * tt)
    _scan_groups(a_ref, u_ref, hf_ref, carry_ref, tt=tt, reverse=False)


def _lru_bwd_kernel(x_ref, xprev_ref, xnext_ref, cw_ref, cb_ref, wg_ref, ba_ref, bx_ref, lam_ref,
                    hf_ref, gl_ref, gg_ref, y_ref, xw_ref, a_ref, u_ref, hb_ref, carry_ref,
                    *, tt, nt):
    j = pl.program_id(1)
    i = nt - 1 - j

    @pl.when(j == 0)
    def _():
        carry_ref[...] = jnp.zeros_like(carry_ref)

    _lru_gates(x_ref, xprev_ref, xnext_ref, cw_ref, cb_ref, wg_ref, ba_ref, bx_ref, lam_ref,
               xw_ref, a_ref, u_ref, tt=tt, first=i == 0, last=i == nt - 1, t0=i * tt)
    _scan_groups(a_ref, u_ref, hb_ref, carry_ref, tt=tt, reverse=True)
    y = (hf_ref[...] + hb_ref[...]) * jax.nn.gelu(gl_ref[...], approximate=True)
    y_ref[...] = _rms(y, gg_ref[...]).astype(y_ref.dtype)


def _lru_specs(tt, nt, Tp, tmap):
    W = LRU_WIDTH
    nsub = Tp // SUB
    per = tt // SUB
    blk = pl.BlockSpec((None, tt, W), lambda b, j: (b, tmap(j), 0))
    prev = pl.BlockSpec((None, SUB, W), lambda b, j: (b, jnp.maximum(tmap(j) * per - 1, 0), 0))
    nxt = pl.BlockSpec((None, SUB, W),
                       lambda b, j: (b, jnp.minimum((tmap(j) + 1) * per, nsub - 1), 0))
    return blk, prev, nxt


def _lru_fwd(xl, cw, cb, wg, ba, bx, lam, *, tt):
    B, Tp, W = xl.shape
    nt = Tp // tt
    blk, prev, nxt = _lru_specs(tt, nt, Tp, lambda j: j)
    full = lambda a: pl.BlockSpec(a.shape, lambda b, j: (0,) * a.ndim)
    return pl.pallas_call(
        functools.partial(_lru_fwd_kernel, tt=tt, nt=nt),
        grid=(B, nt),
        in_specs=[blk, prev, nxt, full(cw), full(cb), full(wg), full(ba), full(bx), full(lam)],
        out_specs=blk,
        out_shape=jax.ShapeDtypeStruct((B, Tp, W), jnp.float32),
        scratch_shapes=[pltpu.VMEM((tt + 2 * SUB, W), jnp.float32),
                        pltpu.VMEM((tt, W), jnp.float32),
                        pltpu.VMEM((tt, W), jnp.float32),
                        pltpu.VMEM((1, W), jnp.float32)],
        compiler_params=pltpu.CompilerParams(
            dimension_semantics=("arbitrary", "arbitrary"), vmem_limit_bytes=VMEM_LIMIT),
        name="lru_fwd",
    )(xl, xl, xl, cw, cb, wg, ba, bx, lam)


def _lru_bwd(xl, cw, cb, wg, ba, bx, lam, hf, gl, gg, *, tt):
    B, Tp, W = xl.shape
    nt = Tp // tt
    blk, prev, nxt = _lru_specs(tt, nt, Tp, lambda j: nt - 1 - j)
    full = lambda a: pl.BlockSpec(a.shape, lambda b, j: (0,) * a.ndim)
    return pl.pallas_call(
        functools.partial(_lru_bwd_kernel, tt=tt, nt=nt),
        grid=(B, nt),
        in_specs=[blk, prev, nxt, full(cw), full(cb), full(wg), full(ba), full(bx), full(lam),
                  blk, blk, full(gg)],
        out_specs=blk,
        out_shape=jax.ShapeDtypeStruct((B, Tp, W), jnp.bfloat16),
        scratch_shapes=[pltpu.VMEM((tt + 2 * SUB, W), jnp.float32),
                        pltpu.VMEM((tt, W), jnp.float32),
                        pltpu.VMEM((tt, W), jnp.float32),
                        pltpu.VMEM((tt, W), jnp.float32),
                        pltpu.VMEM((1, W), jnp.float32)],
        compiler_params=pltpu.CompilerParams(
            dimension_semantics=("arbitrary", "arbitrary"), vmem_limit_bytes=VMEM_LIMIT),
        name="lru_bwd",
    )(xl, xl, xl, cw, cb, wg, ba, bx, lam, hf, gl, gg)


def _attn_kernel(q_ref, k_ref, v_ref, o_ref, *, tq):
    heads = [slice(hh * HP, (hh + 1) * HP) for hh in range(2)]
    scores = [lax.dot_general(q_ref[:, sl], k_ref[:, sl], (((1,), (1,)), ((), ())),
                              preferred_element_type=jnp.float32) for sl in heads]
    accs = []
    for s, sl in zip(scores, heads):
        m = jnp.max(s, axis=-1, keepdims=True)
        p = jnp.exp2(s - m).astype(jnp.bfloat16)
        accs.append(jnp.dot(p, v_ref[:, sl], preferred_element_type=jnp.float32))
    a0, a1 = accs
    lane = lax.broadcasted_iota(jnp.int32, (tq, HP), 1)
    o0 = a0 / a0[:, V_DIM:V_DIM + 1]
    o1 = a1 / a1[:, 0:1]
    o_ref[...] = jnp.where(lane < V_DIM, o0, o1)


def _attn(q, k, v, *, tq):
    B, Tp, _ = q.shape
    npair = MLA_HEADS // 2
    return pl.pallas_call(
        functools.partial(_attn_kernel, tq=tq),
        grid=(B, npair, Tp // tq),
        in_specs=[pl.BlockSpec((None, tq, 2 * HP), lambda b, p, i: (b, i, p)),
                  pl.BlockSpec((None, Tp, 2 * HP), lambda b, p, i: (b, 0, p)),
                  pl.BlockSpec((None, Tp, 2 * HP), lambda b, p, i: (b, 0, p))],
        out_specs=pl.BlockSpec((None, tq, HP), lambda b, p, i: (b, i, p)),
        out_shape=jax.ShapeDtypeStruct((B, Tp, npair * HP), jnp.float32),
        compiler_params=pltpu.CompilerParams(
            dimension_semantics=("arbitrary", "arbitrary", "arbitrary"),
            vmem_limit_bytes=VMEM_LIMIT),
        name="attn",
    )(q, k, v)


def _outproj_kernel(h_ref, yl_ref, ym_ref, mg_ref, wo_ref, o_ref):
    ymn = _rms(ym_ref[...], mg_ref[...]).astype(jnp.bfloat16)
    y = jnp.concatenate([yl_ref[...], ymn], axis=1)
    o_ref[...] = h_ref[...] + jnp.dot(y, wo_ref[...], preferred_element_type=jnp.float32)


def _outproj(hp, yl, ym, mg, wo, *, tm):
    B, Tp, D = hp.shape
    row = lambda w: pl.BlockSpec((None, tm, w), lambda b, i: (b, i, 0))
    full = lambda a: pl.BlockSpec(a.shape, lambda b, i: (0,) * a.ndim)
    return pl.pallas_call(
        _outproj_kernel,
        grid=(B, Tp // tm),
        in_specs=[row(D), row(LRU_WIDTH), row(LRU_WIDTH), full(mg), full(wo)],
        out_specs=row(D),
        out_shape=jax.ShapeDtypeStruct((B, Tp, D), jnp.float32),
        compiler_params=pltpu.CompilerParams(
            dimension_semantics=("arbitrary", "arbitrary"), vmem_limit_bytes=VMEM_LIMIT),
        name="outproj",
    )(hp, yl, ym, mg, wo)


def _ffn_kernel(h_ref, hprev_ref, hnext_ref, g_ref, wg_ref, wu_ref, cw_ref, cb_ref, wd_ref,
                o_ref, gate_ref, *, tm, nt):
    i = pl.program_id(1)
    h = h_ref[...]
    hnext = jnp.where(i == nt - 1, 0.0, hnext_ref[...])
    hw = jnp.concatenate([hprev_ref[...], h, hnext], axis=0)
    hn = _rms(hw, g_ref[...]).astype(jnp.bfloat16)
    gate_ref[...] = jnp.dot(hn, wg_ref[...], preferred_element_type=jnp.float32)
    up = jnp.dot(hn[SUB:SUB + tm], wu_ref[...], preferred_element_type=jnp.float32)
    left = FFN_CONV // 2
    gc = cb_ref[...] + gate_ref[SUB - left:SUB - left + tm, :] * cw_ref[0:1, :]
    for k in range(1, FFN_CONV):
        gc = gc + gate_ref[SUB - left + k:SUB - left + k + tm, :] * cw_ref[k:k + 1, :]
    act = (gc * jax.nn.sigmoid(gc) * up).astype(jnp.bfloat16)
    o_ref[...] = h + jnp.dot(act, wd_ref[...], preferred_element_type=jnp.float32)


def _ffn(h1, g, wg, wu, cw, cb, wd, *, tm, seq):
    B, Tp, D = h1.shape
    nt = seq // tm
    off = X0 // tm
    per = tm // SUB
    nsub = Tp // SUB
    full = lambda a: pl.BlockSpec(a.shape, lambda b, i: (0,) * a.ndim)
    return pl.pallas_call(
        functools.partial(_ffn_kernel, tm=tm, nt=nt),
        grid=(B, nt),
        in_specs=[pl.BlockSpec((None, tm, D), lambda b, i: (b, i + off, 0)),
                  pl.BlockSpec((None, SUB, D), lambda b, i: (b, (i + off) * per - 1, 0)),
                  pl.BlockSpec((None, SUB, D),
                               lambda b, i: (b, jnp.minimum((i + off + 1) * per, nsub - 1), 0)),
                  full(g), full(wg), full(wu), full(cw), full(cb), full(wd)],
        out_specs=pl.BlockSpec((None, tm, D), lambda b, i: (b, i, 0)),
        out_shape=jax.ShapeDtypeStruct((B, seq, D), jnp.float32),
        scratch_shapes=[pltpu.VMEM((tm + 2 * SUB, D_FF), jnp.float32)],
        compiler_params=pltpu.CompilerParams(
            dimension_semantics=("arbitrary", "arbitrary"), vmem_limit_bytes=VMEM_LIMIT),
        name="ffn",
    )(h1, h1, h1, g, wg, wu, cw, cb, wd)


def _pad_heads(w, width, offset=0):
    K = w.shape[0]
    w = w.reshape(K, MLA_HEADS, width)
    w = jnp.pad(w, ((0, 0), (0, 0), (offset, HP - width - offset)))
    return w.reshape(K, MLA_HEADS * HP)


def _block_diag(w):
    H, Dh, _ = w.shape
    eye = jnp.eye(H, dtype=w.dtype)
    return (eye[:, None, :, None] * w[:, :, None, :]).reshape(H * Dh, H * Dh)


def _gate_weights(w_a, w_x):
    half = LRU_WIDTH // 2
    wa = _block_diag(w_a)
    wx = _block_diag(w_x)
    halves = [jnp.concatenate([wa[c * half:(c + 1) * half, c * half:(c + 1) * half],
                               wx[c * half:(c + 1) * half, c * half:(c + 1) * half]], axis=1)
              for c in range(2)]
    return jnp.stack(halves).astype(jnp.bfloat16)


def kernel(x, meta_tokens, norm_mix_g, w_in, conv_lru_w, conv_lru_b, lru_w_a, lru_b_a, lru_w_x,
           lru_b_x, lru_lambda, lru_gate_g, q_latent_g, w_uq, kv_latent_g, w_ukv, q_norm_g,
           k_norm_g, mla_out_g, w_out, norm_ffn_g, w_ffn_up, conv_ffn_w, conv_ffn_b, w_ffn_down):
    B, S, D = x.shape
    f32, bf16 = jnp.float32, jnp.bfloat16
    Tp = X0 + S
    l = 0
    r2 = lambda a: a.reshape(1, -1).astype(f32)

    head = jnp.concatenate([jnp.zeros((PAD, D), x.dtype), meta_tokens.astype(x.dtype)], axis=0)
    hp = jnp.concatenate([jnp.broadcast_to(head[None], (B, X0, D)), x], axis=1)

    o = 2 * LRU_WIDTH + Q_LORA + KV_LORA
    win = jnp.concatenate(
        [w_in[l][:, :o], jnp.pad(w_in[l][:, o:], ((0, 0), (QK_NOPE, LANE - QK_DIM)))],
        axis=1).astype(bf16)
    wuq = _pad_heads(w_uq[l], QK_DIM).astype(bf16)
    wkv = w_ukv[l].reshape(KV_LORA, MLA_HEADS, QK_NOPE + V_DIM)
    wuk = _pad_heads(wkv[:, :, :QK_NOPE].reshape(KV_LORA, -1), QK_NOPE).astype(bf16)
    wv = jnp.pad(wkv[:, :, QK_NOPE:], ((0, 0), (0, 0), (0, HP - V_DIM)))
    wv = jnp.where((jnp.arange(MLA_HEADS) % 2 == 1)[None, :, None],
                   jnp.roll(wv, V_DIM, axis=2), wv)
    wuv = wv.reshape(KV_LORA, MLA_HEADS * HP).astype(bf16)
    qg = jnp.pad(q_norm_g[l], (0, HP - QK_DIM)).reshape(1, HP)
    kg = jnp.pad(k_norm_g[l], (0, HP - QK_DIM)).reshape(1, HP)

    pos = jnp.arange(Tp, dtype=f32) - PAD
    inv_freq = ROPE_THETA ** (-jnp.arange(0, QK_ROPE, 2, dtype=f32) / QK_ROPE)
    ang = pos[:, None] * inv_freq[None, :]
    cos, sin = jnp.cos(ang), jnp.sin(ang)
    half = QK_ROPE // 2
    z = lambda w: jnp.zeros((Tp, w), f32)
    ctab = jnp.concatenate([jnp.ones((Tp, QK_NOPE), f32), cos, cos,
                            jnp.ones((Tp, HP - QK_DIM), f32)], axis=1)
    s1tab = jnp.concatenate([z(QK_NOPE), -sin, z(HP - QK_NOPE - half)], axis=1)
    s2tab = jnp.concatenate([z(QK_NOPE + half), sin, z(HP - QK_DIM)], axis=1)

    xl, gl, q, k, v = _inproj(hp, r2(norm_mix_g[l]), win, r2(q_latent_g[l]), wuq,
                              r2(kv_latent_g[l]), wuk, wuv, qg, kg, ctab, s1tab, s2tab, tm=768)

    cw, cb = conv_lru_w[l], r2(conv_lru_b[l])
    wg = [_gate_weights(lru_w_a[l, d], lru_w_x[l, d]) for d in range(2)]
    hf = _lru_fwd(xl, cw, cb, wg[0], r2(lru_b_a[l, 0]), r2(lru_b_x[l, 0]), r2(lru_lambda[l, 0]),
                  tt=256)
    yl = _lru_bwd(xl, cw, cb, wg[1], r2(lru_b_a[l, 1]), r2(lru_b_x[l, 1]), r2(lru_lambda[l, 1]),
                  hf, gl, r2(lru_gate_g[l]), tt=256)

    ym = _attn(q, k, v, tq=256)

    h1 = _outproj(hp, yl, ym, r2(mla_out_g[l]), w_out[l].astype(bf16), tm=768)

    wup = w_ffn_up[l].astype(bf16)
    return _ffn(h1, r2(norm_ffn_g[l]), wup[:, :D_FF], wup[:, D_FF:], conv_ffn_w[l],
                r2(conv_ffn_b[l]), w_ffn_down[l].astype(bf16), tm=256, seq=S)
```

```python
import functools
import math

import jax
import jax.numpy as jnp
from jax import lax
from jax.experimental import pallas as pl
from jax.experimental.pallas import tpu as pltpu

D_MODEL = 1024
N_META = 16
LRU_WIDTH = 512
LRU_HEADS = 8
LRU_HEAD_DIM = 64
LRU_CONV = 4
LRU_C = 8.0
MLA_HEADS = 8
QK_NOPE = 64
QK_ROPE = 32
QK_DIM = 96
V_DIM = 64
Q_LORA = 256
KV_LORA = 128
ROPE_THETA = 10000.0
D_FF = 2816
FFN_CONV = 3
EPS = 1e-6
NEG_INF = -1e30

LANE = 128
SUB = 8
X0 = 256
PAD = X0 - N_META
HP = LANE
MASK_LANE = QK_DIM
VMEM_LIMIT = 56 * 1024 * 1024


def _rms(x, g):
    return x * lax.rsqrt(jnp.mean(x * x, axis=-1, keepdims=True) + EPS) * g


def _bdot(a, b):
    return jnp.dot(a.astype(jnp.bfloat16), b, preferred_element_type=jnp.float32)


def _inproj_kernel(x_ref, head_ref, gmix_ref, win_ref, qlg_ref, wuq_ref, wuqs_ref, kvlg_ref,
                   wuk_ref, wuv_ref, qg_ref, qgs_ref, kg_ref, kgs_ref, ones_ref, c_ref, s_ref,
                   xlru_ref, glru_ref, q_ref, k_ref, v_ref, *, tm):
    i = pl.program_id(1)
    h = jnp.where(i == 0, head_ref[...], x_ref[...])
    hn = _rms(h, gmix_ref[...])
    proj = _bdot(hn, win_ref[...])
    xlru_ref[...] = proj[:, :LRU_WIDTH]
    glru_ref[...] = proj[:, LRU_WIDTH:2 * LRU_WIDTH]
    o = 2 * LRU_WIDTH
    cq = proj[:, o:o + Q_LORA]
    ckv = proj[:, o + Q_LORA:o + Q_LORA + KV_LORA]
    o += Q_LORA + KV_LORA
    kpe = proj[:, o:o + HP]
    kpe_sw = proj[:, o + HP:]
    cqn = _rms(cq, qlg_ref[...]).astype(jnp.bfloat16)
    qraw = jnp.dot(cqn, wuq_ref[...], preferred_element_type=jnp.float32)
    qraw_sw = jnp.dot(cqn, wuqs_ref[...], preferred_element_type=jnp.float32)
    ckvn = _rms(ckv, kvlg_ref[...]).astype(jnp.bfloat16)
    kraw = jnp.dot(ckvn, wuk_ref[...], preferred_element_type=jnp.float32)
    vraw = jnp.dot(ckvn, wuv_ref[...], preferred_element_type=jnp.float32)

    lane = lax.broadcasted_iota(jnp.int32, (tm, HP), 1)
    row = lax.broadcasted_iota(jnp.int32, (tm, HP), 0) + i * tm
    key_mask = jnp.where(row >= PAD, 0.0, NEG_INF)
    scale = math.log2(math.e) / math.sqrt(QK_DIM)
    cosv, sinv = c_ref[...], s_ref[...]
    q_cos = cosv * (qg_ref[...] * scale)
    q_sin = sinv * (qgs_ref[...] * scale)
    k_cos = cosv * kg_ref[...]
    k_sin_term = kpe_sw * (sinv * kgs_ref[...])

    def inv_rms(xb):
        x2 = xb * xb
        hi = x2.astype(jnp.bfloat16)
        lo = (x2 - hi.astype(jnp.float32)).astype(jnp.bfloat16)
        ss = jnp.dot(jnp.concatenate([hi, lo], axis=1), ones_ref[...],
                     preferred_element_type=jnp.float32)
        return lax.rsqrt(ss * (1.0 / QK_DIM) + EPS)

    for hd in range(MLA_HEADS):
        sl = slice(hd * HP, (hd + 1) * HP)
        xq = qraw[:, sl]
        qb = (xq * q_cos + qraw_sw[:, sl] * q_sin) * inv_rms(xq)
        q_ref[:, sl] = jnp.where(lane == MASK_LANE, 1.0, qb).astype(jnp.bfloat16)
        xk = kraw[:, sl] + kpe
        kb = (xk * k_cos + k_sin_term) * inv_rms(xk)
        k_ref[:, sl] = jnp.where(lane == MASK_LANE, key_mask, kb).astype(jnp.bfloat16)
        ones_lane = V_DIM if hd % 2 == 0 else 0
        v_ref[:, sl] = jnp.where(lane == ones_lane, 1.0, vraw[:, sl]).astype(jnp.bfloat16)


def _x_tile_spec(tm, D):
    return pl.BlockSpec((None, tm, D), lambda b, i: (b, jnp.maximum(i - 1, 0), 0))


def _inproj(x, head, gmix, win, qlg, wuq, wuqs, kvlg, wuk, wuv, qg, qgs, kg, kgs, ones, ctab,
            stab, *, tm):
    B, S, D = x.shape
    assert tm == X0
    Tp = X0 + S
    nt = Tp // tm
    row = lambda w: pl.BlockSpec((None, tm, w), lambda b, i: (b, i, 0))
    full = lambda a: pl.BlockSpec(a.shape, lambda b, i: (0,) * a.ndim)
    tab = pl.BlockSpec((tm, HP), lambda b, i: (i, 0))
    f32, bf16 = jnp.float32, jnp.bfloat16
    return pl.pallas_call(
        functools.partial(_inproj_kernel, tm=tm),
        grid=(B, nt),
        in_specs=[_x_tile_spec(tm, D), full(head), full(gmix), full(win), full(qlg), full(wuq),
                  full(wuqs), full(kvlg), full(wuk), full(wuv), full(qg), full(qgs), full(kg),
                  full(kgs), full(ones), tab, tab],
        out_specs=[row(LRU_WIDTH), row(LRU_WIDTH), row(MLA_HEADS * HP), row(MLA_HEADS * HP),
                   row(MLA_HEADS * HP)],
        out_shape=[jax.ShapeDtypeStruct((B, Tp, LRU_WIDTH), f32),
                   jax.ShapeDtypeStruct((B, Tp, LRU_WIDTH), f32),
                   jax.ShapeDtypeStruct((B, Tp, MLA_HEADS * HP), bf16),
                   jax.ShapeDtypeStruct((B, Tp, MLA_HEADS * HP), bf16),
                   jax.ShapeDtypeStruct((B, Tp, MLA_HEADS * HP), bf16)],
        compiler_params=pltpu.CompilerParams(
            dimension_semantics=("arbitrary", "arbitrary"), vmem_limit_bytes=VMEM_LIMIT),
        name="inproj",
    )(x, head, gmix, win, qlg, wuq, wuqs, kvlg, wuk, wuv, qg, qgs, kg, kgs, ones, ctab, stab)


def _lru_gates(x_ref, xprev_ref, xnext_ref, cw_ref, cb_ref, wg_ref, ba_ref, bx_ref, lam_ref,
               xw_ref, a_ref, u_ref, *, tt, first, last, t0):
    W = LRU_WIDTH
    xw_ref[0:SUB, :] = jnp.where(first, 0.0, xprev_ref[...])
    xw_ref[SUB:SUB + tt, :] = x_ref[...]
    xw_ref[SUB + tt:, :] = jnp.where(last, 0.0, xnext_ref[...])
    left = LRU_CONV // 2
    xc = cb_ref[...] + xw_ref[SUB - left:SUB - left + tt, :] * cw_ref[0:1, :]
    for k in range(1, LRU_CONV):
        xc = xc + xw_ref[SUB - left + k:SUB - left + k + tt, :] * cw_ref[k:k + 1, :]
    xcb = xc.astype(jnp.bfloat16)
    half = W // 2
    g0 = jnp.dot(xcb[:, :half], wg_ref[0], preferred_element_type=jnp.float32)
    g1 = jnp.dot(xcb[:, half:], wg_ref[1], preferred_element_type=jnp.float32)
    ra = jnp.concatenate([g0[:, :half], g1[:, :half]], axis=1) + ba_ref[...]
    ri = jnp.concatenate([g0[:, half:], g1[:, half:]], axis=1) + bx_ref[...]
    r = jax.nn.sigmoid(ra)
    ig = jax.nn.sigmoid(ri)
    lam = lam_ref[...]
    sp = jnp.maximum(-lam, 0.0) + jnp.log(1.0 + jnp.exp(-jnp.abs(lam)))
    log_a = -LRU_C * r * sp
    a = jnp.exp(log_a)
    u = jnp.sqrt(1.0 - a * a) * (ig * xc)
    rows = lax.broadcasted_iota(jnp.int32, (tt, W), 0) + t0
    a_ref[...] = a
    u_ref[...] = jnp.where(rows >= PAD, u, 0.0)


def _scan_groups(a_ref, u_ref, h_ref, carry_ref, *, tt, reverse):
    W = LRU_WIDTH
    ng = tt // SUB
    sub = lax.broadcasted_iota(jnp.int32, (SUB, W), 0)

    def body(g, carry):
        gi = (ng - 1 - g) if reverse else g
        r0 = pl.multiple_of(gi * SUB, SUB)
        a = a_ref[pl.ds(r0, SUB), :]
        u = u_ref[pl.ds(r0, SUB), :]
        for d in (1, 2, 4):
            if reverse:
                keep = sub < SUB - d
                sh = SUB - d
            else:
                keep = sub >= d
                sh = d
            a_sh = jnp.where(keep, pltpu.roll(a, sh, axis=0), 1.0)
            u_sh = jnp.where(keep, pltpu.roll(u, sh, axis=0), 0.0)
            u = u + a * u_sh
            a = a * a_sh
        h = u + a * carry
        h_ref[pl.ds(r0, SUB), :] = h
        return h[0:1, :] if reverse else h[SUB - 1:SUB, :]

    carry_ref[...] = lax.fori_loop(0, ng, body, carry_ref[...])


def _lru_fwd_kernel(x_ref, xprev_ref, xnext_ref, cw_ref, cb_ref, wg_ref, ba_ref, bx_ref, lam_ref,
                    hf_ref, xw_ref, a_ref, u_ref, carry_ref, *, tt, nt):
    i = pl.program_id(1)

    @pl.when(i == 0)
    def _():
        carry_ref[...] = jnp.zeros_like(carry_ref)

    _lru_gates(x_ref, xprev_ref, xnext_ref, cw_ref, cb_ref, wg_ref, ba_ref, bx_ref, lam_ref,
               xw_ref, a_ref, u_ref, tt=tt, first=i == 0, last=i == nt - 1, t0=i * tt)
    _scan_groups(a_ref, u_ref, hf_ref, carry_ref, tt=tt, reverse=False)


def _lru_bwd_kernel(x_ref, xprev_ref, xnext_ref, cw_ref, cb_ref, wg_ref, ba_ref, bx_ref, lam_ref,
                    hf_ref, gl_ref, gg_ref, y_ref, xw_ref, a_ref, u_ref, hb_ref, carry_ref,
                    *, tt, nt):
    j = pl.program_id(1)
    i = nt - 1 - j

    @pl.when(j == 0)
    def _():
        carry_ref[...] = jnp.zeros_like(carry_ref)

    _lru_gates(x_ref, xprev_ref, xnext_ref, cw_ref, cb_ref, wg_ref, ba_ref, bx_ref, lam_ref,
               xw_ref, a_ref, u_ref, tt=tt, first=i == 0, last=i == nt - 1, t0=i * tt)
    _scan_groups(a_ref, u_ref, hb_ref, carry_ref, tt=tt, reverse=True)
    y = (hf_ref[...] + hb_ref[...]) * jax.nn.gelu(gl_ref[...], approximate=True)
    y_ref[...] = _rms(y, gg_ref[...]).astype(y_ref.dtype)


def _lru_specs(tt, nt, Tp, tmap):
    W = LRU_WIDTH
    nsub = Tp // SUB
    per = tt // SUB
    blk = pl.BlockSpec((None, tt, W), lambda b, j: (b, tmap(j), 0))
    prev = pl.BlockSpec((None, SUB, W), lambda b, j: (b, jnp.maximum(tmap(j) * per - 1, 0), 0))
    nxt = pl.BlockSpec((None, SUB, W),
                       lambda b, j: (b, jnp.minimum((tmap(j) + 1) * per, nsub - 1), 0))
    return blk, prev, nxt


def _lru_fwd(xl, cw, cb, wg, ba, bx, lam, *, tt):
    B, Tp, W = xl.shape
    nt = Tp // tt
    blk, prev, nxt = _lru_specs(tt, nt, Tp, lambda j: j)
    full = lambda a: pl.BlockSpec(a.shape, lambda b, j: (0,) * a.ndim)
    return pl.pallas_call(
        functools.partial(_lru_fwd_kernel, tt=tt, nt=nt),
        grid=(B, nt),
        in_specs=[blk, prev, nxt, full(cw), full(cb), full(wg), full(ba), full(bx), full(lam)],
        out_specs=blk,
        out_shape=jax.ShapeDtypeStruct((B, Tp, W), jnp.float32),
        scratch_shapes=[pltpu.VMEM((tt + 2 * SUB, W), jnp.float32),
                        pltpu.VMEM((tt, W), jnp.float32),
                        pltpu.VMEM((tt, W), jnp.float32),
                        pltpu.VMEM((1, W), jnp.float32)],
        compiler_params=pltpu.CompilerParams(
            dimension_semantics=("arbitrary", "arbitrary"), vmem_limit_bytes=VMEM_LIMIT),
        name="lru_fwd",
    )(xl, xl, xl, cw, cb, wg, ba, bx, lam)


def _lru_bwd(xl, cw, cb, wg, ba, bx, lam, hf, gl, gg, *, tt):
    B, Tp, W = xl.shape
    nt = Tp // tt
    blk, prev, nxt = _lru_specs(tt, nt, Tp, lambda j: nt - 1 - j)
    full = lambda a: pl.BlockSpec(a.shape, lambda b, j: (0,) * a.ndim)
    return pl.pallas_call(
        functools.partial(_lru_bwd_kernel, tt=tt, nt=nt),
        grid=(B, nt),
        in_specs=[blk, prev, nxt, full(cw), full(cb), full(wg), full(ba), full(bx), full(lam),
                  blk, blk, full(gg)],
        out_specs=blk,
        out_shape=jax.ShapeDtypeStruct((B, Tp, W), jnp.bfloat16),
        scratch_shapes=[pltpu.VMEM((tt + 2 * SUB, W), jnp.float32),
                        pltpu.VMEM((tt, W), jnp.float32),
                        pltpu.VMEM((tt, W), jnp.float32),
                        pltpu.VMEM((tt, W), jnp.float32),
                        pltpu.VMEM((1, W), jnp.float32)],
        compiler_params=pltpu.CompilerParams(
            dimension_semantics=("arbitrary", "arbitrary"), vmem_limit_bytes=VMEM_LIMIT),
        name="lru_bwd",
    )(xl, xl, xl, cw, cb, wg, ba, bx, lam, hf, gl, gg)


def _attn_kernel(q_ref, k_ref, v_ref, o_ref, *, tq):
    heads = [slice(hh * HP, (hh + 1) * HP) for hh in range(2)]
    scores = [lax.dot_general(q_ref[:, sl], k_ref[:, sl], (((1,), (1,)), ((), ())),
                              preferred_element_type=jnp.float32) for sl in heads]
    accs = []
    for s, sl in zip(scores, heads):
        m = jnp.max(s, axis=-1, keepdims=True)
        p = jnp.exp2(s - m).astype(jnp.bfloat16)
        accs.append(jnp.dot(p, v_ref[:, sl], preferred_element_type=jnp.float32))
    a0, a1 = accs
    lane = lax.broadcasted_iota(jnp.int32, (tq, HP), 1)
    o0 = a0 / a0[:, V_DIM:V_DIM + 1]
    o1 = a1 / a1[:, 0:1]
    o_ref[...] = jnp.where(lane < V_DIM, o0, o1)


def _attn(q, k, v, *, tq):
    B, Tp, _ = q.shape
    npair = MLA_HEADS // 2
    return pl.pallas_call(
        functools.partial(_attn_kernel, tq=tq),
        grid=(B, npair, Tp // tq),
        in_specs=[pl.BlockSpec((None, tq, 2 * HP), lambda b, p, i: (b, i, p)),
                  pl.BlockSpec((None, Tp, 2 * HP), lambda b, p, i: (b, 0, p)),
                  pl.BlockSpec((None, Tp, 2 * HP), lambda b, p, i: (b, 0, p))],
        out_specs=pl.BlockSpec((None, tq, HP), lambda b, p, i: (b, i, p)),
        out_shape=jax.ShapeDtypeStruct((B, Tp, npair * HP), jnp.float32),
        compiler_params=pltpu.CompilerParams(
            dimension_semantics=("arbitrary", "arbitrary", "arbitrary"),
            vmem_limit_bytes=VMEM_LIMIT),
        name="attn",
    )(q, k, v)


def _outproj_kernel(x_ref, head_ref, yl_ref, ym_ref, mg_ref, wo_ref, o_ref):
    h = jnp.where(pl.program_id(1) == 0, head_ref[...], x_ref[...])
    ymn = _rms(ym_ref[...], mg_ref[...]).astype(jnp.bfloat16)
    y = jnp.concatenate([yl_ref[...], ymn], axis=1)
    o_ref[...] = h + jnp.dot(y, wo_ref[...], preferred_element_type=jnp.float32)


def _outproj(x, head, yl, ym, mg, wo, *, tm):
    B, S, D = x.shape
    assert tm == X0
    Tp = X0 + S
    row = lambda w: pl.BlockSpec((None, tm, w), lambda b, i: (b, i, 0))
    full = lambda a: pl.BlockSpec(a.shape, lambda b, i: (0,) * a.ndim)
    return pl.pallas_call(
        _outproj_kernel,
        grid=(B, Tp // tm),
        in_specs=[_x_tile_spec(tm, D), full(head), row(LRU_WIDTH), row(LRU_WIDTH), full(mg),
                  full(wo)],
        out_specs=row(D),
        out_shape=jax.ShapeDtypeStruct((B, Tp, D), jnp.float32),
        compiler_params=pltpu.CompilerParams(
            dimension_semantics=("arbitrary", "arbitrary"), vmem_limit_bytes=VMEM_LIMIT),
        name="outproj",
    )(x, head, yl, ym, mg, wo)


def _ffn_kernel(h_ref, hprev_ref, hnext_ref, g_ref, wg_ref, wu_ref, cw_ref, cb_ref, wd_ref,
                o_ref, gate_ref, *, tm, nt):
    i = pl.program_id(1)
    h = h_ref[...]
    hnext = jnp.where(i == nt - 1, 0.0, hnext_ref[...])
    hw = jnp.concatenate([hprev_ref[...], h, hnext], axis=0)
    hn = _rms(hw, g_ref[...]).astype(jnp.bfloat16)
    gate_ref[...] = jnp.dot(hn, wg_ref[...], preferred_element_type=jnp.float32)
    up = jnp.dot(hn[SUB:SUB + tm], wu_ref[...], preferred_element_type=jnp.float32)
    left = FFN_CONV // 2
    gc = cb_ref[...] + gate_ref[SUB - left:SUB - left + tm, :] * cw_ref[0:1, :]
    for k in range(1, FFN_CONV):
        gc = gc + gate_ref[SUB - left + k:SUB - left + k + tm, :] * cw_ref[k:k + 1, :]
    act = (gc * jax.nn.sigmoid(gc) * up).astype(jnp.bfloat16)
    o_ref[...] = h + jnp.dot(act, wd_ref[...], preferred_element_type=jnp.float32)


def _ffn(h1, g, wg, wu, cw, cb, wd, *, tm, seq):
    B, Tp, D = h1.shape
    nt = seq // tm
    off = X0 // tm
    per = tm // SUB
    nsub = Tp // SUB
    full = lambda a: pl.BlockSpec(a.shape, lambda b, i: (0,) * a.ndim)
    return pl.pallas_call(
        functools.partial(_ffn_kernel, tm=tm, nt=nt),
        grid=(B, nt),
        in_specs=[pl.BlockSpec((None, tm, D), lambda b, i: (b, i + off, 0)),
                  pl.BlockSpec((None, SUB, D), lambda b, i: (b, (i + off) * per - 1, 0)),
                  pl.BlockSpec((None, SUB, D),
                               lambda b, i: (b, jnp.minimum((i + off + 1) * per, nsub - 1), 0)),
                  full(g), full(wg), full(wu), full(cw), full(cb), full(wd)],
        out_specs=pl.BlockSpec((None, tm, D), lambda b, i: (b, i, 0)),
        out_shape=jax.ShapeDtypeStruct((B, seq, D), jnp.float32),
        scratch_shapes=[pltpu.VMEM((tm + 2 * SUB, D_FF), jnp.float32)],
        compiler_params=pltpu.CompilerParams(
            dimension_semantics=("arbitrary", "arbitrary"), vmem_limit_bytes=VMEM_LIMIT),
        name="ffn",
    )(h1, h1, h1, g, wg, wu, cw, cb, wd)


def _pad_heads(w, width, offset=0):
    K = w.shape[0]
    w = w.reshape(K, MLA_HEADS, width)
    w = jnp.pad(w, ((0, 0), (0, 0), (offset, HP - width - offset)))
    return w.reshape(K, MLA_HEADS * HP)


def _block_diag(w):
    H, Dh, _ = w.shape
    eye = jnp.eye(H, dtype=w.dtype)
    return (eye[:, None, :, None] * w[:, :, None, :]).reshape(H * Dh, H * Dh)


def _gate_weights(w_a, w_x):
    half = LRU_WIDTH // 2
    wa = _block_diag(w_a)
    wx = _block_diag(w_x)
    halves = [jnp.concatenate([wa[c * half:(c + 1) * half, c * half:(c + 1) * half],
                               wx[c * half:(c + 1) * half, c * half:(c + 1) * half]], axis=1)
              for c in range(2)]
    return jnp.stack(halves).astype(jnp.bfloat16)


def kernel(x, meta_tokens, norm_mix_g, w_in, conv_lru_w, conv_lru_b, lru_w_a, lru_b_a, lru_w_x,
           lru_b_x, lru_lambda, lru_gate_g, q_latent_g, w_uq, kv_latent_g, w_ukv, q_norm_g,
           k_norm_g, mla_out_g, w_out, norm_ffn_g, w_ffn_up, conv_ffn_w, conv_ffn_b, w_ffn_down):
    B, S, D = x.shape
    f32, bf16 = jnp.float32, jnp.bfloat16
    Tp = X0 + S
    l = 0
    r2 = lambda a: a.reshape(1, -1).astype(f32)

    head = jnp.concatenate([jnp.zeros((PAD, D), x.dtype), meta_tokens.astype(x.dtype)], axis=0)

    half = QK_ROPE // 2
    lanes = jnp.arange(HP)
    swap = jnp.where((lanes >= QK_NOPE) & (lanes < QK_NOPE + half), lanes + half,
                     jnp.where((lanes >= QK_NOPE + half) & (lanes < QK_DIM), lanes - half, lanes))
    swap_heads = (jnp.arange(MLA_HEADS)[:, None] * HP + swap[None, :]).reshape(-1)

    o = 2 * LRU_WIDTH + Q_LORA + KV_LORA
    kpe_w = jnp.pad(w_in[l][:, o:], ((0, 0), (QK_NOPE, LANE - QK_DIM)))
    win = jnp.concatenate([w_in[l][:, :o], kpe_w, kpe_w[:, swap]], axis=1).astype(bf16)
    wuq = _pad_heads(w_uq[l], QK_DIM).astype(bf16)
    wuqs = wuq[:, swap_heads]
    wkv = w_ukv[l].reshape(KV_LORA, MLA_HEADS, QK_NOPE + V_DIM)
    wuk = _pad_heads(wkv[:, :, :QK_NOPE].reshape(KV_LORA, -1), QK_NOPE).astype(bf16)
    wv = jnp.pad(wkv[:, :, QK_NOPE:], ((0, 0), (0, 0), (0, HP - V_DIM)))
    wv = jnp.where((jnp.arange(MLA_HEADS) % 2 == 1)[None, :, None],
                   jnp.roll(wv, V_DIM, axis=2), wv)
    wuv = wv.reshape(KV_LORA, MLA_HEADS * HP).astype(bf16)
    qg = jnp.pad(q_norm_g[l], (0, HP - QK_DIM)).reshape(1, HP)
    kg = jnp.pad(k_norm_g[l], (0, HP - QK_DIM)).reshape(1, HP)
    ones = jnp.ones((2 * HP, HP), bf16)

    pos = jnp.arange(Tp, dtype=f32) - PAD
    inv_freq = ROPE_THETA ** (-jnp.arange(0, QK_ROPE, 2, dtype=f32) / QK_ROPE)
    ang = pos[:, None] * inv_freq[None, :]
    cos, sin = jnp.cos(ang), jnp.sin(ang)
    ctab = jnp.concatenate([jnp.ones((Tp, QK_NOPE), f32), cos, cos,
                            jnp.ones((Tp, HP - QK_DIM), f32)], axis=1)
    stab = jnp.concatenate([jnp.zeros((Tp, QK_NOPE), f32), -sin, sin,
                            jnp.zeros((Tp, HP - QK_DIM), f32)], axis=1)

    xl, gl, q, k, v = _inproj(x, head, r2(norm_mix_g[l]), win, r2(q_latent_g[l]), wuq, wuqs,
                              r2(kv_latent_g[l]), wuk, wuv, qg, qg[:, swap], kg, kg[:, swap],
                              ones, ctab, stab, tm=X0)

    cw, cb = conv_lru_w[l], r2(conv_lru_b[l])
    wg = [_gate_weights(lru_w_a[l, d], lru_w_x[l, d]) for d in range(2)]
    hf = _lru_fwd(xl, cw, cb, wg[0], r2(lru_b_a[l, 0]), r2(lru_b_x[l, 0]), r2(lru_lambda[l, 0]),
                  tt=256)
    yl = _lru_bwd(xl, cw, cb, wg[1], r2(lru_b_a[l, 1]), r2(lru_b_x[l, 1]), r2(lru_lambda[l, 1]),
                  hf, gl, r2(lru_gate_g[l]), tt=256)

    ym = _attn(q, k, v, tq=256)

    h1 = _outproj(x, head, yl, ym, r2(mla_out_g[l]), w_out[l].astype(bf16), tm=X0)

    wup = w_ffn_up[l].astype(bf16)
    return _ffn(h1, r2(norm_ffn_g[l]), wup[:, :D_FF], wup[:, D_FF:], conv_ffn_w[l],
                r2(conv_ffn_b[l]), w_ffn_down[l].astype(bf16), tm=256, seq=S)
```

```python
import functools
import math

import jax
import jax.numpy as jnp
from jax import lax
from jax.experimental import pallas as pl
from jax.experimental.pallas import tpu as pltpu

D_MODEL = 1024
N_META = 16
LRU_WIDTH = 512
LRU_HEADS = 8
LRU_HEAD_DIM = 64
LRU_CONV = 4
LRU_C = 8.0
MLA_HEADS = 8
QK_NOPE = 64
QK_ROPE = 32
QK_DIM = 96
V_DIM = 64
Q_LORA = 256
KV_LORA = 128
ROPE_THETA = 10000.0
D_FF = 2816
FFN_CONV = 3
EPS = 1e-6
NEG_INF = -1e30

LANE = 128
SUB = 8
X0 = 256
PAD = X0 - N_META
HP = LANE
MASK_LANE = QK_DIM
VMEM_LIMIT = 56 * 1024 * 1024


def _rms(x, g):
    return x * lax.rsqrt(jnp.mean(x * x, axis=-1, keepdims=True) + EPS) * g


def _bdot(a, b):
    return jnp.dot(a.astype(jnp.bfloat16), b, preferred_element_type=jnp.float32)


def _inproj_kernel(x_ref, head_ref, gmix_ref, win_ref, qlg_ref, wuq_ref, wuqs_ref, kvlg_ref,
                   wuk_ref, wuvt_ref, qg_ref, qgs_ref, kg_ref, kgs_ref, ones_ref, c_ref, s_ref,
                   xlru_ref, glru_ref, q_ref, k_ref, vt_ref, *, tm):
    i = pl.program_id(1)
    h = jnp.where(i == 0, head_ref[...], x_ref[...])
    hn = _rms(h, gmix_ref[...])
    proj = _bdot(hn, win_ref[...])
    xlru_ref[...] = proj[:, :LRU_WIDTH]
    glru_ref[...] = proj[:, LRU_WIDTH:2 * LRU_WIDTH]
    o = 2 * LRU_WIDTH
    cq = proj[:, o:o + Q_LORA]
    ckv = proj[:, o + Q_LORA:o + Q_LORA + KV_LORA]
    o += Q_LORA + KV_LORA
    kpe = proj[:, o:o + HP]
    kpe_sw = proj[:, o + HP:]
    cqn = _rms(cq, qlg_ref[...]).astype(jnp.bfloat16)
    qraw = jnp.dot(cqn, wuq_ref[...], preferred_element_type=jnp.float32)
    qraw_sw = jnp.dot(cqn, wuqs_ref[...], preferred_element_type=jnp.float32)
    ckvn = _rms(ckv, kvlg_ref[...]).astype(jnp.bfloat16)
    kraw = jnp.dot(ckvn, wuk_ref[...], preferred_element_type=jnp.float32)
    vt = lax.dot_general(wuvt_ref[...], ckvn, (((1,), (1,)), ((), ())),
                         preferred_element_type=jnp.float32)
    vrow = lax.broadcasted_iota(jnp.int32, vt.shape, 0) % (2 * HP)
    vt_ref[...] = jnp.where((vrow == V_DIM) | (vrow == HP), 1.0, vt).astype(jnp.bfloat16)

    lane = lax.broadcasted_iota(jnp.int32, (tm, HP), 1)
    row = lax.broadcasted_iota(jnp.int32, (tm, HP), 0) + i * tm
    key_mask = jnp.where(row >= PAD, 0.0, NEG_INF)
    scale = math.log2(math.e) / math.sqrt(QK_DIM)
    cosv, sinv = c_ref[...], s_ref[...]
    q_cos = cosv * (qg_ref[...] * scale)
    q_sin = sinv * (qgs_ref[...] * scale)
    k_cos = cosv * kg_ref[...]
    k_sin_term = kpe_sw * (sinv * kgs_ref[...])

    def inv_rms(xb):
        x2 = xb * xb
        hi = x2.astype(jnp.bfloat16)
        lo = (x2 - hi.astype(jnp.float32)).astype(jnp.bfloat16)
        ss = jnp.dot(jnp.concatenate([hi, lo], axis=1), ones_ref[...],
                     preferred_element_type=jnp.float32)
        return lax.rsqrt(ss * (1.0 / QK_DIM) + EPS)

    for hd in range(MLA_HEADS):
        sl = slice(hd * HP, (hd + 1) * HP)
        xq = qraw[:, sl]
        qb = (xq * q_cos + qraw_sw[:, sl] * q_sin) * inv_rms(xq)
        q_ref[:, sl] = jnp.where(lane == MASK_LANE, 1.0, qb).astype(jnp.bfloat16)
        xk = kraw[:, sl] + kpe
        kb = (xk * k_cos + k_sin_term) * inv_rms(xk)
        k_ref[:, sl] = jnp.where(lane == MASK_LANE, key_mask, kb).astype(jnp.bfloat16)


def _x_tile_spec(tm, D):
    return pl.BlockSpec((None, tm, D), lambda b, i: (b, jnp.maximum(i - 1, 0), 0))


def _inproj(x, head, gmix, win, qlg, wuq, wuqs, kvlg, wuk, wuv, qg, qgs, kg, kgs, ones, ctab,
            stab, *, tm):
    B, S, D = x.shape
    assert tm == X0
    Tp = X0 + S
    nt = Tp // tm
    row = lambda w: pl.BlockSpec((None, tm, w), lambda b, i: (b, i, 0))
    full = lambda a: pl.BlockSpec(a.shape, lambda b, i: (0,) * a.ndim)
    tab = pl.BlockSpec((tm, HP), lambda b, i: (i, 0))
    f32, bf16 = jnp.float32, jnp.bfloat16
    return pl.pallas_call(
        functools.partial(_inproj_kernel, tm=tm),
        grid=(B, nt),
        in_specs=[_x_tile_spec(tm, D), full(head), full(gmix), full(win), full(qlg), full(wuq),
                  full(wuqs), full(kvlg), full(wuk), full(wuv), full(qg), full(qgs), full(kg),
                  full(kgs), full(ones), tab, tab],
        out_specs=[row(LRU_WIDTH), row(LRU_WIDTH), row(MLA_HEADS * HP), row(MLA_HEADS * HP),
                   pl.BlockSpec((None, MLA_HEADS * HP, tm), lambda b, i: (b, 0, i))],
        out_shape=[jax.ShapeDtypeStruct((B, Tp, LRU_WIDTH), f32),
                   jax.ShapeDtypeStruct((B, Tp, LRU_WIDTH), f32),
                   jax.ShapeDtypeStruct((B, Tp, MLA_HEADS * HP), bf16),
                   jax.ShapeDtypeStruct((B, Tp, MLA_HEADS * HP), bf16),
                   jax.ShapeDtypeStruct((B, MLA_HEADS * HP, Tp), bf16)],
        compiler_params=pltpu.CompilerParams(
            dimension_semantics=("arbitrary", "arbitrary"), vmem_limit_bytes=VMEM_LIMIT),
        name="inproj",
    )(x, head, gmix, win, qlg, wuq, wuqs, kvlg, wuk, wuv, qg, qgs, kg, kgs, ones, ctab, stab)


def _lru_gates(x_ref, xprev_ref, xnext_ref, cw_ref, cb_ref, wg_ref, ba_ref, bx_ref, lam_ref,
               xw_ref, a_ref, u_ref, *, tt, first, last, t0):
    W = LRU_WIDTH
    xw_ref[0:SUB, :] = jnp.where(first, 0.0, xprev_ref[...])
    xw_ref[SUB:SUB + tt, :] = x_ref[...]
    xw_ref[SUB + tt:, :] = jnp.where(last, 0.0, xnext_ref[...])
    left = LRU_CONV // 2
    xc = cb_ref[...] + xw_ref[SUB - left:SUB - left + tt, :] * cw_ref[0:1, :]
    for k in range(1, LRU_CONV):
        xc = xc + xw_ref[SUB - left + k:SUB - left + k + tt, :] * cw_ref[k:k + 1, :]
    xcb = xc.astype(jnp.bfloat16)
    half = W // 2
    g0 = jnp.dot(xcb[:, :half], wg_ref[0], preferred_element_type=jnp.float32)
    g1 = jnp.dot(xcb[:, half:], wg_ref[1], preferred_element_type=jnp.float32)
    ra = jnp.concatenate([g0[:, :half], g1[:, :half]], axis=1) + ba_ref[...]
    ri = jnp.concatenate([g0[:, half:], g1[:, half:]], axis=1) + bx_ref[...]
    r = jax.nn.sigmoid(ra)
    ig = jax.nn.sigmoid(ri)
    lam = lam_ref[...]
    sp = jnp.maximum(-lam, 0.0) + jnp.log(1.0 + jnp.exp(-jnp.abs(lam)))
    log_a = -LRU_C * r * sp
    a = jnp.exp(log_a)
    u = jnp.sqrt(1.0 - a * a) * (ig * xc)
    rows = lax.broadcasted_iota(jnp.int32, (tt, W), 0) + t0
    a_ref[...] = a
    u_ref[...] = jnp.where(rows >= PAD, u, 0.0)


def _scan_groups(a_ref, u_ref, h_ref, carry_ref, *, tt, reverse):
    W = LRU_WIDTH
    ng = tt // SUB
    sub = lax.broadcasted_iota(jnp.int32, (SUB, W), 0)

    def body(g, carry):
        gi = (ng - 1 - g) if reverse else g
        r0 = pl.multiple_of(gi * SUB, SUB)
        a = a_ref[pl.ds(r0, SUB), :]
        u = u_ref[pl.ds(r0, SUB), :]
        for d in (1, 2, 4):
            if reverse:
                keep = sub < SUB - d
                sh = SUB - d
            else:
                keep = sub >= d
                sh = d
            a_sh = jnp.where(keep, pltpu.roll(a, sh, axis=0), 1.0)
            u_sh = jnp.where(keep, pltpu.roll(u, sh, axis=0), 0.0)
            u = u + a * u_sh
            a = a * a_sh
        h = u + a * carry
        h_ref[pl.ds(r0, SUB), :] = h
        return h[0:1, :] if reverse else h[SUB - 1:SUB, :]

    carry_ref[...] = lax.fori_loop(0, ng, body, carry_ref[...])


def _lru_fwd_kernel(x_ref, xprev_ref, xnext_ref, cw_ref, cb_ref, wg_ref, ba_ref, bx_ref, lam_ref,
                    hf_ref, xw_ref, a_ref, u_ref, carry_ref, *, tt, nt):
    i = pl.program_id(1)

    @pl.when(i == 0)
    def _():
        carry_ref[...] = jnp.zeros_like(carry_ref)

    _lru_gates(x_ref, xprev_ref, xnext_ref, cw_ref, cb_ref, wg_ref, ba_ref, bx_ref, lam_ref,
               xw_ref, a_ref, u_ref, tt=tt, first=i == 0, last=i == nt - 1, t0=i * tt)
    _scan_groups(a_ref, u_ref, hf_ref, carry_ref, tt=tt, reverse=False)


def _lru_bwd_kernel(x_ref, xprev_ref, xnext_ref, cw_ref, cb_ref, wg_ref, ba_ref, bx_ref, lam_ref,
                    hf_ref, gl_ref, gg_ref, y_ref, xw_ref, a_ref, u_ref, hb_ref, carry_ref,
                    *, tt, nt):
    j = pl.program_id(1)
    i = nt - 1 - j

    @pl.when(j == 0)
    def _():
        carry_ref[...] = jnp.zeros_like(carry_ref)

    _lru_gates(x_ref, xprev_ref, xnext_ref, cw_ref, cb_ref, wg_ref, ba_ref, bx_ref, lam_ref,
               xw_ref, a_ref, u_ref, tt=tt, first=i == 0, last=i == nt - 1, t0=i * tt)
    _scan_groups(a_ref, u_ref, hb_ref, carry_ref, tt=tt, reverse=True)
    y = (hf_ref[...] + hb_ref[...]) * jax.nn.gelu(gl_ref[...], approximate=True)
    y_ref[...] = _rms(y, gg_ref[...]).astype(y_ref.dtype)


def _lru_specs(tt, nt, Tp, tmap):
    W = LRU_WIDTH
    nsub = Tp // SUB
    per = tt // SUB
    blk = pl.BlockSpec((None, tt, W), lambda b, j: (b, tmap(j), 0))
    prev = pl.BlockSpec((None, SUB, W), lambda b, j: (b, jnp.maximum(tmap(j) * per - 1, 0), 0))
    nxt = pl.BlockSpec((None, SUB, W),
                       lambda b, j: (b, jnp.minimum((tmap(j) + 1) * per, nsub - 1), 0))
    return blk, prev, nxt


def _lru_fwd(xl, cw, cb, wg, ba, bx, lam, *, tt):
    B, Tp, W = xl.shape
    nt = Tp // tt
    blk, prev, nxt = _lru_specs(tt, nt, Tp, lambda j: j)
    full = lambda a: pl.BlockSpec(a.shape, lambda b, j: (0,) * a.ndim)
    return pl.pallas_call(
        functools.partial(_lru_fwd_kernel, tt=tt, nt=nt),
        grid=(B, nt),
        in_specs=[blk, prev, nxt, full(cw), full(cb), full(wg), full(ba), full(bx), full(lam)],
        out_specs=blk,
        out_shape=jax.ShapeDtypeStruct((B, Tp, W), jnp.float32),
        scratch_shapes=[pltpu.VMEM((tt + 2 * SUB, W), jnp.float32),
                        pltpu.VMEM((tt, W), jnp.float32),
                        pltpu.VMEM((tt, W), jnp.float32),
                        pltpu.VMEM((1, W), jnp.float32)],
        compiler_params=pltpu.CompilerParams(
            dimension_semantics=("arbitrary", "arbitrary"), vmem_limit_bytes=VMEM_LIMIT),
        name="lru_fwd",
    )(xl, xl, xl, cw, cb, wg, ba, bx, lam)


def _lru_bwd(xl, cw, cb, wg, ba, bx, lam, hf, gl, gg, *, tt):
    B, Tp, W = xl.shape
    nt = Tp // tt
    blk, prev, nxt = _lru_specs(tt, nt, Tp, lambda j: nt - 1 - j)
    full = lambda a: pl.BlockSpec(a.shape, lambda b, j: (0,) * a.ndim)
    return pl.pallas_call(
        functools.partial(_lru_bwd_kernel, tt=tt, nt=nt),
        grid=(B, nt),
        in_specs=[blk, prev, nxt, full(cw), full(cb), full(wg), full(ba), full(bx), full(lam),
                  blk, blk, full(gg)],
        out_specs=blk,
        out_shape=jax.ShapeDtypeStruct((B, Tp, W), jnp.bfloat16),
        scratch_shapes=[pltpu.VMEM((tt + 2 * SUB, W), jnp.float32),
                        pltpu.VMEM((tt, W), jnp.float32),
                        pltpu.VMEM((tt, W), jnp.float32),
                        pltpu.VMEM((tt, W), jnp.float32),
                        pltpu.VMEM((1, W), jnp.float32)],
        compiler_params=pltpu.CompilerParams(
            dimension_semantics=("arbitrary", "arbitrary"), vmem_limit_bytes=VMEM_LIMIT),
        name="lru_bwd",
    )(xl, xl, xl, cw, cb, wg, ba, bx, lam, hf, gl, gg)


def _max_over_rows(s):
    rows = s.shape[0]
    while rows % (2 * SUB) == 0 and rows > 32 * SUB:
        fold = 4 if rows % (4 * SUB) == 0 else 2
        rows //= fold
        s = jnp.max(s.reshape(fold, rows, s.shape[1]), axis=0)
    return jnp.max(s, axis=0, keepdims=True)


def _attn_kernel(q_ref, k_ref, vt_ref, o_ref, *, tq, lead, span):
    heads = [slice(hh * HP, (hh + 1) * HP) for hh in range(2)]
    nkeys = k_ref.shape[0]
    nt = (((1,), (1,)), ((), ()))
    rows = lax.broadcasted_iota(jnp.int32, (HP, tq), 0)

    def scores(hh, lo, hi):
        return lax.dot_general(k_ref[lo:hi, heads[hh]], q_ref[:, heads[hh]], nt,
                               preferred_element_type=jnp.float32)

    def pv(hh, lo, hi, s, m):
        return jnp.dot(vt_ref[heads[hh], lo:hi], jnp.exp2(s - m).astype(jnp.bfloat16),
                       preferred_element_type=jnp.float32)

    def finish(a0, a1):
        l0, l1 = a0[V_DIM:V_DIM + 1, :], a1[0:1, :]
        return jnp.where(rows < V_DIM, a0 / l0, a1 / l1).T, l0, l1

    accs = []
    for hh in range(2):
        s = scores(hh, 0, lead)
        m = _max_over_rows(s)
        acc = pv(hh, 0, lead, s, m)
        for lo in range(lead, nkeys, span):
            acc = acc + pv(hh, lo, lo + span, scores(hh, lo, lo + span), m)
        accs.append(acc)
    out, l0, l1 = finish(*accs)
    lmin = jnp.min(jnp.minimum(l0, l1))
    lmax = jnp.max(jnp.maximum(l0, l1))
    safe = (lmin > 2.0 ** -100) & (lmax < 2.0 ** 100)

    @pl.when(safe)
    def _():
        o_ref[...] = out

    @pl.when(jnp.logical_not(safe))
    def _():
        exact = []
        for hh in range(2):
            s = scores(hh, 0, nkeys)
            exact.append(pv(hh, 0, nkeys, s, _max_over_rows(s)))
        o_ref[...] = finish(*exact)[0]


def _attn(q, k, vt, *, tq, lead, span):
    B, Tp, _ = q.shape
    npair = MLA_HEADS // 2
    return pl.pallas_call(
        functools.partial(_attn_kernel, tq=tq, lead=lead, span=span),
        grid=(B, npair, Tp // tq),
        in_specs=[pl.BlockSpec((None, tq, 2 * HP), lambda b, p, i: (b, i, p)),
                  pl.BlockSpec((None, Tp, 2 * HP), lambda b, p, i: (b, 0, p)),
                  pl.BlockSpec((None, 2 * HP, Tp), lambda b, p, i: (b, p, 0))],
        out_specs=pl.BlockSpec((None, tq, HP), lambda b, p, i: (b, i, p)),
        out_shape=jax.ShapeDtypeStruct((B, Tp, npair * HP), jnp.float32),
        compiler_params=pltpu.CompilerParams(
            dimension_semantics=("arbitrary", "arbitrary", "arbitrary"),
            vmem_limit_bytes=VMEM_LIMIT),
        name="attn",
    )(q, k, vt)


def _outproj_kernel(x_ref, head_ref, yl_ref, ym_ref, mg_ref, wo_ref, o_ref):
    h = jnp.where(pl.program_id(1) == 0, head_ref[...], x_ref[...])
    ymn = _rms(ym_ref[...], mg_ref[...]).astype(jnp.bfloat16)
    y = jnp.concatenate([yl_ref[...], ymn], axis=1)
    o_ref[...] = h + jnp.dot(y, wo_ref[...], preferred_element_type=jnp.float32)


def _outproj(x, head, yl, ym, mg, wo, *, tm):
    B, S, D = x.shape
    assert tm == X0
    Tp = X0 + S
    row = lambda w: pl.BlockSpec((None, tm, w), lambda b, i: (b, i, 0))
    full = lambda a: pl.BlockSpec(a.shape, lambda b, i: (0,) * a.ndim)
    return pl.pallas_call(
        _outproj_kernel,
        grid=(B, Tp // tm),
        in_specs=[_x_tile_spec(tm, D), full(head), row(LRU_WIDTH), row(LRU_WIDTH), full(mg),
                  full(wo)],
        out_specs=row(D),
        out_shape=jax.ShapeDtypeStruct((B, Tp, D), jnp.float32),
        compiler_params=pltpu.CompilerParams(
            dimension_semantics=("arbitrary", "arbitrary"), vmem_limit_bytes=VMEM_LIMIT),
        name="outproj",
    )(x, head, yl, ym, mg, wo)


def _ffn_kernel(h_ref, hprev_ref, hnext_ref, g_ref, wg_ref, wu_ref, cw_ref, cb_ref, wd_ref,
                o_ref, gate_ref, *, tm, nt):
    i = pl.program_id(1)
    h = h_ref[...]
    hnext = jnp.where(i == nt - 1, 0.0, hnext_ref[...])
    hw = jnp.concatenate([hprev_ref[...], h, hnext], axis=0)
    hn = _rms(hw, g_ref[...]).astype(jnp.bfloat16)
    gate_ref[...] = jnp.dot(hn, wg_ref[...], preferred_element_type=jnp.float32)
    up = jnp.dot(hn[SUB:SUB + tm], wu_ref[...], preferred_element_type=jnp.float32)
    left = FFN_CONV // 2
    gc = cb_ref[...] + gate_ref[SUB - left:SUB - left + tm, :] * cw_ref[0:1, :]
    for k in range(1, FFN_CONV):
        gc = gc + gate_ref[SUB - left + k:SUB - left + k + tm, :] * cw_ref[k:k + 1, :]
    act = (gc * jax.nn.sigmoid(gc) * up).astype(jnp.bfloat16)
    o_ref[...] = h + jnp.dot(act, wd_ref[...], preferred_element_type=jnp.float32)


def _ffn(h1, g, wg, wu, cw, cb, wd, *, tm, seq):
    B, Tp, D = h1.shape
    nt = seq // tm
    off = X0 // tm
    per = tm // SUB
    nsub = Tp // SUB
    full = lambda a: pl.BlockSpec(a.shape, lambda b, i: (0,) * a.ndim)
    return pl.pallas_call(
        functools.partial(_ffn_kernel, tm=tm, nt=nt),
        grid=(B, nt),
        in_specs=[pl.BlockSpec((None, tm, D), lambda b, i: (b, i + off, 0)),
                  pl.BlockSpec((None, SUB, D), lambda b, i: (b, (i + off) * per - 1, 0)),
                  pl.BlockSpec((None, SUB, D),
                               lambda b, i: (b, jnp.minimum((i + off + 1) * per, nsub - 1), 0)),
                  full(g), full(wg), full(wu), full(cw), full(cb), full(wd)],
        out_specs=pl.BlockSpec((None, tm, D), lambda b, i: (b, i, 0)),
        out_shape=jax.ShapeDtypeStruct((B, seq, D), jnp.float32),
        scratch_shapes=[pltpu.VMEM((tm + 2 * SUB, D_FF), jnp.float32)],
        compiler_params=pltpu.CompilerParams(
            dimension_semantics=("arbitrary", "arbitrary"), vmem_limit_bytes=VMEM_LIMIT),
        name="ffn",
    )(h1, h1, h1, g, wg, wu, cw, cb, wd)


def _pad_heads(w, width, offset=0):
    K = w.shape[0]
    w = w.reshape(K, MLA_HEADS, width)
    w = jnp.pad(w, ((0, 0), (0, 0), (offset, HP - width - offset)))
    return w.reshape(K, MLA_HEADS * HP)


def _block_diag(w):
    H, Dh, _ = w.shape
    eye = jnp.eye(H, dtype=w.dtype)
    return (eye[:, None, :, None] * w[:, :, None, :]).reshape(H * Dh, H * Dh)


def _gate_weights(w_a, w_x):
    half = LRU_WIDTH // 2
    wa = _block_diag(w_a)
    wx = _block_diag(w_x)
    halves = [jnp.concatenate([wa[c * half:(c + 1) * half, c * half:(c + 1) * half],
                               wx[c * half:(c + 1) * half, c * half:(c + 1) * half]], axis=1)
              for c in range(2)]
    return jnp.stack(halves).astype(jnp.bfloat16)


def kernel(x, meta_tokens, norm_mix_g, w_in, conv_lru_w, conv_lru_b, lru_w_a, lru_b_a, lru_w_x,
           lru_b_x, lru_lambda, lru_gate_g, q_latent_g, w_uq, kv_latent_g, w_ukv, q_norm_g,
           k_norm_g, mla_out_g, w_out, norm_ffn_g, w_ffn_up, conv_ffn_w, conv_ffn_b, w_ffn_down):
    B, S, D = x.shape
    f32, bf16 = jnp.float32, jnp.bfloat16
    Tp = X0 + S
    l = 0
    r2 = lambda a: a.reshape(1, -1).astype(f32)

    head = jnp.concatenate([jnp.zeros((PAD, D), x.dtype), meta_tokens.astype(x.dtype)], axis=0)

    half = QK_ROPE // 2
    lanes = jnp.arange(HP)
    swap = jnp.where((lanes >= QK_NOPE) & (lanes < QK_NOPE + half), lanes + half,
                     jnp.where((lanes >= QK_NOPE + half) & (lanes < QK_DIM), lanes - half, lanes))
    swap_heads = (jnp.arange(MLA_HEADS)[:, None] * HP + swap[None, :]).reshape(-1)

    o = 2 * LRU_WIDTH + Q_LORA + KV_LORA
    kpe_w = jnp.pad(w_in[l][:, o:], ((0, 0), (QK_NOPE, LANE - QK_DIM)))
    win = jnp.concatenate([w_in[l][:, :o], kpe_w, kpe_w[:, swap]], axis=1).astype(bf16)
    wuq = _pad_heads(w_uq[l], QK_DIM).astype(bf16)
    wuqs = wuq[:, swap_heads]
    wkv = w_ukv[l].reshape(KV_LORA, MLA_HEADS, QK_NOPE + V_DIM)
    wuk = _pad_heads(wkv[:, :, :QK_NOPE].reshape(KV_LORA, -1), QK_NOPE).astype(bf16)
    wv = jnp.pad(wkv[:, :, QK_NOPE:], ((0, 0), (0, 0), (0, HP - V_DIM)))
    wv = jnp.where((jnp.arange(MLA_HEADS) % 2 == 1)[None, :, None],
                   jnp.roll(wv, V_DIM, axis=2), wv)
    wuvt = wv.reshape(KV_LORA, MLA_HEADS * HP).T.astype(bf16)
    qg = jnp.pad(q_norm_g[l], (0, HP - QK_DIM)).reshape(1, HP)
    kg = jnp.pad(k_norm_g[l], (0, HP - QK_DIM)).reshape(1, HP)
    ones = jnp.ones((2 * HP, HP), bf16)

    pos = jnp.arange(Tp, dtype=f32) - PAD
    inv_freq = ROPE_THETA ** (-jnp.arange(0, QK_ROPE, 2, dtype=f32) / QK_ROPE)
    ang = pos[:, None] * inv_freq[None, :]
    cos, sin = jnp.cos(ang), jnp.sin(ang)
    ctab = jnp.concatenate([jnp.ones((Tp, QK_NOPE), f32), cos, cos,
                            jnp.ones((Tp, HP - QK_DIM), f32)], axis=1)
    stab = jnp.concatenate([jnp.zeros((Tp, QK_NOPE), f32), -sin, sin,
                            jnp.zeros((Tp, HP - QK_DIM), f32)], axis=1)

    xl, gl, q, k, vt = _inproj(x, head, r2(norm_mix_g[l]), win, r2(q_latent_g[l]), wuq, wuqs,
                              r2(kv_latent_g[l]), wuk, wuvt, qg, qg[:, swap], kg, kg[:, swap],
                              ones, ctab, stab, tm=X0)

    cw, cb = conv_lru_w[l], r2(conv_lru_b[l])
    wg = [_gate_weights(lru_w_a[l, d], lru_w_x[l, d]) for d in range(2)]
    hf = _lru_fwd(xl, cw, cb, wg[0], r2(lru_b_a[l, 0]), r2(lru_b_x[l, 0]), r2(lru_lambda[l, 0]),
                  tt=256)
    yl = _lru_bwd(xl, cw, cb, wg[1], r2(lru_b_a[l, 1]), r2(lru_b_x[l, 1]), r2(lru_lambda[l, 1]),
                  hf, gl, r2(lru_gate_g[l]), tt=256)

    ym = _attn(q, k, vt, tq=256, lead=768, span=768)

    h1 = _outproj(x, head, yl, ym, r2(mla_out_g[l]), w_out[l].astype(bf16), tm=X0)

    wup = w_ffn_up[l].astype(bf16)
    return _ffn(h1, r2(norm_ffn_g[l]), wup[:, :D_FF], wup[:, D_FF:], conv_ffn_w[l],
                r2(conv_ffn_b[l]), w_ffn_down[l].astype(bf16), tm=256, seq=S)
```

```python
import functools
import math

import jax
import jax.numpy as jnp
from jax import lax
from jax.experimental import pallas as pl
from jax.experimental.pallas import tpu as pltpu

D_MODEL = 1024
N_META = 16
LRU_WIDTH = 512
LRU_HEADS = 8
LRU_HEAD_DIM = 64
LRU_CONV = 4
LRU_C = 8.0
MLA_HEADS = 8
QK_NOPE = 64
QK_ROPE = 32
QK_DIM = 96
V_DIM = 64
Q_LORA = 256
KV_LORA = 128
ROPE_THETA = 10000.0
D_FF = 2816
FFN_CONV = 3
EPS = 1e-6
NEG_INF = -1e30

LANE = 128
SUB = 8
X0 = 256
PAD = X0 - N_META
HP = LANE
MASK_LANE = QK_DIM
VMEM_LIMIT = 56 * 1024 * 1024


def _rms(x, g):
    return x * lax.rsqrt(jnp.mean(x * x, axis=-1, keepdims=True) + EPS) * g


def _bdot(a, b):
    return jnp.dot(a.astype(jnp.bfloat16), b, preferred_element_type=jnp.float32)


def _inproj_kernel(x_ref, head_ref, gmix_ref, win_ref, qlg_ref, wuq_ref, wuqs_ref, kvlg_ref,
                   wuk_ref, wuvt_ref, qg_ref, qgs_ref, kg_ref, kgs_ref, ones_ref, c_ref, s_ref,
                   xlru_ref, glru_ref, q_ref, k_ref, vt_ref, *, tm):
    i = pl.program_id(1)
    h = jnp.where(i == 0, head_ref[...], x_ref[...])
    hn = _rms(h, gmix_ref[...])
    proj = _bdot(hn, win_ref[...])
    xlru_ref[...] = proj[:, :LRU_WIDTH]
    glru_ref[...] = proj[:, LRU_WIDTH:2 * LRU_WIDTH]
    o = 2 * LRU_WIDTH
    cq = proj[:, o:o + Q_LORA]
    ckv = proj[:, o + Q_LORA:o + Q_LORA + KV_LORA]
    o += Q_LORA + KV_LORA
    kpe = proj[:, o:o + HP]
    kpe_sw = proj[:, o + HP:]
    cqn = _rms(cq, qlg_ref[...]).astype(jnp.bfloat16)
    qraw = jnp.dot(cqn, wuq_ref[...], preferred_element_type=jnp.float32)
    qraw_sw = jnp.dot(cqn, wuqs_ref[...], preferred_element_type=jnp.float32)
    ckvn = _rms(ckv, kvlg_ref[...]).astype(jnp.bfloat16)
    kraw = jnp.dot(ckvn, wuk_ref[...], preferred_element_type=jnp.float32)
    vt = lax.dot_general(wuvt_ref[...], ckvn, (((1,), (1,)), ((), ())),
                         preferred_element_type=jnp.float32)
    vrow = lax.broadcasted_iota(jnp.int32, vt.shape, 0) % (2 * HP)
    vt_ref[...] = jnp.where((vrow == V_DIM) | (vrow == HP), 1.0, vt).astype(jnp.bfloat16)

    lane = lax.broadcasted_iota(jnp.int32, (tm, HP), 1)
    row = lax.broadcasted_iota(jnp.int32, (tm, HP), 0) + i * tm
    key_mask = jnp.where(row >= PAD, 0.0, NEG_INF)
    scale = math.log2(math.e) / math.sqrt(QK_DIM)
    cosv, sinv = c_ref[...], s_ref[...]
    q_cos = cosv * (qg_ref[...] * scale)
    q_sin = sinv * (qgs_ref[...] * scale)
    k_cos = cosv * kg_ref[...]
    k_sin_term = kpe_sw * (sinv * kgs_ref[...])

    def inv_rms(xb):
        x2 = xb * xb
        hi = x2.astype(jnp.bfloat16)
        lo = (x2 - hi.astype(jnp.float32)).astype(jnp.bfloat16)
        ss = jnp.dot(jnp.concatenate([hi, lo], axis=1), ones_ref[...],
                     preferred_element_type=jnp.float32)
        return lax.rsqrt(ss * (1.0 / QK_DIM) + EPS)

    for hd in range(MLA_HEADS):
        sl = slice(hd * HP, (hd + 1) * HP)
        xq = qraw[:, sl]
        qb = (xq * q_cos + qraw_sw[:, sl] * q_sin) * inv_rms(xq)
        q_ref[:, sl] = jnp.where(lane == MASK_LANE, 1.0, qb).astype(jnp.bfloat16)
        xk = kraw[:, sl] + kpe
        kb = (xk * k_cos + k_sin_term) * inv_rms(xk)
        k_ref[:, sl] = jnp.where(lane == MASK_LANE, key_mask, kb).astype(jnp.bfloat16)


def _x_tile_spec(tm, D):
    return pl.BlockSpec((None, tm, D), lambda b, i: (b, jnp.maximum(i - 1, 0), 0))


def _inproj(x, head, gmix, win, qlg, wuq, wuqs, kvlg, wuk, wuv, qg, qgs, kg, kgs, ones, ctab,
            stab, *, tm):
    B, S, D = x.shape
    assert tm == X0
    Tp = X0 + S
    nt = Tp // tm
    row = lambda w: pl.BlockSpec((None, tm, w), lambda b, i: (b, i, 0))
    full = lambda a: pl.BlockSpec(a.shape, lambda b, i: (0,) * a.ndim)
    tab = pl.BlockSpec((tm, HP), lambda b, i: (i, 0))
    f32, bf16 = jnp.float32, jnp.bfloat16
    return pl.pallas_call(
        functools.partial(_inproj_kernel, tm=tm),
        grid=(B, nt),
        in_specs=[_x_tile_spec(tm, D), full(head), full(gmix), full(win), full(qlg), full(wuq),
                  full(wuqs), full(kvlg), full(wuk), full(wuv), full(qg), full(qgs), full(kg),
                  full(kgs), full(ones), tab, tab],
        out_specs=[row(LRU_WIDTH), row(LRU_WIDTH), row(MLA_HEADS * HP), row(MLA_HEADS * HP),
                   pl.BlockSpec((None, MLA_HEADS * HP, tm), lambda b, i: (b, 0, i))],
        out_shape=[jax.ShapeDtypeStruct((B, Tp, LRU_WIDTH), f32),
                   jax.ShapeDtypeStruct((B, Tp, LRU_WIDTH), f32),
                   jax.ShapeDtypeStruct((B, Tp, MLA_HEADS * HP), bf16),
                   jax.ShapeDtypeStruct((B, Tp, MLA_HEADS * HP), bf16),
                   jax.ShapeDtypeStruct((B, MLA_HEADS * HP, Tp), bf16)],
        compiler_params=pltpu.CompilerParams(
            dimension_semantics=("arbitrary", "arbitrary"), vmem_limit_bytes=VMEM_LIMIT),
        name="inproj",
    )(x, head, gmix, win, qlg, wuq, wuqs, kvlg, wuk, wuv, qg, qgs, kg, kgs, ones, ctab, stab)


def _lru_gates(x_ref, xprev_ref, xnext_ref, cw_ref, cb_ref, wg_ref, ba_ref, bx_ref, lam_ref,
               xw_ref, a_ref, u_ref, *, tt, first, last, t0):
    W = LRU_WIDTH
    xw_ref[0:SUB, :] = jnp.where(first, 0.0, xprev_ref[...])
    xw_ref[SUB:SUB + tt, :] = x_ref[...]
    xw_ref[SUB + tt:, :] = jnp.where(last, 0.0, xnext_ref[...])
    left = LRU_CONV // 2
    xc = cb_ref[...] + xw_ref[SUB - left:SUB - left + tt, :] * cw_ref[0:1, :]
    for k in range(1, LRU_CONV):
        xc = xc + xw_ref[SUB - left + k:SUB - left + k + tt, :] * cw_ref[k:k + 1, :]
    xcb = xc.astype(jnp.bfloat16)
    half = W // 2
    g0 = jnp.dot(xcb[:, :half], wg_ref[0], preferred_element_type=jnp.float32)
    g1 = jnp.dot(xcb[:, half:], wg_ref[1], preferred_element_type=jnp.float32)
    ra = jnp.concatenate([g0[:, :half], g1[:, :half]], axis=1) + ba_ref[...]
    ri = jnp.concatenate([g0[:, half:], g1[:, half:]], axis=1) + bx_ref[...]
    r = jax.nn.sigmoid(ra)
    ig = jax.nn.sigmoid(ri)
    lam = lam_ref[...]
    sp = jnp.maximum(-lam, 0.0) + jnp.log(1.0 + jnp.exp(-jnp.abs(lam)))
    log_a = -LRU_C * r * sp
    a = jnp.exp(log_a)
    u = jnp.sqrt(1.0 - a * a) * (ig * xc)
    rows = lax.broadcasted_iota(jnp.int32, (tt, W), 0) + t0
    a_ref[...] = a
    u_ref[...] = jnp.where(rows >= PAD, u, 0.0)


def _scan_groups(a_ref, u_ref, h_ref, carry_ref, *, tt, reverse):
    W = LRU_WIDTH
    ng = tt // SUB
    sub = lax.broadcasted_iota(jnp.int32, (SUB, W), 0)

    def body(g, carry):
        gi = (ng - 1 - g) if reverse else g
        r0 = pl.multiple_of(gi * SUB, SUB)
        a = a_ref[pl.ds(r0, SUB), :]
        u = u_ref[pl.ds(r0, SUB), :]
        for d in (1, 2, 4):
            if reverse:
                keep = sub < SUB - d
                sh = SUB - d
            else:
                keep = sub >= d
                sh = d
            a_sh = jnp.where(keep, pltpu.roll(a, sh, axis=0), 1.0)
            u_sh = jnp.where(keep, pltpu.roll(u, sh, axis=0), 0.0)
            u = u + a * u_sh
            a = a * a_sh
        h = u + a * carry
        h_ref[pl.ds(r0, SUB), :] = h
        return h[0:1, :] if reverse else h[SUB - 1:SUB, :]

    carry_ref[...] = lax.fori_loop(0, ng, body, carry_ref[...])


def _lru_fwd_kernel(x_ref, xprev_ref, xnext_ref, cw_ref, cb_ref, wg_ref, ba_ref, bx_ref, lam_ref,
                    hf_ref, xw_ref, a_ref, u_ref, carry_ref, *, tt, nt):
    i = pl.program_id(1)

    @pl.when(i == 0)
    def _():
        carry_ref[...] = jnp.zeros_like(carry_ref)

    _lru_gates(x_ref, xprev_ref, xnext_ref, cw_ref, cb_ref, wg_ref, ba_ref, bx_ref, lam_ref,
               xw_ref, a_ref, u_ref, tt=tt, first=i == 0, last=i == nt - 1, t0=i * tt)
    _scan_groups(a_ref, u_ref, hf_ref, carry_ref, tt=tt, reverse=False)


def _lru_bwd_kernel(x_ref, xprev_ref, xnext_ref, cw_ref, cb_ref, wg_ref, ba_ref, bx_ref, lam_ref,
                    hf_ref, gl_ref, gg_ref, y_ref, xw_ref, a_ref, u_ref, hb_ref, carry_ref,
                    *, tt, nt):
    j = pl.program_id(1)
    i = nt - 1 - j

    @pl.when(j == 0)
    def _():
        carry_ref[...] = jnp.zeros_like(carry_ref)

    _lru_gates(x_ref, xprev_ref, xnext_ref, cw_ref, cb_ref, wg_ref, ba_ref, bx_ref, lam_ref,
               xw_ref, a_ref, u_ref, tt=tt, first=i == 0, last=i == nt - 1, t0=i * tt)
    _scan_groups(a_ref, u_ref, hb_ref, carry_ref, tt=tt, reverse=True)
    y = (hf_ref[...] + hb_ref[...]) * jax.nn.gelu(gl_ref[...], approximate=True)
    y_ref[...] = _rms(y, gg_ref[...]).astype(y_ref.dtype)


def _lru_specs(tt, nt, Tp, tmap):
    W = LRU_WIDTH
    nsub = Tp // SUB
    per = tt // SUB
    blk = pl.BlockSpec((None, tt, W), lambda b, j: (b, tmap(j), 0))
    prev = pl.BlockSpec((None, SUB, W), lambda b, j: (b, jnp.maximum(tmap(j) * per - 1, 0), 0))
    nxt = pl.BlockSpec((None, SUB, W),
                       lambda b, j: (b, jnp.minimum((tmap(j) + 1) * per, nsub - 1), 0))
    return blk, prev, nxt


def _lru_fwd(xl, cw, cb, wg, ba, bx, lam, *, tt):
    B, Tp, W = xl.shape
    nt = Tp // tt
    blk, prev, nxt = _lru_specs(tt, nt, Tp, lambda j: j)
    full = lambda a: pl.BlockSpec(a.shape, lambda b, j: (0,) * a.ndim)
    return pl.pallas_call(
        functools.partial(_lru_fwd_kernel, tt=tt, nt=nt),
        grid=(B, nt),
        in_specs=[blk, prev, nxt, full(cw), full(cb), full(wg), full(ba), full(bx), full(lam)],
        out_specs=blk,
        out_shape=jax.ShapeDtypeStruct((B, Tp, W), jnp.float32),
        scratch_shapes=[pltpu.VMEM((tt + 2 * SUB, W), jnp.float32),
                        pltpu.VMEM((tt, W), jnp.float32),
                        pltpu.VMEM((tt, W), jnp.float32),
                        pltpu.VMEM((1, W), jnp.float32)],
        compiler_params=pltpu.CompilerParams(
            dimension_semantics=("arbitrary", "arbitrary"), vmem_limit_bytes=VMEM_LIMIT),
        name="lru_fwd",
    )(xl, xl, xl, cw, cb, wg, ba, bx, lam)


def _lru_bwd(xl, cw, cb, wg, ba, bx, lam, hf, gl, gg, *, tt):
    B, Tp, W = xl.shape
    nt = Tp // tt
    blk, prev, nxt = _lru_specs(tt, nt, Tp, lambda j: nt - 1 - j)
    full = lambda a: pl.BlockSpec(a.shape, lambda b, j: (0,) * a.ndim)
    return pl.pallas_call(
        functools.partial(_lru_bwd_kernel, tt=tt, nt=nt),
        grid=(B, nt),
        in_specs=[blk, prev, nxt, full(cw), full(cb), full(wg), full(ba), full(bx), full(lam),
                  blk, blk, full(gg)],
        out_specs=blk,
        out_shape=jax.ShapeDtypeStruct((B, Tp, W), jnp.bfloat16),
        scratch_shapes=[pltpu.VMEM((tt + 2 * SUB, W), jnp.float32),
                        pltpu.VMEM((tt, W), jnp.float32),
                        pltpu.VMEM((tt, W), jnp.float32),
                        pltpu.VMEM((tt, W), jnp.float32),
                        pltpu.VMEM((1, W), jnp.float32)],
        compiler_params=pltpu.CompilerParams(
            dimension_semantics=("arbitrary", "arbitrary"), vmem_limit_bytes=VMEM_LIMIT),
        name="lru_bwd",
    )(xl, xl, xl, cw, cb, wg, ba, bx, lam, hf, gl, gg)


def _max_over_rows(s):
    rows = s.shape[0]
    while rows % (2 * SUB) == 0 and rows > 32 * SUB:
        fold = 4 if rows % (4 * SUB) == 0 else 2
        rows //= fold
        s = jnp.max(s.reshape(fold, rows, s.shape[1]), axis=0)
    return jnp.max(s, axis=0, keepdims=True)


def _attn_kernel(q_ref, k_ref, vt_ref, o_ref, *, tq, lead, span):
    heads = [slice(hh * HP, (hh + 1) * HP) for hh in range(2)]
    nkeys = k_ref.shape[0]
    nt = (((1,), (1,)), ((), ()))
    rows = lax.broadcasted_iota(jnp.int32, (HP, tq), 0)

    def scores(hh, lo, hi):
        return lax.dot_general(k_ref[lo:hi, heads[hh]], q_ref[:, heads[hh]], nt,
                               preferred_element_type=jnp.float32)

    def pv(hh, lo, hi, s, m):
        return jnp.dot(vt_ref[heads[hh], lo:hi], jnp.exp2(s - m).astype(jnp.bfloat16),
                       preferred_element_type=jnp.float32)

    def finish(a0, a1):
        l0, l1 = a0[V_DIM:V_DIM + 1, :], a1[0:1, :]
        return jnp.where(rows < V_DIM, a0 / l0, a1 / l1).T, l0, l1

    accs = []
    for hh in range(2):
        s = scores(hh, 0, lead)
        m = _max_over_rows(s)
        acc = pv(hh, 0, lead, s, m)
        for lo in range(lead, nkeys, span):
            acc = acc + pv(hh, lo, lo + span, scores(hh, lo, lo + span), m)
        accs.append(acc)
    out, l0, l1 = finish(*accs)
    lmin = jnp.min(jnp.minimum(l0, l1))
    lmax = jnp.max(jnp.maximum(l0, l1))
    safe = (lmin > 2.0 ** -100) & (lmax < 2.0 ** 100)

    @pl.when(safe)
    def _():
        o_ref[...] = out

    @pl.when(jnp.logical_not(safe))
    def _():
        exact = []
        for hh in range(2):
            s = scores(hh, 0, nkeys)
            exact.append(pv(hh, 0, nkeys, s, _max_over_rows(s)))
        o_ref[...] = finish(*exact)[0]


def _attn(q, k, vt, *, tq, lead, span):
    B, Tp, _ = q.shape
    npair = MLA_HEADS // 2
    return pl.pallas_call(
        functools.partial(_attn_kernel, tq=tq, lead=lead, span=span),
        grid=(B, npair, Tp // tq),
        in_specs=[pl.BlockSpec((None, tq, 2 * HP), lambda b, p, i: (b, i, p)),
                  pl.BlockSpec((None, Tp, 2 * HP), lambda b, p, i: (b, 0, p)),
                  pl.BlockSpec((None, 2 * HP, Tp), lambda b, p, i: (b, p, 0))],
        out_specs=pl.BlockSpec((None, tq, HP), lambda b, p, i: (b, i, p)),
        out_shape=jax.ShapeDtypeStruct((B, Tp, npair * HP), jnp.float32),
        compiler_params=pltpu.CompilerParams(
            dimension_semantics=("arbitrary", "arbitrary", "arbitrary"),
            vmem_limit_bytes=VMEM_LIMIT),
        name="attn",
    )(q, k, vt)


def _mix_ffn_kernel(x_ref, xprev_ref, xnext_ref, mtail_ref, yl_ref, ylprev_ref, ylnext_ref,
                    ym_ref, ymprev_ref, ymnext_ref, mg_ref, wo_ref, g_ref, wg_ref, wu_ref,
                    cw_ref, cb_ref, wd_ref, o_ref, gate_ref, *, tm, nt):
    i = pl.program_id(1)
    f32, bf16 = jnp.float32, jnp.bfloat16
    H = 2 * SUB
    xw = jnp.concatenate([jnp.where(i == 0, mtail_ref[...], xprev_ref[...]), x_ref[...],
                          xnext_ref[...]], axis=0)
    ylw = jnp.concatenate([ylprev_ref[...].astype(f32)[H - SUB:], yl_ref[...].astype(f32),
                           ylnext_ref[...].astype(f32)[:SUB]], axis=0).astype(bf16)
    ymw = jnp.concatenate([ymprev_ref[...], ym_ref[...], ymnext_ref[...]], axis=0)
    y = jnp.concatenate([ylw, _rms(ymw, mg_ref[...]).astype(bf16)], axis=1)
    h1 = xw + jnp.dot(y, wo_ref[...], preferred_element_type=f32)
    r = lax.broadcasted_iota(jnp.int32, h1.shape, 0)
    h1 = jnp.where((i == nt - 1) & (r >= tm + SUB), 0.0, h1)
    hn = _rms(h1, g_ref[...]).astype(bf16)
    gate_ref[...] = jnp.dot(hn, wg_ref[...], preferred_element_type=f32)
    up = jnp.dot(hn[SUB:SUB + tm], wu_ref[...], preferred_element_type=f32)
    left = FFN_CONV // 2
    gc = cb_ref[...] + gate_ref[SUB - left:SUB - left + tm, :] * cw_ref[0:1, :]
    for k in range(1, FFN_CONV):
        gc = gc + gate_ref[SUB - left + k:SUB - left + k + tm, :] * cw_ref[k:k + 1, :]
    act = (gc * jax.nn.sigmoid(gc) * up).astype(bf16)
    o_ref[...] = h1[SUB:SUB + tm] + jnp.dot(act, wd_ref[...], preferred_element_type=f32)


def _mix_ffn(x, mtail, yl, ym, mg, wo, g, wg, wu, cw, cb, wd, *, tm):
    B, S, D = x.shape
    Tp = yl.shape[1]
    W = yl.shape[2]
    nt = S // tm
    off = X0 // tm
    per = tm // SUB
    H = 2 * SUB
    perh = tm // H
    full = lambda a: pl.BlockSpec(a.shape, lambda b, i: (0,) * a.ndim)

    def halo(w, rows, per_tile, nblk, shift):
        prev = pl.BlockSpec((None, rows, w),
                            lambda b, i: (b, jnp.maximum((i + shift) * per_tile - 1, 0), 0))
        nxt = pl.BlockSpec((None, rows, w),
                           lambda b, i: (b, jnp.minimum((i + shift + 1) * per_tile, nblk - 1), 0))
        return prev, nxt

    xprev, xnext = halo(D, SUB, per, S // SUB, 0)
    ylprev, ylnext = halo(W, H, perh, Tp // H, off)
    ymprev, ymnext = halo(W, SUB, per, Tp // SUB, off)
    return pl.pallas_call(
        functools.partial(_mix_ffn_kernel, tm=tm, nt=nt),
        grid=(B, nt),
        in_specs=[pl.BlockSpec((None, tm, D), lambda b, i: (b, i, 0)), xprev, xnext, full(mtail),
                  pl.BlockSpec((None, tm, W), lambda b, i: (b, i + off, 0)), ylprev, ylnext,
                  pl.BlockSpec((None, tm, W), lambda b, i: (b, i + off, 0)), ymprev, ymnext,
                  full(mg), full(wo), full(g), full(wg), full(wu), full(cw), full(cb), full(wd)],
        out_specs=pl.BlockSpec((None, tm, D), lambda b, i: (b, i, 0)),
        out_shape=jax.ShapeDtypeStruct((B, S, D), jnp.float32),
        scratch_shapes=[pltpu.VMEM((tm + 2 * SUB, D_FF), jnp.float32)],
        compiler_params=pltpu.CompilerParams(
            dimension_semantics=("arbitrary", "arbitrary"), vmem_limit_bytes=VMEM_LIMIT),
        name="mix_ffn",
    )(x, x, x, mtail, yl, yl, yl, ym, ym, ym, mg, wo, g, wg, wu, cw, cb, wd)


def _pad_heads(w, width, offset=0):
    K = w.shape[0]
    w = w.reshape(K, MLA_HEADS, width)
    w = jnp.pad(w, ((0, 0), (0, 0), (offset, HP - width - offset)))
    return w.reshape(K, MLA_HEADS * HP)


def _block_diag(w):
    H, Dh, _ = w.shape
    eye = jnp.eye(H, dtype=w.dtype)
    return (eye[:, None, :, None] * w[:, :, None, :]).reshape(H * Dh, H * Dh)


def _gate_weights(w_a, w_x):
    half = LRU_WIDTH // 2
    wa = _block_diag(w_a)
    wx = _block_diag(w_x)
    halves = [jnp.concatenate([wa[c * half:(c + 1) * half, c * half:(c + 1) * half],
                               wx[c * half:(c + 1) * half, c * half:(c + 1) * half]], axis=1)
              for c in range(2)]
    return jnp.stack(halves).astype(jnp.bfloat16)


def kernel(x, meta_tokens, norm_mix_g, w_in, conv_lru_w, conv_lru_b, lru_w_a, lru_b_a, lru_w_x,
           lru_b_x, lru_lambda, lru_gate_g, q_latent_g, w_uq, kv_latent_g, w_ukv, q_norm_g,
           k_norm_g, mla_out_g, w_out, norm_ffn_g, w_ffn_up, conv_ffn_w, conv_ffn_b, w_ffn_down):
    B, S, D = x.shape
    f32, bf16 = jnp.float32, jnp.bfloat16
    Tp = X0 + S
    l = 0
    r2 = lambda a: a.reshape(1, -1).astype(f32)

    head = jnp.concatenate([jnp.zeros((PAD, D), x.dtype), meta_tokens.astype(x.dtype)], axis=0)

    half = QK_ROPE // 2
    lanes = jnp.arange(HP)
    swap = jnp.where((lanes >= QK_NOPE) & (lanes < QK_NOPE + half), lanes + half,
                     jnp.where((lanes >= QK_NOPE + half) & (lanes < QK_DIM), lanes - half, lanes))
    swap_heads = (jnp.arange(MLA_HEADS)[:, None] * HP + swap[None, :]).reshape(-1)

    o = 2 * LRU_WIDTH + Q_LORA + KV_LORA
    kpe_w = jnp.pad(w_in[l][:, o:], ((0, 0), (QK_NOPE, LANE - QK_DIM)))
    win = jnp.concatenate([w_in[l][:, :o], kpe_w, kpe_w[:, swap]], axis=1).astype(bf16)
    wuq = _pad_heads(w_uq[l], QK_DIM).astype(bf16)
    wuqs = wuq[:, swap_heads]
    wkv = w_ukv[l].reshape(KV_LORA, MLA_HEADS, QK_NOPE + V_DIM)
    wuk = _pad_heads(wkv[:, :, :QK_NOPE].reshape(KV_LORA, -1), QK_NOPE).astype(bf16)
    wv = jnp.pad(wkv[:, :, QK_NOPE:], ((0, 0), (0, 0), (0, HP - V_DIM)))
    wv = jnp.where((jnp.arange(MLA_HEADS) % 2 == 1)[None, :, None],
                   jnp.roll(wv, V_DIM, axis=2), wv)
    wuvt = wv.reshape(KV_LORA, MLA_HEADS * HP).T.astype(bf16)
    qg = jnp.pad(q_norm_g[l], (0, HP - QK_DIM)).reshape(1, HP)
    kg = jnp.pad(k_norm_g[l], (0, HP - QK_DIM)).reshape(1, HP)
    ones = jnp.ones((2 * HP, HP), bf16)

    pos = jnp.arange(Tp, dtype=f32) - PAD
    inv_freq = ROPE_THETA ** (-jnp.arange(0, QK_ROPE, 2, dtype=f32) / QK_ROPE)
    ang = pos[:, None] * inv_freq[None, :]
    cos, sin = jnp.cos(ang), jnp.sin(ang)
    ctab = jnp.concatenate([jnp.ones((Tp, QK_NOPE), f32), cos, cos,
                            jnp.ones((Tp, HP - QK_DIM), f32)], axis=1)
    stab = jnp.concatenate([jnp.zeros((Tp, QK_NOPE), f32), -sin, sin,
                            jnp.zeros((Tp, HP - QK_DIM), f32)], axis=1)

    xl, gl, q, k, vt = _inproj(x, head, r2(norm_mix_g[l]), win, r2(q_latent_g[l]), wuq, wuqs,
                              r2(kv_latent_g[l]), wuk, wuvt, qg, qg[:, swap], kg, kg[:, swap],
                              ones, ctab, stab, tm=X0)

    cw, cb = conv_lru_w[l], r2(conv_lru_b[l])
    wg = [_gate_weights(lru_w_a[l, d], lru_w_x[l, d]) for d in range(2)]
    hf = _lru_fwd(xl, cw, cb, wg[0], r2(lru_b_a[l, 0]), r2(lru_b_x[l, 0]), r2(lru_lambda[l, 0]),
                  tt=256)
    yl = _lru_bwd(xl, cw, cb, wg[1], r2(lru_b_a[l, 1]), r2(lru_b_x[l, 1]), r2(lru_lambda[l, 1]),
                  hf, gl, r2(lru_gate_g[l]), tt=256)

    ym = _attn(q, k, vt, tq=256, lead=768, span=768)

    wup = w_ffn_up[l].astype(bf16)
    return _mix_ffn(x, head[X0 - SUB:], yl, ym, r2(mla_out_g[l]), w_out[l].astype(bf16),
                    r2(norm_ffn_g[l]), wup[:, :D_FF], wup[:, D_FF:], conv_ffn_w[l],
                    r2(conv_ffn_b[l]), w_ffn_down[l].astype(bf16), tm=256)
```

```python
import functools
import math

import jax
import jax.numpy as jnp
from jax import lax
from jax.experimental import pallas as pl
from jax.experimental.pallas import tpu as pltpu

D_MODEL = 1024
N_META = 16
LRU_WIDTH = 512
LRU_HEADS = 8
LRU_HEAD_DIM = 64
LRU_CONV = 4
LRU_C = 8.0
MLA_HEADS = 8
QK_NOPE = 64
QK_ROPE = 32
QK_DIM = 96
V_DIM = 64
Q_LORA = 256
KV_LORA = 128
ROPE_THETA = 10000.0
D_FF = 2816
FFN_CONV = 3
EPS = 1e-6
NEG_INF = -1e30

LANE = 128
SUB = 8
X0 = 256
PAD = X0 - N_META
HP = LANE
MASK_LANE = QK_DIM
VMEM_LIMIT = 56 * 1024 * 1024


def _rms(x, g):
    return x * lax.rsqrt(jnp.mean(x * x, axis=-1, keepdims=True) + EPS) * g


def _bdot(a, b):
    return jnp.dot(a.astype(jnp.bfloat16), b, preferred_element_type=jnp.float32)


def _inproj_kernel(x_ref, head_ref, gmix_ref, win_ref, qlg_ref, wuq_ref, wuqs_ref, kvlg_ref,
                   wuk_ref, wuvt_ref, qg_ref, qgs_ref, kg_ref, kgs_ref, ones_ref, c_ref, s_ref,
                   xlru_ref, glru_ref, q_ref, k_ref, vt_ref, *, tm):
    i = pl.program_id(1)
    h = jnp.where(i == 0, head_ref[...], x_ref[...])
    hn = _rms(h, gmix_ref[...])
    proj = _bdot(hn, win_ref[...])
    for c in range(LRU_WIDTH // LANE):
        xlru_ref[c] = proj[:, c * LANE:(c + 1) * LANE]
    glru_ref[...] = proj[:, LRU_WIDTH:2 * LRU_WIDTH]
    o = 2 * LRU_WIDTH
    cq = proj[:, o:o + Q_LORA]
    ckv = proj[:, o + Q_LORA:o + Q_LORA + KV_LORA]
    o += Q_LORA + KV_LORA
    kpe = proj[:, o:o + HP]
    kpe_sw = proj[:, o + HP:]
    cqn = _rms(cq, qlg_ref[...]).astype(jnp.bfloat16)
    qraw = jnp.dot(cqn, wuq_ref[...], preferred_element_type=jnp.float32)
    qraw_sw = jnp.dot(cqn, wuqs_ref[...], preferred_element_type=jnp.float32)
    ckvn = _rms(ckv, kvlg_ref[...]).astype(jnp.bfloat16)
    kraw = jnp.dot(ckvn, wuk_ref[...], preferred_element_type=jnp.float32)
    vt = lax.dot_general(wuvt_ref[...], ckvn, (((1,), (1,)), ((), ())),
                         preferred_element_type=jnp.float32)
    vrow = lax.broadcasted_iota(jnp.int32, vt.shape, 0) % (2 * HP)
    vt_ref[...] = jnp.where((vrow == V_DIM) | (vrow == HP), 1.0, vt).astype(jnp.bfloat16)

    lane = lax.broadcasted_iota(jnp.int32, (tm, HP), 1)
    row = lax.broadcasted_iota(jnp.int32, (tm, HP), 0) + i * tm
    key_mask = jnp.where(row >= PAD, 0.0, NEG_INF)
    scale = math.log2(math.e) / math.sqrt(QK_DIM)
    cosv, sinv = c_ref[...], s_ref[...]
    q_cos = cosv * (qg_ref[...] * scale)
    q_sin = sinv * (qgs_ref[...] * scale)
    k_cos = cosv * kg_ref[...]
    k_sin_term = kpe_sw * (sinv * kgs_ref[...])

    def inv_rms(xb):
        x2 = xb * xb
        hi = x2.astype(jnp.bfloat16)
        lo = (x2 - hi.astype(jnp.float32)).astype(jnp.bfloat16)
        ss = jnp.dot(jnp.concatenate([hi, lo], axis=1), ones_ref[...],
                     preferred_element_type=jnp.float32)
        return lax.rsqrt(ss * (1.0 / QK_DIM) + EPS)

    for hd in range(MLA_HEADS):
        sl = slice(hd * HP, (hd + 1) * HP)
        xq = qraw[:, sl]
        qb = (xq * q_cos + qraw_sw[:, sl] * q_sin) * inv_rms(xq)
        q_ref[:, sl] = jnp.where(lane == MASK_LANE, 1.0, qb).astype(jnp.bfloat16)
        xk = kraw[:, sl] + kpe
        kb = (xk * k_cos + k_sin_term) * inv_rms(xk)
        k_ref[:, sl] = jnp.where(lane == MASK_LANE, key_mask, kb).astype(jnp.bfloat16)


def _x_tile_spec(tm, D):
    return pl.BlockSpec((None, tm, D), lambda b, i: (b, jnp.maximum(i - 1, 0), 0))


def _inproj(x, head, gmix, win, qlg, wuq, wuqs, kvlg, wuk, wuv, qg, qgs, kg, kgs, ones, ctab,
            stab, *, tm):
    B, S, D = x.shape
    assert tm == X0
    Tp = X0 + S
    nt = Tp // tm
    row = lambda w: pl.BlockSpec((None, tm, w), lambda b, i: (b, i, 0))
    full = lambda a: pl.BlockSpec(a.shape, lambda b, i: (0,) * a.ndim)
    tab = pl.BlockSpec((tm, HP), lambda b, i: (i, 0))
    f32, bf16 = jnp.float32, jnp.bfloat16
    return pl.pallas_call(
        functools.partial(_inproj_kernel, tm=tm),
        grid=(B, nt),
        in_specs=[_x_tile_spec(tm, D), full(head), full(gmix), full(win), full(qlg), full(wuq),
                  full(wuqs), full(kvlg), full(wuk), full(wuv), full(qg), full(qgs), full(kg),
                  full(kgs), full(ones), tab, tab],
        out_specs=[pl.BlockSpec((None, LRU_WIDTH // LANE, tm, LANE), lambda b, i: (b, 0, i, 0)),
                   row(LRU_WIDTH), row(MLA_HEADS * HP), row(MLA_HEADS * HP),
                   pl.BlockSpec((None, MLA_HEADS * HP, tm), lambda b, i: (b, 0, i))],
        out_shape=[jax.ShapeDtypeStruct((B, LRU_WIDTH // LANE, Tp, LANE), f32),
                   jax.ShapeDtypeStruct((B, Tp, LRU_WIDTH), f32),
                   jax.ShapeDtypeStruct((B, Tp, MLA_HEADS * HP), bf16),
                   jax.ShapeDtypeStruct((B, Tp, MLA_HEADS * HP), bf16),
                   jax.ShapeDtypeStruct((B, MLA_HEADS * HP, Tp), bf16)],
        compiler_params=pltpu.CompilerParams(
            dimension_semantics=("arbitrary", "arbitrary"), vmem_limit_bytes=VMEM_LIMIT),
        name="inproj",
    )(x, head, gmix, win, qlg, wuq, wuqs, kvlg, wuk, wuv, qg, qgs, kg, kgs, ones, ctab, stab)


NSLAB = LRU_WIDTH // LANE


def _lru_gates(x_ref, xprev_ref, xnext_ref, cw_ref, cb_ref, wg_ref, ba_ref, bx_ref, lam_ref,
               xw_ref, *, tt, first, last, t0):
    W = LRU_WIDTH
    seg = tt // SUB
    xw_ref[:, 0:SUB, :] = jnp.where(first, 0.0, xprev_ref[...])
    xw_ref[:, SUB:SUB + tt, :] = x_ref[...]
    xw_ref[:, SUB + tt:, :] = jnp.where(last, 0.0, xnext_ref[...])
    left = LRU_CONV // 2
    slabs = []
    for c in range(NSLAB):
        cs = slice(c * LANE, (c + 1) * LANE)
        pieces = []
        for j in range(seg):
            acc = cb_ref[:, cs]
            for k in range(LRU_CONV):
                tap = xw_ref[c, pl.ds(SUB - left + j + k, SUB, stride=seg), :]
                acc = acc + tap * cw_ref[k:k + 1, cs]
            pieces.append(acc)
        slabs.append(jnp.concatenate(pieces, axis=0))
    xc = jnp.concatenate(slabs, axis=1)
    xcb = xc.astype(jnp.bfloat16)
    half = W // 2
    g0 = jnp.dot(xcb[:, :half], wg_ref[0], preferred_element_type=jnp.float32)
    g1 = jnp.dot(xcb[:, half:], wg_ref[1], preferred_element_type=jnp.float32)
    ra = jnp.concatenate([g0[:, :half], g1[:, :half]], axis=1) + ba_ref[...]
    ri = jnp.concatenate([g0[:, half:], g1[:, half:]], axis=1) + bx_ref[...]
    r = jax.nn.sigmoid(ra)
    ig = jax.nn.sigmoid(ri)
    lam = lam_ref[...]
    sp = jnp.maximum(-lam, 0.0) + jnp.log(1.0 + jnp.exp(-jnp.abs(lam)))
    log_a = -LRU_C * r * sp
    a = jnp.exp(log_a)
    u = jnp.sqrt(1.0 - a * a) * (ig * xc)
    row = lax.broadcasted_iota(jnp.int32, (tt, W), 0)
    time = t0 + (row % SUB) * seg + row // SUB
    return a, jnp.where(time >= PAD, u, 0.0)


def _scan_block(a, u, h_ref, carry_ref, *, tt, reverse):
    W = LRU_WIDTH
    seg = tt // SUB
    order = range(seg - 1, -1, -1) if reverse else range(seg)
    h = jnp.zeros((SUB, W), jnp.float32)
    p = jnp.ones((SUB, W), jnp.float32)
    hs, ps = {}, {}
    for j in order:
        aj = a[j * SUB:(j + 1) * SUB]
        h = aj * h + u[j * SUB:(j + 1) * SUB]
        p = aj * p
        hs[j], ps[j] = h, p
    sub = lax.broadcasted_iota(jnp.int32, (SUB, W), 0)
    e, pf = h, p
    for d in (1, 2, 4):
        if reverse:
            keep, sh = sub < SUB - d, SUB - d
        else:
            keep, sh = sub >= d, d
        e_sh = jnp.where(keep, pltpu.roll(e, sh, axis=0), 0.0)
        pf_sh = jnp.where(keep, pltpu.roll(pf, sh, axis=0), 1.0)
        e = e + pf * e_sh
        pf = pf * pf_sh
    carry = carry_ref[...]
    leaving = e + pf * carry
    if reverse:
        entering = jnp.where(sub < SUB - 1, pltpu.roll(leaving, SUB - 1, axis=0), carry)
        carry_ref[...] = leaving[0:1]
    else:
        entering = jnp.where(sub >= 1, pltpu.roll(leaving, 1, axis=0), carry)
        carry_ref[...] = leaving[SUB - 1:SUB]
    for j in range(seg):
        hj = hs[j] + ps[j] * entering
        for c in range(NSLAB):
            h_ref[c, pl.ds(j, SUB, stride=seg), :] = hj[:, c * LANE:(c + 1) * LANE]


def _lru_fwd_kernel(x_ref, xprev_ref, xnext_ref, cw_ref, cb_ref, wg_ref, ba_ref, bx_ref, lam_ref,
                    hf_ref, xw_ref, carry_ref, *, tt, nt):
    i = pl.program_id(1)

    @pl.when(i == 0)
    def _():
        carry_ref[...] = jnp.zeros_like(carry_ref)

    a, u = _lru_gates(x_ref, xprev_ref, xnext_ref, cw_ref, cb_ref, wg_ref, ba_ref, bx_ref,
                      lam_ref, xw_ref, tt=tt, first=i == 0, last=i == nt - 1, t0=i * tt)
    _scan_block(a, u, hf_ref, carry_ref, tt=tt, reverse=False)


def _lru_bwd_kernel(x_ref, xprev_ref, xnext_ref, cw_ref, cb_ref, wg_ref, ba_ref, bx_ref, lam_ref,
                    hf_ref, gl_ref, gg_ref, y_ref, xw_ref, hb_ref, carry_ref, *, tt, nt):
    j = pl.program_id(1)
    i = nt - 1 - j

    @pl.when(j == 0)
    def _():
        carry_ref[...] = jnp.zeros_like(carry_ref)

    a, u = _lru_gates(x_ref, xprev_ref, xnext_ref, cw_ref, cb_ref, wg_ref, ba_ref, bx_ref,
                      lam_ref, xw_ref, tt=tt, first=i == 0, last=i == nt - 1, t0=i * tt)
    _scan_block(a, u, hb_ref, carry_ref, tt=tt, reverse=True)
    ys = [(hf_ref[c] + hb_ref[c]) * jax.nn.gelu(gl_ref[:, c * LANE:(c + 1) * LANE],
                                                approximate=True) for c in range(NSLAB)]
    y = jnp.concatenate(ys, axis=1)
    y_ref[...] = _rms(y, gg_ref[...]).astype(y_ref.dtype)


def _lru_specs(tt, nt, Tp, tmap):
    nsub = Tp // SUB
    per = tt // SUB
    slab = lambda rows, f: pl.BlockSpec((None, NSLAB, rows, LANE), lambda b, j: (b, 0, f(j), 0))
    blk = slab(tt, tmap)
    prev = slab(SUB, lambda j: jnp.maximum(tmap(j) * per - 1, 0))
    nxt = slab(SUB, lambda j: jnp.minimum((tmap(j) + 1) * per, nsub - 1))
    return blk, prev, nxt


def _lru_fwd(xl, cw, cb, wg, ba, bx, lam, *, tt):
    B, _, Tp, _ = xl.shape
    nt = Tp // tt
    blk, prev, nxt = _lru_specs(tt, nt, Tp, lambda j: j)
    full = lambda a: pl.BlockSpec(a.shape, lambda b, j: (0,) * a.ndim)
    return pl.pallas_call(
        functools.partial(_lru_fwd_kernel, tt=tt, nt=nt),
        grid=(B, nt),
        in_specs=[blk, prev, nxt, full(cw), full(cb), full(wg), full(ba), full(bx), full(lam)],
        out_specs=blk,
        out_shape=jax.ShapeDtypeStruct((B, NSLAB, Tp, LANE), jnp.float32),
        scratch_shapes=[pltpu.VMEM((NSLAB, tt + 2 * SUB, LANE), jnp.float32),
                        pltpu.VMEM((1, LRU_WIDTH), jnp.float32)],
        compiler_params=pltpu.CompilerParams(
            dimension_semantics=("arbitrary", "arbitrary"), vmem_limit_bytes=VMEM_LIMIT),
        name="lru_fwd",
    )(xl, xl, xl, cw, cb, wg, ba, bx, lam)


def _lru_bwd(xl, cw, cb, wg, ba, bx, lam, hf, gl, gg, *, tt):
    B, _, Tp, _ = xl.shape
    W = LRU_WIDTH
    nt = Tp // tt
    rev = lambda j: nt - 1 - j
    blk, prev, nxt = _lru_specs(tt, nt, Tp, rev)
    row = pl.BlockSpec((None, tt, W), lambda b, j: (b, rev(j), 0))
    full = lambda a: pl.BlockSpec(a.shape, lambda b, j: (0,) * a.ndim)
    return pl.pallas_call(
        functools.partial(_lru_bwd_kernel, tt=tt, nt=nt),
        grid=(B, nt),
        in_specs=[blk, prev, nxt, full(cw), full(cb), full(wg), full(ba), full(bx), full(lam),
                  blk, row, full(gg)],
        out_specs=row,
        out_shape=jax.ShapeDtypeStruct((B, Tp, W), jnp.bfloat16),
        scratch_shapes=[pltpu.VMEM((NSLAB, tt + 2 * SUB, LANE), jnp.float32),
                        pltpu.VMEM((NSLAB, tt, LANE), jnp.float32),
                        pltpu.VMEM((1, LRU_WIDTH), jnp.float32)],
        compiler_params=pltpu.CompilerParams(
            dimension_semantics=("arbitrary", "arbitrary"), vmem_limit_bytes=VMEM_LIMIT),
        name="lru_bwd",
    )(xl, xl, xl, cw, cb, wg, ba, bx, lam, hf, gl, gg)


def _max_over_rows(s):
    rows = s.shape[0]
    while rows % (2 * SUB) == 0 and rows > 32 * SUB:
        fold = 4 if rows % (4 * SUB) == 0 else 2
        rows //= fold
        s = jnp.max(s.reshape(fold, rows, s.shape[1]), axis=0)
    return jnp.max(s, axis=0, keepdims=True)


def _attn_kernel(q_ref, k_ref, vt_ref, o_ref, *, tq, lead, span):
    heads = [slice(hh * HP, (hh + 1) * HP) for hh in range(2)]
    nkeys = k_ref.shape[0]
    nt = (((1,), (1,)), ((), ()))
    rows = lax.broadcasted_iota(jnp.int32, (HP, tq), 0)

    def scores(hh, lo, hi):
        return lax.dot_general(k_ref[lo:hi, heads[hh]], q_ref[:, heads[hh]], nt,
                               preferred_element_type=jnp.float32)

    def pv(hh, lo, hi, s, m):
        return jnp.dot(vt_ref[heads[hh], lo:hi], jnp.exp2(s - m).astype(jnp.bfloat16),
                       preferred_element_type=jnp.float32)

    def finish(a0, a1):
        l0, l1 = a0[V_DIM:V_DIM + 1, :], a1[0:1, :]
        return jnp.where(rows < V_DIM, a0 / l0, a1 / l1).T, l0, l1

    accs = []
    for hh in range(2):
        s = scores(hh, 0, lead)
        m = _max_over_rows(s)
        acc = pv(hh, 0, lead, s, m)
        for lo in range(lead, nkeys, span):
            acc = acc + pv(hh, lo, lo + span, scores(hh, lo, lo + span), m)
        accs.append(acc)
    out, l0, l1 = finish(*accs)
    lmin = jnp.min(jnp.minimum(l0, l1))
    lmax = jnp.max(jnp.maximum(l0, l1))
    safe = (lmin > 2.0 ** -100) & (lmax < 2.0 ** 100)

    @pl.when(safe)
    def _():
        o_ref[...] = out

    @pl.when(jnp.logical_not(safe))
    def _():
        exact = []
        for hh in range(2):
            s = scores(hh, 0, nkeys)
            exact.append(pv(hh, 0, nkeys, s, _max_over_rows(s)))
        o_ref[...] = finish(*exact)[0]


def _attn(q, k, vt, *, tq, lead, span):
    B, Tp, _ = q.shape
    npair = MLA_HEADS // 2
    return pl.pallas_call(
        functools.partial(_attn_kernel, tq=tq, lead=lead, span=span),
        grid=(B, npair, Tp // tq),
        in_specs=[pl.BlockSpec((None, tq, 2 * HP), lambda b, p, i: (b, i, p)),
                  pl.BlockSpec((None, Tp, 2 * HP), lambda b, p, i: (b, 0, p)),
                  pl.BlockSpec((None, 2 * HP, Tp), lambda b, p, i: (b, p, 0))],
        out_specs=pl.BlockSpec((None, tq, HP), lambda b, p, i: (b, i, p)),
        out_shape=jax.ShapeDtypeStruct((B, Tp, npair * HP), jnp.float32),
        compiler_params=pltpu.CompilerParams(
            dimension_semantics=("arbitrary", "arbitrary", "arbitrary"),
            vmem_limit_bytes=VMEM_LIMIT),
        name="attn",
    )(q, k, vt)


def _mix_ffn_kernel(x_ref, xprev_ref, xnext_ref, mtail_ref, yl_ref, ylprev_ref, ylnext_ref,
                    ym_ref, ymprev_ref, ymnext_ref, mg_ref, wo_ref, g_ref, wg_ref, wu_ref,
                    cw_ref, cb_ref, wd_ref, o_ref, gate_ref, *, tm, nt):
    i = pl.program_id(1)
    f32, bf16 = jnp.float32, jnp.bfloat16
    H = 2 * SUB
    xw = jnp.concatenate([jnp.where(i == 0, mtail_ref[...], xprev_ref[...]), x_ref[...],
                          xnext_ref[...]], axis=0)
    ylw = jnp.concatenate([ylprev_ref[...].astype(f32)[H - SUB:], yl_ref[...].astype(f32),
                           ylnext_ref[...].astype(f32)[:SUB]], axis=0).astype(bf16)
    ymw = jnp.concatenate([ymprev_ref[...], ym_ref[...], ymnext_ref[...]], axis=0)
    y = jnp.concatenate([ylw, _rms(ymw, mg_ref[...]).astype(bf16)], axis=1)
    h1 = xw + jnp.dot(y, wo_ref[...], preferred_element_type=f32)
    r = lax.broadcasted_iota(jnp.int32, h1.shape, 0)
    h1 = jnp.where((i == nt - 1) & (r >= tm + SUB), 0.0, h1)
    hn = _rms(h1, g_ref[...]).astype(bf16)
    gate_ref[...] = jnp.dot(hn, wg_ref[...], preferred_element_type=f32)
    up = jnp.dot(hn[SUB:SUB + tm], wu_ref[...], preferred_element_type=f32)
    left = FFN_CONV // 2
    gc = cb_ref[...] + gate_ref[SUB - left:SUB - left + tm, :] * cw_ref[0:1, :]
    for k in range(1, FFN_CONV):
        gc = gc + gate_ref[SUB - left + k:SUB - left + k + tm, :] * cw_ref[k:k + 1, :]
    act = (gc * jax.nn.sigmoid(gc) * up).astype(bf16)
    o_ref[...] = h1[SUB:SUB + tm] + jnp.dot(act, wd_ref[...], preferred_element_type=f32)


def _mix_ffn(x, mtail, yl, ym, mg, wo, g, wg, wu, cw, cb, wd, *, tm):
    B, S, D = x.shape
    Tp = yl.shape[1]
    W = yl.shape[2]
    nt = S // tm
    off = X0 // tm
    per = tm // SUB
    H = 2 * SUB
    perh = tm // H
    full = lambda a: pl.BlockSpec(a.shape, lambda b, i: (0,) * a.ndim)

    def halo(w, rows, per_tile, nblk, shift):
        prev = pl.BlockSpec((None, rows, w),
                            lambda b, i: (b, jnp.maximum((i + shift) * per_tile - 1, 0), 0))
        nxt = pl.BlockSpec((None, rows, w),
                           lambda b, i: (b, jnp.minimum((i + shift + 1) * per_tile, nblk - 1), 0))
        return prev, nxt

    xprev, xnext = halo(D, SUB, per, S // SUB, 0)
    ylprev, ylnext = halo(W, H, perh, Tp // H, off)
    ymprev, ymnext = halo(W, SUB, per, Tp // SUB, off)
    return pl.pallas_call(
        functools.partial(_mix_ffn_kernel, tm=tm, nt=nt),
        grid=(B, nt),
        in_specs=[pl.BlockSpec((None, tm, D), lambda b, i: (b, i, 0)), xprev, xnext, full(mtail),
                  pl.BlockSpec((None, tm, W), lambda b, i: (b, i + off, 0)), ylprev, ylnext,
                  pl.BlockSpec((None, tm, W), lambda b, i: (b, i + off, 0)), ymprev, ymnext,
                  full(mg), full(wo), full(g), full(wg), full(wu), full(cw), full(cb), full(wd)],
        out_specs=pl.BlockSpec((None, tm, D), lambda b, i: (b, i, 0)),
        out_shape=jax.ShapeDtypeStruct((B, S, D), jnp.float32),
        scratch_shapes=[pltpu.VMEM((tm + 2 * SUB, D_FF), jnp.float32)],
        compiler_params=pltpu.CompilerParams(
            dimension_semantics=("arbitrary", "arbitrary"), vmem_limit_bytes=VMEM_LIMIT),
        name="mix_ffn",
    )(x, x, x, mtail, yl, yl, yl, ym, ym, ym, mg, wo, g, wg, wu, cw, cb, wd)


def _pad_heads(w, width, offset=0):
    K = w.shape[0]
    w = w.reshape(K, MLA_HEADS, width)
    w = jnp.pad(w, ((0, 0), (0, 0), (offset, HP - width - offset)))
    return w.reshape(K, MLA_HEADS * HP)


def _block_diag(w):
    H, Dh, _ = w.shape
    eye = jnp.eye(H, dtype=w.dtype)
    return (eye[:, None, :, None] * w[:, :, None, :]).reshape(H * Dh, H * Dh)


def _gate_weights(w_a, w_x):
    half = LRU_WIDTH // 2
    wa = _block_diag(w_a)
    wx = _block_diag(w_x)
    halves = [jnp.concatenate([wa[c * half:(c + 1) * half, c * half:(c + 1) * half],
                               wx[c * half:(c + 1) * half, c * half:(c + 1) * half]], axis=1)
              for c in range(2)]
    return jnp.stack(halves).astype(jnp.bfloat16)


def kernel(x, meta_tokens, norm_mix_g, w_in, conv_lru_w, conv_lru_b, lru_w_a, lru_b_a, lru_w_x,
           lru_b_x, lru_lambda, lru_gate_g, q_latent_g, w_uq, kv_latent_g, w_ukv, q_norm_g,
           k_norm_g, mla_out_g, w_out, norm_ffn_g, w_ffn_up, conv_ffn_w, conv_ffn_b, w_ffn_down):
    B, S, D = x.shape
    f32, bf16 = jnp.float32, jnp.bfloat16
    Tp = X0 + S
    l = 0
    r2 = lambda a: a.reshape(1, -1).astype(f32)

    head = jnp.concatenate([jnp.zeros((PAD, D), x.dtype), meta_tokens.astype(x.dtype)], axis=0)

    half = QK_ROPE // 2
    lanes = jnp.arange(HP)
    swap = jnp.where((lanes >= QK_NOPE) & (lanes < QK_NOPE + half), lanes + half,
                     jnp.where((lanes >= QK_NOPE + half) & (lanes < QK_DIM), lanes - half, lanes))
    swap_heads = (jnp.arange(MLA_HEADS)[:, None] * HP + swap[None, :]).reshape(-1)

    o = 2 * LRU_WIDTH + Q_LORA + KV_LORA
    kpe_w = jnp.pad(w_in[l][:, o:], ((0, 0), (QK_NOPE, LANE - QK_DIM)))
    win = jnp.concatenate([w_in[l][:, :o], kpe_w, kpe_w[:, swap]], axis=1).astype(bf16)
    wuq = _pad_heads(w_uq[l], QK_DIM).astype(bf16)
    wuqs = wuq[:, swap_heads]
    wkv = w_ukv[l].reshape(KV_LORA, MLA_HEADS, QK_NOPE + V_DIM)
    wuk = _pad_heads(wkv[:, :, :QK_NOPE].reshape(KV_LORA, -1), QK_NOPE).astype(bf16)
    wv = jnp.pad(wkv[:, :, QK_NOPE:], ((0, 0), (0, 0), (0, HP - V_DIM)))
    wv = jnp.where((jnp.arange(MLA_HEADS) % 2 == 1)[None, :, None],
                   jnp.roll(wv, V_DIM, axis=2), wv)
    wuvt = wv.reshape(KV_LORA, MLA_HEADS * HP).T.astype(bf16)
    qg = jnp.pad(q_norm_g[l], (0, HP - QK_DIM)).reshape(1, HP)
    kg = jnp.pad(k_norm_g[l], (0, HP - QK_DIM)).reshape(1, HP)
    ones = jnp.ones((2 * HP, HP), bf16)

    pos = jnp.arange(Tp, dtype=f32) - PAD
    inv_freq = ROPE_THETA ** (-jnp.arange(0, QK_ROPE, 2, dtype=f32) / QK_ROPE)
    ang = pos[:, None] * inv_freq[None, :]
    cos, sin = jnp.cos(ang), jnp.sin(ang)
    ctab = jnp.concatenate([jnp.ones((Tp, QK_NOPE), f32), cos, cos,
                            jnp.ones((Tp, HP - QK_DIM), f32)], axis=1)
    stab = jnp.concatenate([jnp.zeros((Tp, QK_NOPE), f32), -sin, sin,
                            jnp.zeros((Tp, HP - QK_DIM), f32)], axis=1)

    xl, gl, q, k, vt = _inproj(x, head, r2(norm_mix_g[l]), win, r2(q_latent_g[l]), wuq, wuqs,
                              r2(kv_latent_g[l]), wuk, wuvt, qg, qg[:, swap], kg, kg[:, swap],
                              ones, ctab, stab, tm=X0)

    cw, cb = conv_lru_w[l], r2(conv_lru_b[l])
    wg = [_gate_weights(lru_w_a[l, d], lru_w_x[l, d]) for d in range(2)]
    hf = _lru_fwd(xl, cw, cb, wg[0], r2(lru_b_a[l, 0]), r2(lru_b_x[l, 0]), r2(lru_lambda[l, 0]),
                  tt=264)
    yl = _lru_bwd(xl, cw, cb, wg[1], r2(lru_b_a[l, 1]), r2(lru_b_x[l, 1]), r2(lru_lambda[l, 1]),
                  hf, gl, r2(lru_gate_g[l]), tt=264)

    ym = _attn(q, k, vt, tq=256, lead=768, span=768)

    wup = w_ffn_up[l].astype(bf16)
    return _mix_ffn(x, head[X0 - SUB:], yl, ym, r2(mla_out_g[l]), w_out[l].astype(bf16),
                    r2(norm_ffn_g[l]), wup[:, :D_FF], wup[:, D_FF:], conv_ffn_w[l],
                    r2(conv_ffn_b[l]), w_ffn_down[l].astype(bf16), tm=256)
```

```python
import functools
import math

import jax
import jax.numpy as jnp
from jax import lax
from jax.experimental import pallas as pl
from jax.experimental.pallas import tpu as pltpu

D_MODEL = 1024
N_META = 16
LRU_WIDTH = 512
LRU_HEADS = 8
LRU_HEAD_DIM = 64
LRU_CONV = 4
LRU_C = 8.0
MLA_HEADS = 8
QK_NOPE = 64
QK_ROPE = 32
QK_DIM = 96
V_DIM = 64
Q_LORA = 256
KV_LORA = 128
ROPE_THETA = 10000.0
D_FF = 2816
FFN_CONV = 3
EPS = 1e-6
NEG_INF = -1e30

LANE = 128
SUB = 8
X0 = 256
PAD = X0 - N_META
HP = LANE
MASK_LANE = QK_DIM
VMEM_LIMIT = 56 * 1024 * 1024


def _rms(x, g):
    return x * lax.rsqrt(jnp.mean(x * x, axis=-1, keepdims=True) + EPS) * g


def _bdot(a, b):
    return jnp.dot(a.astype(jnp.bfloat16), b, preferred_element_type=jnp.float32)


def _inproj_kernel(x_ref, head_ref, gmix_ref, win_ref, qlg_ref, wuq_ref, wuqs_ref, kvlg_ref,
                   wuk_ref, wuvt_ref, qg_ref, qgs_ref, kg_ref, kgs_ref, ones_ref, c_ref, s_ref,
                   xlru_ref, glru_ref, q_ref, k_ref, vt_ref, *, tm):
    i = pl.program_id(1)
    h = jnp.where(i == 0, head_ref[...], x_ref[...])
    hn = _rms(h, gmix_ref[...])
    proj = _bdot(hn, win_ref[...])
    for c in range(LRU_WIDTH // LANE):
        xlru_ref[c] = proj[:, c * LANE:(c + 1) * LANE]
    glru_ref[...] = proj[:, LRU_WIDTH:2 * LRU_WIDTH]
    o = 2 * LRU_WIDTH
    cq = proj[:, o:o + Q_LORA]
    ckv = proj[:, o + Q_LORA:o + Q_LORA + KV_LORA]
    o += Q_LORA + KV_LORA
    kpe = proj[:, o:o + HP]
    kpe_sw = proj[:, o + HP:]
    cqn = _rms(cq, qlg_ref[...]).astype(jnp.bfloat16)
    qraw = jnp.dot(cqn, wuq_ref[...], preferred_element_type=jnp.float32)
    qraw_sw = jnp.dot(cqn, wuqs_ref[...], preferred_element_type=jnp.float32)
    ckvn = _rms(ckv, kvlg_ref[...]).astype(jnp.bfloat16)
    kraw = jnp.dot(ckvn, wuk_ref[...], preferred_element_type=jnp.float32)
    vt = lax.dot_general(wuvt_ref[...], ckvn, (((1,), (1,)), ((), ())),
                         preferred_element_type=jnp.float32)
    vrow = lax.broadcasted_iota(jnp.int32, vt.shape, 0) % (2 * HP)
    vt_ref[...] = jnp.where((vrow == V_DIM) | (vrow == HP), 1.0, vt).astype(jnp.bfloat16)

    lane = lax.broadcasted_iota(jnp.int32, (tm, HP), 1)
    row = lax.broadcasted_iota(jnp.int32, (tm, HP), 0) + i * tm
    key_mask = jnp.where(row >= PAD, 0.0, NEG_INF)
    scale = math.log2(math.e) / math.sqrt(QK_DIM)
    cosv, sinv = c_ref[...], s_ref[...]
    q_cos = cosv * (qg_ref[...] * scale)
    q_sin = sinv * (qgs_ref[...] * scale)
    k_cos = cosv * kg_ref[...]
    k_sin_term = kpe_sw * (sinv * kgs_ref[...])

    def inv_rms(xb):
        x2 = xb * xb
        hi = x2.astype(jnp.bfloat16)
        lo = (x2 - hi.astype(jnp.float32)).astype(jnp.bfloat16)
        ss = jnp.dot(jnp.concatenate([hi, lo], axis=1), ones_ref[...],
                     preferred_element_type=jnp.float32)
        return lax.rsqrt(ss * (1.0 / QK_DIM) + EPS)

    for hd in range(MLA_HEADS):
        sl = slice(hd * HP, (hd + 1) * HP)
        xq = qraw[:, sl]
        qb = (xq * q_cos + qraw_sw[:, sl] * q_sin) * inv_rms(xq)
        q_ref[:, sl] = jnp.where(lane == MASK_LANE, 1.0, qb).astype(jnp.bfloat16)
        xk = kraw[:, sl] + kpe
        kb = (xk * k_cos + k_sin_term) * inv_rms(xk)
        k_ref[:, sl] = jnp.where(lane == MASK_LANE, key_mask, kb).astype(jnp.bfloat16)


def _x_tile_spec(tm, D):
    return pl.BlockSpec((None, tm, D), lambda b, i: (b, jnp.maximum(i - 1, 0), 0))


def _inproj(x, head, gmix, win, qlg, wuq, wuqs, kvlg, wuk, wuv, qg, qgs, kg, kgs, ones, ctab,
            stab, *, tm):
    B, S, D = x.shape
    assert tm == X0
    Tp = X0 + S
    nt = Tp // tm
    row = lambda w: pl.BlockSpec((None, tm, w), lambda b, i: (b, i, 0))
    full = lambda a: pl.BlockSpec(a.shape, lambda b, i: (0,) * a.ndim)
    tab = pl.BlockSpec((tm, HP), lambda b, i: (i, 0))
    f32, bf16 = jnp.float32, jnp.bfloat16
    return pl.pallas_call(
        functools.partial(_inproj_kernel, tm=tm),
        grid=(B, nt),
        in_specs=[_x_tile_spec(tm, D), full(head), full(gmix), full(win), full(qlg), full(wuq),
                  full(wuqs), full(kvlg), full(wuk), full(wuv), full(qg), full(qgs), full(kg),
                  full(kgs), full(ones), tab, tab],
        out_specs=[pl.BlockSpec((None, LRU_WIDTH // LANE, tm, LANE), lambda b, i: (b, 0, i, 0)),
                   row(LRU_WIDTH), row(MLA_HEADS * HP), row(MLA_HEADS * HP),
                   pl.BlockSpec((None, MLA_HEADS * HP, tm), lambda b, i: (b, 0, i))],
        out_shape=[jax.ShapeDtypeStruct((B, LRU_WIDTH // LANE, Tp, LANE), f32),
                   jax.ShapeDtypeStruct((B, Tp, LRU_WIDTH), f32),
                   jax.ShapeDtypeStruct((B, Tp, MLA_HEADS * HP), bf16),
                   jax.ShapeDtypeStruct((B, Tp, MLA_HEADS * HP), bf16),
                   jax.ShapeDtypeStruct((B, MLA_HEADS * HP, Tp), bf16)],
        compiler_params=pltpu.CompilerParams(
            dimension_semantics=("arbitrary", "arbitrary"), vmem_limit_bytes=VMEM_LIMIT),
        name="inproj",
    )(x, head, gmix, win, qlg, wuq, wuqs, kvlg, wuk, wuv, qg, qgs, kg, kgs, ones, ctab, stab)


NSLAB = LRU_WIDTH // LANE


def _lru_gates(x_ref, xprev_ref, xnext_ref, cw_ref, cb_ref, wg_ref, ba_ref, bx_ref, lam_ref,
               xw_ref, *, tt, first, last, t0):
    W = LRU_WIDTH
    seg = tt // SUB
    xw_ref[:, 0:SUB, :] = jnp.where(first, 0.0, xprev_ref[...])
    xw_ref[:, SUB:SUB + tt, :] = x_ref[...]
    xw_ref[:, SUB + tt:, :] = jnp.where(last, 0.0, xnext_ref[...])
    left = LRU_CONV // 2
    slabs = []
    for c in range(NSLAB):
        cs = slice(c * LANE, (c + 1) * LANE)
        pieces = []
        for j in range(seg):
            acc = cb_ref[:, cs]
            for k in range(LRU_CONV):
                tap = xw_ref[c, pl.ds(SUB - left + j + k, SUB, stride=seg), :]
                acc = acc + tap * cw_ref[k:k + 1, cs]
            pieces.append(acc)
        slabs.append(jnp.concatenate(pieces, axis=0))
    xc = jnp.concatenate(slabs, axis=1)
    xcb = xc.astype(jnp.bfloat16)
    half = W // 2
    g0 = jnp.dot(xcb[:, :half], wg_ref[0], preferred_element_type=jnp.float32)
    g1 = jnp.dot(xcb[:, half:], wg_ref[1], preferred_element_type=jnp.float32)
    ra = jnp.concatenate([g0[:, :half], g1[:, :half]], axis=1) + ba_ref[...]
    ri = jnp.concatenate([g0[:, half:], g1[:, half:]], axis=1) + bx_ref[...]
    r = jax.nn.sigmoid(ra)
    ig = jax.nn.sigmoid(ri)
    lam = lam_ref[...]
    sp = jnp.maximum(-lam, 0.0) + jnp.log(1.0 + jnp.exp(-jnp.abs(lam)))
    log_a = -LRU_C * r * sp
    a = jnp.exp(log_a)
    u = jnp.sqrt(1.0 - a * a) * (ig * xc)
    row = lax.broadcasted_iota(jnp.int32, (tt, W), 0)
    time = t0 + (row % SUB) * seg + row // SUB
    return a, jnp.where(time >= PAD, u, 0.0)


def _scan_block(a, u, h_ref, carry_ref, *, tt, reverse):
    W = LRU_WIDTH
    seg = tt // SUB
    order = range(seg - 1, -1, -1) if reverse else range(seg)
    h = jnp.zeros((SUB, W), jnp.float32)
    p = jnp.ones((SUB, W), jnp.float32)
    hs, ps = {}, {}
    for j in order:
        aj = a[j * SUB:(j + 1) * SUB]
        h = aj * h + u[j * SUB:(j + 1) * SUB]
        p = aj * p
        hs[j], ps[j] = h, p
    sub = lax.broadcasted_iota(jnp.int32, (SUB, W), 0)
    e, pf = h, p
    for d in (1, 2, 4):
        if reverse:
            keep, sh = sub < SUB - d, SUB - d
        else:
            keep, sh = sub >= d, d
        e_sh = jnp.where(keep, pltpu.roll(e, sh, axis=0), 0.0)
        pf_sh = jnp.where(keep, pltpu.roll(pf, sh, axis=0), 1.0)
        e = e + pf * e_sh
        pf = pf * pf_sh
    carry = carry_ref[...]
    leaving = e + pf * carry
    if reverse:
        entering = jnp.where(sub < SUB - 1, pltpu.roll(leaving, SUB - 1, axis=0), carry)
        carry_ref[...] = leaving[0:1]
    else:
        entering = jnp.where(sub >= 1, pltpu.roll(leaving, 1, axis=0), carry)
        carry_ref[...] = leaving[SUB - 1:SUB]
    for j in range(seg):
        hj = hs[j] + ps[j] * entering
        for c in range(NSLAB):
            h_ref[c, pl.ds(j, SUB, stride=seg), :] = hj[:, c * LANE:(c + 1) * LANE]


def _lru_fwd_kernel(x_ref, xprev_ref, xnext_ref, cw_ref, cb_ref, wg_ref, ba_ref, bx_ref, lam_ref,
                    hf_ref, xw_ref, carry_ref, *, tt, nt):
    i = pl.program_id(1)

    @pl.when(i == 0)
    def _():
        carry_ref[...] = jnp.zeros_like(carry_ref)

    a, u = _lru_gates(x_ref, xprev_ref, xnext_ref, cw_ref, cb_ref, wg_ref, ba_ref, bx_ref,
                      lam_ref, xw_ref, tt=tt, first=i == 0, last=i == nt - 1, t0=i * tt)
    _scan_block(a, u, hf_ref, carry_ref, tt=tt, reverse=False)


def _lru_bwd_kernel(x_ref, xprev_ref, xnext_ref, cw_ref, cb_ref, wg_ref, ba_ref, bx_ref, lam_ref,
                    hf_ref, gl_ref, gg_ref, y_ref, xw_ref, hb_ref, carry_ref, *, tt, nt):
    j = pl.program_id(1)
    i = nt - 1 - j

    @pl.when(j == 0)
    def _():
        carry_ref[...] = jnp.zeros_like(carry_ref)

    a, u = _lru_gates(x_ref, xprev_ref, xnext_ref, cw_ref, cb_ref, wg_ref, ba_ref, bx_ref,
                      lam_ref, xw_ref, tt=tt, first=i == 0, last=i == nt - 1, t0=i * tt)
    _scan_block(a, u, hb_ref, carry_ref, tt=tt, reverse=True)
    ys = [(hf_ref[c] + hb_ref[c]) * jax.nn.gelu(gl_ref[:, c * LANE:(c + 1) * LANE],
                                                approximate=True) for c in range(NSLAB)]
    y = jnp.concatenate(ys, axis=1)
    y_ref[...] = _rms(y, gg_ref[...]).astype(y_ref.dtype)


def _lru_specs(tt, nt, Tp, tmap):
    nsub = Tp // SUB
    per = tt // SUB
    slab = lambda rows, f: pl.BlockSpec((None, NSLAB, rows, LANE), lambda b, j: (b, 0, f(j), 0))
    blk = slab(tt, tmap)
    prev = slab(SUB, lambda j: jnp.maximum(tmap(j) * per - 1, 0))
    nxt = slab(SUB, lambda j: jnp.minimum((tmap(j) + 1) * per, nsub - 1))
    return blk, prev, nxt


def _lru_fwd(xl, cw, cb, wg, ba, bx, lam, *, tt):
    B, _, Tp, _ = xl.shape
    nt = Tp // tt
    blk, prev, nxt = _lru_specs(tt, nt, Tp, lambda j: j)
    full = lambda a: pl.BlockSpec(a.shape, lambda b, j: (0,) * a.ndim)
    return pl.pallas_call(
        functools.partial(_lru_fwd_kernel, tt=tt, nt=nt),
        grid=(B, nt),
        in_specs=[blk, prev, nxt, full(cw), full(cb), full(wg), full(ba), full(bx), full(lam)],
        out_specs=blk,
        out_shape=jax.ShapeDtypeStruct((B, NSLAB, Tp, LANE), jnp.float32),
        scratch_shapes=[pltpu.VMEM((NSLAB, tt + 2 * SUB, LANE), jnp.float32),
                        pltpu.VMEM((1, LRU_WIDTH), jnp.float32)],
        compiler_params=pltpu.CompilerParams(
            dimension_semantics=("arbitrary", "arbitrary"), vmem_limit_bytes=VMEM_LIMIT),
        name="lru_fwd",
    )(xl, xl, xl, cw, cb, wg, ba, bx, lam)


def _lru_bwd(xl, cw, cb, wg, ba, bx, lam, hf, gl, gg, *, tt):
    B, _, Tp, _ = xl.shape
    W = LRU_WIDTH
    nt = Tp // tt
    rev = lambda j: nt - 1 - j
    blk, prev, nxt = _lru_specs(tt, nt, Tp, rev)
    row = pl.BlockSpec((None, tt, W), lambda b, j: (b, rev(j), 0))
    full = lambda a: pl.BlockSpec(a.shape, lambda b, j: (0,) * a.ndim)
    return pl.pallas_call(
        functools.partial(_lru_bwd_kernel, tt=tt, nt=nt),
        grid=(B, nt),
        in_specs=[blk, prev, nxt, full(cw), full(cb), full(wg), full(ba), full(bx), full(lam),
                  blk, row, full(gg)],
        out_specs=row,
        out_shape=jax.ShapeDtypeStruct((B, Tp, W), jnp.bfloat16),
        scratch_shapes=[pltpu.VMEM((NSLAB, tt + 2 * SUB, LANE), jnp.float32),
                        pltpu.VMEM((NSLAB, tt, LANE), jnp.float32),
                        pltpu.VMEM((1, LRU_WIDTH), jnp.float32)],
        compiler_params=pltpu.CompilerParams(
            dimension_semantics=("arbitrary", "arbitrary"), vmem_limit_bytes=VMEM_LIMIT),
        name="lru_bwd",
    )(xl, xl, xl, cw, cb, wg, ba, bx, lam, hf, gl, gg)


def _max_over_rows(s):
    rows = s.shape[0]
    while rows % (2 * SUB) == 0 and rows > 32 * SUB:
        fold = 4 if rows % (4 * SUB) == 0 else 2
        rows //= fold
        s = jnp.max(s.reshape(fold, rows, s.shape[1]), axis=0)
    return jnp.max(s, axis=0, keepdims=True)


def _attn_kernel(q_ref, k_ref, vt_ref, o_ref, *, tq, lead, span):
    heads = [slice(hh * HP, (hh + 1) * HP) for hh in range(2)]
    nkeys = k_ref.shape[0]
    nt = (((1,), (1,)), ((), ()))
    rows = lax.broadcasted_iota(jnp.int32, (HP, tq), 0)

    def scores(hh, lo, hi):
        return lax.dot_general(k_ref[lo:hi, heads[hh]], q_ref[:, heads[hh]], nt,
                               preferred_element_type=jnp.float32)

    def pv(hh, lo, hi, s, m):
        return jnp.dot(vt_ref[heads[hh], lo:hi], jnp.exp2(s - m).astype(jnp.bfloat16),
                       preferred_element_type=jnp.float32)

    def finish(a0, a1):
        l0, l1 = a0[V_DIM:V_DIM + 1, :], a1[0:1, :]
        return jnp.where(rows < V_DIM, a0 / l0, a1 / l1).T, l0, l1

    accs = []
    for hh in range(2):
        s = scores(hh, 0, lead)
        m = _max_over_rows(s)
        acc = pv(hh, 0, lead, s, m)
        for lo in range(lead, nkeys, span):
            acc = acc + pv(hh, lo, lo + span, scores(hh, lo, lo + span), m)
        accs.append(acc)
    out, l0, l1 = finish(*accs)
    lmin = jnp.min(jnp.minimum(l0, l1))
    lmax = jnp.max(jnp.maximum(l0, l1))
    safe = (lmin > 2.0 ** -100) & (lmax < 2.0 ** 100)

    @pl.when(safe)
    def _():
        o_ref[...] = out

    @pl.when(jnp.logical_not(safe))
    def _():
        exact = []
        for hh in range(2):
            s = scores(hh, 0, nkeys)
            exact.append(pv(hh, 0, nkeys, s, _max_over_rows(s)))
        o_ref[...] = finish(*exact)[0]


def _attn(q, k, vt, *, tq, lead, span):
    B, Tp, _ = q.shape
    npair = MLA_HEADS // 2
    return pl.pallas_call(
        functools.partial(_attn_kernel, tq=tq, lead=lead, span=span),
        grid=(B, npair, Tp // tq),
        in_specs=[pl.BlockSpec((None, tq, 2 * HP), lambda b, p, i: (b, i, p)),
                  pl.BlockSpec((None, Tp, 2 * HP), lambda b, p, i: (b, 0, p)),
                  pl.BlockSpec((None, 2 * HP, Tp), lambda b, p, i: (b, p, 0))],
        out_specs=pl.BlockSpec((None, tq, HP), lambda b, p, i: (b, i, p)),
        out_shape=jax.ShapeDtypeStruct((B, Tp, npair * HP), jnp.float32),
        compiler_params=pltpu.CompilerParams(
            dimension_semantics=("arbitrary", "arbitrary", "arbitrary"),
            vmem_limit_bytes=VMEM_LIMIT),
        name="attn",
    )(q, k, vt)


def _mix_ffn_kernel(x_ref, xprev_ref, xnext_ref, mtail_ref, yl_ref, ylprev_ref, ylnext_ref,
                    ym_ref, ymprev_ref, ymnext_ref, mg_ref, wo_ref, g_ref, wg_ref, wu_ref,
                    cw_ref, cb_ref, wd_ref, o_ref, gate_ref, *, tm, nt):
    i = pl.program_id(1)
    f32, bf16 = jnp.float32, jnp.bfloat16
    H = 2 * SUB
    xw = jnp.concatenate([jnp.where(i == 0, mtail_ref[...], xprev_ref[...]), x_ref[...],
                          xnext_ref[...]], axis=0)
    ylw = jnp.concatenate([ylprev_ref[...].astype(f32)[H - SUB:], yl_ref[...].astype(f32),
                           ylnext_ref[...].astype(f32)[:SUB]], axis=0).astype(bf16)
    ymw = jnp.concatenate([ymprev_ref[...], ym_ref[...], ymnext_ref[...]], axis=0)
    y = jnp.concatenate([ylw, _rms(ymw, mg_ref[...]).astype(bf16)], axis=1)
    h1 = xw + jnp.dot(y, wo_ref[...], preferred_element_type=f32)
    r = lax.broadcasted_iota(jnp.int32, h1.shape, 0)
    h1 = jnp.where((i == nt - 1) & (r >= tm + SUB), 0.0, h1)
    hn = _rms(h1, g_ref[...]).astype(bf16)
    gate_ref[...] = jnp.dot(hn, wg_ref[...], preferred_element_type=f32)
    up = jnp.dot(hn[SUB:SUB + tm], wu_ref[...], preferred_element_type=f32)
    left = FFN_CONV // 2
    gc = cb_ref[...] + gate_ref[SUB - left:SUB - left + tm, :] * cw_ref[0:1, :]
    for k in range(1, FFN_CONV):
        gc = gc + gate_ref[SUB - left + k:SUB - left + k + tm, :] * cw_ref[k:k + 1, :]
    act = (gc * jax.nn.sigmoid(gc) * up).astype(bf16)
    o_ref[...] = h1[SUB:SUB + tm] + jnp.dot(act, wd_ref[...], preferred_element_type=f32)


def _mix_ffn(x, mtail, yl, ym, mg, wo, g, wg, wu, cw, cb, wd, *, tm):
    B, S, D = x.shape
    Tp = yl.shape[1]
    W = yl.shape[2]
    nt = S // tm
    off = X0 // tm
    per = tm // SUB
    H = 2 * SUB
    perh = tm // H
    full = lambda a: pl.BlockSpec(a.shape, lambda b, i: (0,) * a.ndim)

    def halo(w, rows, per_tile, nblk, shift):
        prev = pl.BlockSpec((None, rows, w),
                            lambda b, i: (b, jnp.maximum((i + shift) * per_tile - 1, 0), 0))
        nxt = pl.BlockSpec((None, rows, w),
                           lambda b, i: (b, jnp.minimum((i + shift + 1) * per_tile, nblk - 1), 0))
        return prev, nxt

    xprev, xnext = halo(D, SUB, per, S // SUB, 0)
    ylprev, ylnext = halo(W, H, perh, Tp // H, off)
    ymprev, ymnext = halo(W, SUB, per, Tp // SUB, off)
    return pl.pallas_call(
        functools.partial(_mix_ffn_kernel, tm=tm, nt=nt),
        grid=(B, nt),
        in_specs=[pl.BlockSpec((None, tm, D), lambda b, i: (b, i, 0)), xprev, xnext, full(mtail),
                  pl.BlockSpec((None, tm, W), lambda b, i: (b, i + off, 0)), ylprev, ylnext,
                  pl.BlockSpec((None, tm, W), lambda b, i: (b, i + off, 0)), ymprev, ymnext,
                  full(mg), full(wo), full(g), full(wg), full(wu), full(cw), full(cb), full(wd)],
        out_specs=pl.BlockSpec((None, tm, D), lambda b, i: (b, i, 0)),
        out_shape=jax.ShapeDtypeStruct((B, S, D), jnp.float32),
        scratch_shapes=[pltpu.VMEM((tm + 2 * SUB, D_FF), jnp.float32)],
        compiler_params=pltpu.CompilerParams(
            dimension_semantics=("arbitrary", "arbitrary"), vmem_limit_bytes=VMEM_LIMIT),
        name="mix_ffn",
    )(x, x, x, mtail, yl, yl, yl, ym, ym, ym, mg, wo, g, wg, wu, cw, cb, wd)


def _pad_heads(w, width, offset=0):
    K = w.shape[0]
    w = w.reshape(K, MLA_HEADS, width)
    w = jnp.pad(w, ((0, 0), (0, 0), (offset, HP - width - offset)))
    return w.reshape(K, MLA_HEADS * HP)


def _block_diag(w):
    H, Dh, _ = w.shape
    eye = jnp.eye(H, dtype=w.dtype)
    return (eye[:, None, :, None] * w[:, :, None, :]).reshape(H * Dh, H * Dh)


def _gate_weights(w_a, w_x):
    half = LRU_WIDTH // 2
    wa = _block_diag(w_a)
    wx = _block_diag(w_x)
    halves = [jnp.concatenate([wa[c * half:(c + 1) * half, c * half:(c + 1) * half],
                               wx[c * half:(c + 1) * half, c * half:(c + 1) * half]], axis=1)
              for c in range(2)]
    return jnp.stack(halves).astype(jnp.bfloat16)


def kernel(x, meta_tokens, norm_mix_g, w_in, conv_lru_w, conv_lru_b, lru_w_a, lru_b_a, lru_w_x,
           lru_b_x, lru_lambda, lru_gate_g, q_latent_g, w_uq, kv_latent_g, w_ukv, q_norm_g,
           k_norm_g, mla_out_g, w_out, norm_ffn_g, w_ffn_up, conv_ffn_w, conv_ffn_b, w_ffn_down):
    B, S, D = x.shape
    f32, bf16 = jnp.float32, jnp.bfloat16
    Tp = X0 + S
    l = 0
    r2 = lambda a: a.reshape(1, -1).astype(f32)

    head = jnp.concatenate([jnp.zeros((PAD, D), x.dtype), meta_tokens.astype(x.dtype)], axis=0)

    half = QK_ROPE // 2
    lanes = jnp.arange(HP)
    swap = jnp.where((lanes >= QK_NOPE) & (lanes < QK_NOPE + half), lanes + half,
                     jnp.where((lanes >= QK_NOPE + half) & (lanes < QK_DIM), lanes - half, lanes))
    swap_heads = (jnp.arange(MLA_HEADS)[:, None] * HP + swap[None, :]).reshape(-1)

    o = 2 * LRU_WIDTH + Q_LORA + KV_LORA
    kpe_w = jnp.pad(w_in[l][:, o:], ((0, 0), (QK_NOPE, LANE - QK_DIM)))
    win = jnp.concatenate([w_in[l][:, :o], kpe_w, kpe_w[:, swap]], axis=1).astype(bf16)
    wuq = _pad_heads(w_uq[l], QK_DIM).astype(bf16)
    wuqs = wuq[:, swap_heads]
    wkv = w_ukv[l].reshape(KV_LORA, MLA_HEADS, QK_NOPE + V_DIM)
    wuk = _pad_heads(wkv[:, :, :QK_NOPE].reshape(KV_LORA, -1), QK_NOPE).astype(bf16)
    wv = jnp.pad(wkv[:, :, QK_NOPE:], ((0, 0), (0, 0), (0, HP - V_DIM)))
    wv = jnp.where((jnp.arange(MLA_HEADS) % 2 == 1)[None, :, None],
                   jnp.roll(wv, V_DIM, axis=2), wv)
    wuvt = wv.reshape(KV_LORA, MLA_HEADS * HP).T.astype(bf16)
    qg = jnp.pad(q_norm_g[l], (0, HP - QK_DIM)).reshape(1, HP)
    kg = jnp.pad(k_norm_g[l], (0, HP - QK_DIM)).reshape(1, HP)
    ones = jnp.ones((2 * HP, HP), bf16)

    pos = jnp.arange(Tp, dtype=f32) - PAD
    inv_freq = ROPE_THETA ** (-jnp.arange(0, QK_ROPE, 2, dtype=f32) / QK_ROPE)
    ang = pos[:, None] * inv_freq[None, :]
    cos, sin = jnp.cos(ang), jnp.sin(ang)
    ctab = jnp.concatenate([jnp.ones((Tp, QK_NOPE), f32), cos, cos,
                            jnp.ones((Tp, HP - QK_DIM), f32)], axis=1)
    stab = jnp.concatenate([jnp.zeros((Tp, QK_NOPE), f32), -sin, sin,
                            jnp.zeros((Tp, HP - QK_DIM), f32)], axis=1)

    xl, gl, q, k, vt = _inproj(x, head, r2(norm_mix_g[l]), win, r2(q_latent_g[l]), wuq, wuqs,
                              r2(kv_latent_g[l]), wuk, wuvt, qg, qg[:, swap], kg, kg[:, swap],
                              ones, ctab, stab, tm=X0)

    cw, cb = conv_lru_w[l], r2(conv_lru_b[l])
    wg = [_gate_weights(lru_w_a[l, d], lru_w_x[l, d]) for d in range(2)]
    hf = _lru_fwd(xl, cw, cb, wg[0], r2(lru_b_a[l, 0]), r2(lru_b_x[l, 0]), r2(lru_lambda[l, 0]),
                  tt=264)
    yl = _lru_bwd(xl, cw, cb, wg[1], r2(lru_b_a[l, 1]), r2(lru_b_x[l, 1]), r2(lru_lambda[l, 1]),
                  hf, gl, r2(lru_gate_g[l]), tt=264)

    ym = _attn(q, k, vt, tq=768, lead=768, span=768)

    wup = w_ffn_up[l].astype(bf16)
    return _mix_ffn(x, head[X0 - SUB:], yl, ym, r2(mla_out_g[l]), w_out[l].astype(bf16),
                    r2(norm_ffn_g[l]), wup[:, :D_FF], wup[:, D_FF:], conv_ffn_w[l],
                    r2(conv_ffn_b[l]), w_ffn_down[l].astype(bf16), tm=256)
```

```python
import functools
import math

import jax
import jax.numpy as jnp
from jax import lax
from jax.experimental import pallas as pl
from jax.experimental.pallas import tpu as pltpu

D_MODEL = 1024
N_META = 16
LRU_WIDTH = 512
LRU_HEADS = 8
LRU_HEAD_DIM = 64
LRU_CONV = 4
LRU_C = 8.0
MLA_HEADS = 8
QK_NOPE = 64
QK_ROPE = 32
QK_DIM = 96
V_DIM = 64
Q_LORA = 256
KV_LORA = 128
ROPE_THETA = 10000.0
D_FF = 2816
FFN_CONV = 3
EPS = 1e-6
NEG_INF = -1e30

LANE = 128
SUB = 8
X0 = 256
PAD = X0 - N_META
HP = LANE
MASK_LANE = QK_DIM
VROWS = V_DIM + 2 * SUB
VMEM_LIMIT = 56 * 1024 * 1024


def _rms(x, g):
    return x * lax.rsqrt(jnp.mean(x * x, axis=-1, keepdims=True) + EPS) * g


def _bdot(a, b):
    return jnp.dot(a.astype(jnp.bfloat16), b, preferred_element_type=jnp.float32)


def _inproj_kernel(x_ref, head_ref, gmix_ref, win_ref, qlg_ref, wuq_ref, wuqs_ref, kvlg_ref,
                   wuk_ref, wuvt_ref, qg_ref, qgs_ref, kg_ref, kgs_ref, ones_ref, c_ref, s_ref,
                   xlru_ref, glru_ref, q_ref, k_ref, vt_ref, *, tm):
    i = pl.program_id(1)
    h = jnp.where(i == 0, head_ref[...], x_ref[...])
    hn = _rms(h, gmix_ref[...])
    proj = _bdot(hn, win_ref[...])
    for c in range(LRU_WIDTH // LANE):
        xlru_ref[c] = proj[:, c * LANE:(c + 1) * LANE]
    glru_ref[...] = proj[:, LRU_WIDTH:2 * LRU_WIDTH]
    o = 2 * LRU_WIDTH
    cq = proj[:, o:o + Q_LORA]
    ckv = proj[:, o + Q_LORA:o + Q_LORA + KV_LORA]
    o += Q_LORA + KV_LORA
    kpe = proj[:, o:o + HP]
    kpe_sw = proj[:, o + HP:]
    cqn = _rms(cq, qlg_ref[...]).astype(jnp.bfloat16)
    qraw = jnp.dot(cqn, wuq_ref[...], preferred_element_type=jnp.float32)
    qraw_sw = jnp.dot(cqn, wuqs_ref[...], preferred_element_type=jnp.float32)
    ckvn = _rms(ckv, kvlg_ref[...]).astype(jnp.bfloat16)
    kraw = jnp.dot(ckvn, wuk_ref[...], preferred_element_type=jnp.float32)
    vt = lax.dot_general(wuvt_ref[...], ckvn, (((1,), (1,)), ((), ())),
                         preferred_element_type=jnp.float32)
    vrow = lax.broadcasted_iota(jnp.int32, vt.shape, 0) % VROWS
    vt_ref[...] = jnp.where(vrow == V_DIM, 1.0, vt).astype(jnp.bfloat16)

    lane = lax.broadcasted_iota(jnp.int32, (tm, HP), 1)
    row = lax.broadcasted_iota(jnp.int32, (tm, HP), 0) + i * tm
    key_mask = jnp.where(row >= PAD, 0.0, NEG_INF)
    scale = math.log2(math.e) / math.sqrt(QK_DIM)
    cosv, sinv = c_ref[...], s_ref[...]
    q_cos = cosv * (qg_ref[...] * scale)
    q_sin = sinv * (qgs_ref[...] * scale)
    k_cos = cosv * kg_ref[...]
    k_sin_term = kpe_sw * (sinv * kgs_ref[...])

    def inv_rms(xb):
        x2 = xb * xb
        hi = x2.astype(jnp.bfloat16)
        lo = (x2 - hi.astype(jnp.float32)).astype(jnp.bfloat16)
        ss = jnp.dot(jnp.concatenate([hi, lo], axis=1), ones_ref[...],
                     preferred_element_type=jnp.float32)
        return lax.rsqrt(ss * (1.0 / QK_DIM) + EPS)

    for hd in range(MLA_HEADS):
        sl = slice(hd * HP, (hd + 1) * HP)
        xq = qraw[:, sl]
        qb = (xq * q_cos + qraw_sw[:, sl] * q_sin) * inv_rms(xq)
        q_ref[:, sl] = jnp.where(lane == MASK_LANE, 1.0, qb).astype(jnp.bfloat16)
        xk = kraw[:, sl] + kpe
        kb = (xk * k_cos + k_sin_term) * inv_rms(xk)
        k_ref[:, sl] = jnp.where(lane == MASK_LANE, key_mask, kb).astype(jnp.bfloat16)


def _x_tile_spec(tm, D):
    return pl.BlockSpec((None, tm, D), lambda b, i: (b, jnp.maximum(i - 1, 0), 0))


def _inproj(x, head, gmix, win, qlg, wuq, wuqs, kvlg, wuk, wuv, qg, qgs, kg, kgs, ones, ctab,
            stab, *, tm):
    B, S, D = x.shape
    assert tm == X0
    Tp = X0 + S
    nt = Tp // tm
    row = lambda w: pl.BlockSpec((None, tm, w), lambda b, i: (b, i, 0))
    full = lambda a: pl.BlockSpec(a.shape, lambda b, i: (0,) * a.ndim)
    tab = pl.BlockSpec((tm, HP), lambda b, i: (i, 0))
    f32, bf16 = jnp.float32, jnp.bfloat16
    return pl.pallas_call(
        functools.partial(_inproj_kernel, tm=tm),
        grid=(B, nt),
        in_specs=[_x_tile_spec(tm, D), full(head), full(gmix), full(win), full(qlg), full(wuq),
                  full(wuqs), full(kvlg), full(wuk), full(wuv), full(qg), full(qgs), full(kg),
                  full(kgs), full(ones), tab, tab],
        out_specs=[pl.BlockSpec((None, LRU_WIDTH // LANE, tm, LANE), lambda b, i: (b, 0, i, 0)),
                   row(LRU_WIDTH), row(MLA_HEADS * HP), row(MLA_HEADS * HP),
                   pl.BlockSpec((None, MLA_HEADS * VROWS, tm), lambda b, i: (b, 0, i))],
        out_shape=[jax.ShapeDtypeStruct((B, LRU_WIDTH // LANE, Tp, LANE), f32),
                   jax.ShapeDtypeStruct((B, Tp, LRU_WIDTH), f32),
                   jax.ShapeDtypeStruct((B, Tp, MLA_HEADS * HP), bf16),
                   jax.ShapeDtypeStruct((B, Tp, MLA_HEADS * HP), bf16),
                   jax.ShapeDtypeStruct((B, MLA_HEADS * VROWS, Tp), bf16)],
        compiler_params=pltpu.CompilerParams(
            dimension_semantics=("arbitrary", "arbitrary"), vmem_limit_bytes=VMEM_LIMIT),
        name="inproj",
    )(x, head, gmix, win, qlg, wuq, wuqs, kvlg, wuk, wuv, qg, qgs, kg, kgs, ones, ctab, stab)


NSLAB = LRU_WIDTH // LANE


def _lru_gates(x_ref, xprev_ref, xnext_ref, cw_ref, cb_ref, wg_ref, ba_ref, bx_ref, lam_ref,
               xw_ref, *, tt, first, last, t0):
    W = LRU_WIDTH
    seg = tt // SUB
    xw_ref[:, 0:SUB, :] = jnp.where(first, 0.0, xprev_ref[...])
    xw_ref[:, SUB:SUB + tt, :] = x_ref[...]
    xw_ref[:, SUB + tt:, :] = jnp.where(last, 0.0, xnext_ref[...])
    left = LRU_CONV // 2
    slabs = []
    for c in range(NSLAB):
        cs = slice(c * LANE, (c + 1) * LANE)
        pieces = []
        for j in range(seg):
            acc = cb_ref[:, cs]
            for k in range(LRU_CONV):
                tap = xw_ref[c, pl.ds(SUB - left + j + k, SUB, stride=seg), :]
                acc = acc + tap * cw_ref[k:k + 1, cs]
            pieces.append(acc)
        slabs.append(jnp.concatenate(pieces, axis=0))
    xc = jnp.concatenate(slabs, axis=1)
    xcb = xc.astype(jnp.bfloat16)
    half = W // 2
    g0 = jnp.dot(xcb[:, :half], wg_ref[0], preferred_element_type=jnp.float32)
    g1 = jnp.dot(xcb[:, half:], wg_ref[1], preferred_element_type=jnp.float32)
    ra = jnp.concatenate([g0[:, :half], g1[:, :half]], axis=1) + ba_ref[...]
    ri = jnp.concatenate([g0[:, half:], g1[:, half:]], axis=1) + bx_ref[...]
    r = jax.nn.sigmoid(ra)
    ig = jax.nn.sigmoid(ri)
    lam = lam_ref[...]
    sp = jnp.maximum(-lam, 0.0) + jnp.log(1.0 + jnp.exp(-jnp.abs(lam)))
    log_a = -LRU_C * r * sp
    a = jnp.exp(log_a)
    u = jnp.sqrt(1.0 - a * a) * (ig * xc)
    row = lax.broadcasted_iota(jnp.int32, (tt, W), 0)
    time = t0 + (row % SUB) * seg + row // SUB
    return a, jnp.where(time >= PAD, u, 0.0)


def _scan_block(a, u, h_ref, carry_ref, *, tt, reverse):
    W = LRU_WIDTH
    seg = tt // SUB
    order = range(seg - 1, -1, -1) if reverse else range(seg)
    h = jnp.zeros((SUB, W), jnp.float32)
    p = jnp.ones((SUB, W), jnp.float32)
    hs, ps = {}, {}
    for j in order:
        aj = a[j * SUB:(j + 1) * SUB]
        h = aj * h + u[j * SUB:(j + 1) * SUB]
        p = aj * p
        hs[j], ps[j] = h, p
    sub = lax.broadcasted_iota(jnp.int32, (SUB, W), 0)
    e, pf = h, p
    for d in (1, 2, 4):
        if reverse:
            keep, sh = sub < SUB - d, SUB - d
        else:
            keep, sh = sub >= d, d
        e_sh = jnp.where(keep, pltpu.roll(e, sh, axis=0), 0.0)
        pf_sh = jnp.where(keep, pltpu.roll(pf, sh, axis=0), 1.0)
        e = e + pf * e_sh
        pf = pf * pf_sh
    carry = carry_ref[...]
    leaving = e + pf * carry
    if reverse:
        entering = jnp.where(sub < SUB - 1, pltpu.roll(leaving, SUB - 1, axis=0), carry)
        carry_ref[...] = leaving[0:1]
    else:
        entering = jnp.where(sub >= 1, pltpu.roll(leaving, 1, axis=0), carry)
        carry_ref[...] = leaving[SUB - 1:SUB]
    for j in range(seg):
        hj = hs[j] + ps[j] * entering
        for c in range(NSLAB):
            h_ref[c, pl.ds(j, SUB, stride=seg), :] = hj[:, c * LANE:(c + 1) * LANE]


def _lru_fwd_kernel(x_ref, xprev_ref, xnext_ref, cw_ref, cb_ref, wg_ref, ba_ref, bx_ref, lam_ref,
                    hf_ref, xw_ref, carry_ref, *, tt, nt):
    i = pl.program_id(1)

    @pl.when(i == 0)
    def _():
        carry_ref[...] = jnp.zeros_like(carry_ref)

    a, u = _lru_gates(x_ref, xprev_ref, xnext_ref, cw_ref, cb_ref, wg_ref, ba_ref, bx_ref,
                      lam_ref, xw_ref, tt=tt, first=i == 0, last=i == nt - 1, t0=i * tt)
    _scan_block(a, u, hf_ref, carry_ref, tt=tt, reverse=False)


def _lru_bwd_kernel(x_ref, xprev_ref, xnext_ref, cw_ref, cb_ref, wg_ref, ba_ref, bx_ref, lam_ref,
                    hf_ref, gl_ref, gg_ref, y_ref, xw_ref, hb_ref, carry_ref, *, tt, nt):
    j = pl.program_id(1)
    i = nt - 1 - j

    @pl.when(j == 0)
    def _():
        carry_ref[...] = jnp.zeros_like(carry_ref)

    a, u = _lru_gates(x_ref, xprev_ref, xnext_ref, cw_ref, cb_ref, wg_ref, ba_ref, bx_ref,
                      lam_ref, xw_ref, tt=tt, first=i == 0, last=i == nt - 1, t0=i * tt)
    _scan_block(a, u, hb_ref, carry_ref, tt=tt, reverse=True)
    ys = [(hf_ref[c] + hb_ref[c]) * jax.nn.gelu(gl_ref[:, c * LANE:(c + 1) * LANE],
                                                approximate=True) for c in range(NSLAB)]
    y = jnp.concatenate(ys, axis=1)
    y_ref[...] = _rms(y, gg_ref[...]).astype(y_ref.dtype)


def _lru_specs(tt, nt, Tp, tmap):
    nsub = Tp // SUB
    per = tt // SUB
    slab = lambda rows, f: pl.BlockSpec((None, NSLAB, rows, LANE), lambda b, j: (b, 0, f(j), 0))
    blk = slab(tt, tmap)
    prev = slab(SUB, lambda j: jnp.maximum(tmap(j) * per - 1, 0))
    nxt = slab(SUB, lambda j: jnp.minimum((tmap(j) + 1) * per, nsub - 1))
    return blk, prev, nxt


def _lru_fwd(xl, cw, cb, wg, ba, bx, lam, *, tt):
    B, _, Tp, _ = xl.shape
    nt = Tp // tt
    blk, prev, nxt = _lru_specs(tt, nt, Tp, lambda j: j)
    full = lambda a: pl.BlockSpec(a.shape, lambda b, j: (0,) * a.ndim)
    return pl.pallas_call(
        functools.partial(_lru_fwd_kernel, tt=tt, nt=nt),
        grid=(B, nt),
        in_specs=[blk, prev, nxt, full(cw), full(cb), full(wg), full(ba), full(bx), full(lam)],
        out_specs=blk,
        out_shape=jax.ShapeDtypeStruct((B, NSLAB, Tp, LANE), jnp.float32),
        scratch_shapes=[pltpu.VMEM((NSLAB, tt + 2 * SUB, LANE), jnp.float32),
                        pltpu.VMEM((1, LRU_WIDTH), jnp.float32)],
        compiler_params=pltpu.CompilerParams(
            dimension_semantics=("arbitrary", "arbitrary"), vmem_limit_bytes=VMEM_LIMIT),
        name="lru_fwd",
    )(xl, xl, xl, cw, cb, wg, ba, bx, lam)


def _lru_bwd(xl, cw, cb, wg, ba, bx, lam, hf, gl, gg, *, tt):
    B, _, Tp, _ = xl.shape
    W = LRU_WIDTH
    nt = Tp // tt
    rev = lambda j: nt - 1 - j
    blk, prev, nxt = _lru_specs(tt, nt, Tp, rev)
    row = pl.BlockSpec((None, tt, W), lambda b, j: (b, rev(j), 0))
    full = lambda a: pl.BlockSpec(a.shape, lambda b, j: (0,) * a.ndim)
    return pl.pallas_call(
        functools.partial(_lru_bwd_kernel, tt=tt, nt=nt),
        grid=(B, nt),
        in_specs=[blk, prev, nxt, full(cw), full(cb), full(wg), full(ba), full(bx), full(lam),
                  blk, row, full(gg)],
        out_specs=row,
        out_shape=jax.ShapeDtypeStruct((B, Tp, W), jnp.bfloat16),
        scratch_shapes=[pltpu.VMEM((NSLAB, tt + 2 * SUB, LANE), jnp.float32),
                        pltpu.VMEM((NSLAB, tt, LANE), jnp.float32),
                        pltpu.VMEM((1, LRU_WIDTH), jnp.float32)],
        compiler_params=pltpu.CompilerParams(
            dimension_semantics=("arbitrary", "arbitrary"), vmem_limit_bytes=VMEM_LIMIT),
        name="lru_bwd",
    )(xl, xl, xl, cw, cb, wg, ba, bx, lam, hf, gl, gg)


def _max_over_rows(s):
    rows = s.shape[0]
    while rows % (2 * SUB) == 0 and rows > 32 * SUB:
        fold = 4 if rows % (4 * SUB) == 0 else 2
        rows //= fold
        s = jnp.max(s.reshape(fold, rows, s.shape[1]), axis=0)
    return jnp.max(s, axis=0, keepdims=True)


def _attn_kernel(q_ref, k_ref, vt_ref, o_ref, *, tq, lead, span):
    heads = [slice(hh * HP, (hh + 1) * HP) for hh in range(2)]
    nkeys = k_ref.shape[0]
    nt = (((1,), (1,)), ((), ()))

    def scores(hh, lo, hi, qt):
        return lax.dot_general(k_ref[lo:hi, heads[hh]], qt[:, heads[hh]], nt,
                               preferred_element_type=jnp.float32)

    def pv(hh, lo, hi, s, m):
        return jnp.dot(vt_ref[hh * VROWS:(hh + 1) * VROWS, lo:hi],
                       jnp.exp2(s - m).astype(jnp.bfloat16),
                       preferred_element_type=jnp.float32)

    def finish(a0, a1):
        l0, l1 = a0[V_DIM:V_DIM + 1, :], a1[V_DIM:V_DIM + 1, :]
        return jnp.concatenate([a0[:V_DIM] / l0, a1[:V_DIM] / l1], axis=0).T, l0, l1

    qt = q_ref[...]
    accs = []
    for hh in range(2):
        s = scores(hh, 0, lead, qt)
        m = _max_over_rows(s)
        acc = pv(hh, 0, lead, s, m)
        for lo in range(lead, nkeys, span):
            acc = acc + pv(hh, lo, lo + span, scores(hh, lo, lo + span, qt), m)
        accs.append(acc)
    out, l0, l1 = finish(*accs)
    lmin = jnp.min(jnp.minimum(l0, l1))
    lmax = jnp.max(jnp.maximum(l0, l1))
    safe = (lmin > 2.0 ** -100) & (lmax < 2.0 ** 100)

    @pl.when(safe)
    def _():
        o_ref[...] = out

    @pl.when(jnp.logical_not(safe))
    def _():
        sub = 2 * HP

        def redo(c, carry):
            r0 = pl.multiple_of(c * sub, sub)
            qs = q_ref[pl.ds(r0, sub), :]
            exact = []
            for hh in range(2):
                s = scores(hh, 0, nkeys, qs)
                exact.append(pv(hh, 0, nkeys, s, _max_over_rows(s)))
            o_ref[pl.ds(r0, sub), :] = finish(*exact)[0]
            return carry

        lax.fori_loop(0, tq // sub, redo, 0)


def _attn(q, k, vt, *, tq, lead, span):
    B, Tp, _ = q.shape
    npair = MLA_HEADS // 2
    return pl.pallas_call(
        functools.partial(_attn_kernel, tq=tq, lead=lead, span=span),
        grid=(B, npair, Tp // tq),
        in_specs=[pl.BlockSpec((None, tq, 2 * HP), lambda b, p, i: (b, i, p)),
                  pl.BlockSpec((None, Tp, 2 * HP), lambda b, p, i: (b, 0, p)),
                  pl.BlockSpec((None, 2 * VROWS, Tp), lambda b, p, i: (b, p, 0))],
        out_specs=pl.BlockSpec((None, tq, HP), lambda b, p, i: (b, i, p)),
        out_shape=jax.ShapeDtypeStruct((B, Tp, npair * HP), jnp.float32),
        compiler_params=pltpu.CompilerParams(
            dimension_semantics=("arbitrary", "arbitrary", "arbitrary"),
            vmem_limit_bytes=VMEM_LIMIT),
        name="attn",
    )(q, k, vt)


def _mix_ffn_kernel(*refs, tm, nt, ns):
    x_ref, xprev_ref, xnext_ref, mtail_ref = refs[:4]
    yl_refs, (ylprev_ref, ylnext_ref) = refs[4:4 + ns], refs[4 + ns:6 + ns]
    ym_refs, (ymprev_ref, ymnext_ref) = refs[6 + ns:6 + 2 * ns], refs[6 + 2 * ns:8 + 2 * ns]
    (mg_ref, wo_ref, g_ref, wg_ref, wu_ref, cw_ref, cb_ref, wd_ref, o_ref,
     gate_ref) = refs[8 + 2 * ns:]
    i = pl.program_id(1)
    f32, bf16 = jnp.float32, jnp.bfloat16
    H = 2 * SUB
    xw = jnp.concatenate([jnp.where(i == 0, mtail_ref[...], xprev_ref[...]), x_ref[...],
                          xnext_ref[...]], axis=0)
    ylw = jnp.concatenate([ylprev_ref[...].astype(f32)[H - SUB:]]
                          + [r[...].astype(f32) for r in yl_refs]
                          + [ylnext_ref[...].astype(f32)[:SUB]], axis=0).astype(bf16)
    ymw = jnp.concatenate([ymprev_ref[...]] + [r[...] for r in ym_refs] + [ymnext_ref[...]],
                          axis=0)
    y = jnp.concatenate([ylw, _rms(ymw, mg_ref[...]).astype(bf16)], axis=1)
    h1 = xw + jnp.dot(y, wo_ref[...], preferred_element_type=f32)
    r = lax.broadcasted_iota(jnp.int32, h1.shape, 0)
    h1 = jnp.where((i == nt - 1) & (r >= tm + SUB), 0.0, h1)
    hn = _rms(h1, g_ref[...]).astype(bf16)
    gate_ref[...] = jnp.dot(hn, wg_ref[...], preferred_element_type=f32)
    up = jnp.dot(hn[SUB:SUB + tm], wu_ref[...], preferred_element_type=f32)
    left = FFN_CONV // 2
    gc = cb_ref[...] + gate_ref[SUB - left:SUB - left + tm, :] * cw_ref[0:1, :]
    for k in range(1, FFN_CONV):
        gc = gc + gate_ref[SUB - left + k:SUB - left + k + tm, :] * cw_ref[k:k + 1, :]
    act = (gc * jax.nn.sigmoid(gc) * up).astype(bf16)
    o_ref[...] = h1[SUB:SUB + tm] + jnp.dot(act, wd_ref[...], preferred_element_type=f32)


def _mix_ffn(x, mtail, yl, ym, mg, wo, g, wg, wu, cw, cb, wd, *, tm):
    B, S, D = x.shape
    Tp = yl.shape[1]
    W = yl.shape[2]
    assert tm % X0 == 0 and S % tm == 0
    nt = S // tm
    ns = tm // X0
    H = 2 * SUB
    full = lambda a: pl.BlockSpec(a.shape, lambda b, i: (0,) * a.ndim)

    def halo(w, rows, total, first_row):
        prev = pl.BlockSpec((None, rows, w),
                            lambda b, i: (b, jnp.maximum(first_row(i) // rows - 1, 0), 0))
        nxt = pl.BlockSpec((None, rows, w),
                           lambda b, i: (b, jnp.minimum((first_row(i) + tm) // rows,
                                                        total // rows - 1), 0))
        return [prev, nxt]

    def mixer_tiles(w):
        return [pl.BlockSpec((None, X0, w), lambda b, i, s=s: (b, 1 + i * ns + s, 0))
                for s in range(ns)]

    seq_row = lambda i: i * tm
    pad_row = lambda i: X0 + i * tm
    in_specs = ([pl.BlockSpec((None, tm, D), lambda b, i: (b, i, 0))] + halo(D, SUB, S, seq_row)
                + [full(mtail)]
                + mixer_tiles(W) + halo(W, H, Tp, pad_row)
                + mixer_tiles(W) + halo(W, SUB, Tp, pad_row)
                + [full(a) for a in (mg, wo, g, wg, wu, cw, cb, wd)])
    return pl.pallas_call(
        functools.partial(_mix_ffn_kernel, tm=tm, nt=nt, ns=ns),
        grid=(B, nt),
        in_specs=in_specs,
        out_specs=pl.BlockSpec((None, tm, D), lambda b, i: (b, i, 0)),
        out_shape=jax.ShapeDtypeStruct((B, S, D), jnp.float32),
        scratch_shapes=[pltpu.VMEM((tm + 2 * SUB, D_FF), jnp.float32)],
        compiler_params=pltpu.CompilerParams(
            dimension_semantics=("arbitrary", "arbitrary"), vmem_limit_bytes=VMEM_LIMIT),
        name="mix_ffn",
    )(x, x, x, mtail, *([yl] * (ns + 2)), *([ym] * (ns + 2)), mg, wo, g, wg, wu, cw, cb, wd)


def _pad_heads(w, width, offset=0):
    K = w.shape[0]
    w = w.reshape(K, MLA_HEADS, width)
    w = jnp.pad(w, ((0, 0), (0, 0), (offset, HP - width - offset)))
    return w.reshape(K, MLA_HEADS * HP)


def _block_diag(w):
    H, Dh, _ = w.shape
    eye = jnp.eye(H, dtype=w.dtype)
    return (eye[:, None, :, None] * w[:, :, None, :]).reshape(H * Dh, H * Dh)


def _gate_weights(w_a, w_x):
    half = LRU_WIDTH // 2
    wa = _block_diag(w_a)
    wx = _block_diag(w_x)
    halves = [jnp.concatenate([wa[c * half:(c + 1) * half, c * half:(c + 1) * half],
                               wx[c * half:(c + 1) * half, c * half:(c + 1) * half]], axis=1)
              for c in range(2)]
    return jnp.stack(halves).astype(jnp.bfloat16)


def kernel(x, meta_tokens, norm_mix_g, w_in, conv_lru_w, conv_lru_b, lru_w_a, lru_b_a, lru_w_x,
           lru_b_x, lru_lambda, lru_gate_g, q_latent_g, w_uq, kv_latent_g, w_ukv, q_norm_g,
           k_norm_g, mla_out_g, w_out, norm_ffn_g, w_ffn_up, conv_ffn_w, conv_ffn_b, w_ffn_down):
    B, S, D = x.shape
    f32, bf16 = jnp.float32, jnp.bfloat16
    Tp = X0 + S
    l = 0
    r2 = lambda a: a.reshape(1, -1).astype(f32)

    head = jnp.concatenate([jnp.zeros((PAD, D), x.dtype), meta_tokens.astype(x.dtype)], axis=0)

    half = QK_ROPE // 2
    lanes = jnp.arange(HP)
    swap = jnp.where((lanes >= QK_NOPE) & (lanes < QK_NOPE + half), lanes + half,
                     jnp.where((lanes >= QK_NOPE + half) & (lanes < QK_DIM), lanes - half, lanes))
    swap_heads = (jnp.arange(MLA_HEADS)[:, None] * HP + swap[None, :]).reshape(-1)

    o = 2 * LRU_WIDTH + Q_LORA + KV_LORA
    kpe_w = jnp.pad(w_in[l][:, o:], ((0, 0), (QK_NOPE, LANE - QK_DIM)))
    win = jnp.concatenate([w_in[l][:, :o], kpe_w, kpe_w[:, swap]], axis=1).astype(bf16)
    wuq = _pad_heads(w_uq[l], QK_DIM).astype(bf16)
    wuqs = wuq[:, swap_heads]
    wkv = w_ukv[l].reshape(KV_LORA, MLA_HEADS, QK_NOPE + V_DIM)
    wuk = _pad_heads(wkv[:, :, :QK_NOPE].reshape(KV_LORA, -1), QK_NOPE).astype(bf16)
    wv = jnp.pad(wkv[:, :, QK_NOPE:], ((0, 0), (0, 0), (0, VROWS - V_DIM)))
    wuvt = wv.reshape(KV_LORA, MLA_HEADS * VROWS).T.astype(bf16)
    qg = jnp.pad(q_norm_g[l], (0, HP - QK_DIM)).reshape(1, HP)
    kg = jnp.pad(k_norm_g[l], (0, HP - QK_DIM)).reshape(1, HP)
    ones = jnp.ones((2 * HP, HP), bf16)

    pos = jnp.arange(Tp, dtype=f32) - PAD
    inv_freq = ROPE_THETA ** (-jnp.arange(0, QK_ROPE, 2, dtype=f32) / QK_ROPE)
    ang = pos[:, None] * inv_freq[None, :]
    cos, sin = jnp.cos(ang), jnp.sin(ang)
    ctab = jnp.concatenate([jnp.ones((Tp, QK_NOPE), f32), cos, cos,
                            jnp.ones((Tp, HP - QK_DIM), f32)], axis=1)
    stab = jnp.concatenate([jnp.zeros((Tp, QK_NOPE), f32), -sin, sin,
                            jnp.zeros((Tp, HP - QK_DIM), f32)], axis=1)

    xl, gl, q, k, vt = _inproj(x, head, r2(norm_mix_g[l]), win, r2(q_latent_g[l]), wuq, wuqs,
                              r2(kv_latent_g[l]), wuk, wuvt, qg, qg[:, swap], kg, kg[:, swap],
                              ones, ctab, stab, tm=X0)

    cw, cb = conv_lru_w[l], r2(conv_lru_b[l])
    wg = [_gate_weights(lru_w_a[l, d], lru_w_x[l, d]) for d in range(2)]
    hf = _lru_fwd(xl, cw, cb, wg[0], r2(lru_b_a[l, 0]), r2(lru_b_x[l, 0]), r2(lru_lambda[l, 0]),
                  tt=264)
    yl = _lru_bwd(xl, cw, cb, wg[1], r2(lru_b_a[l, 1]), r2(lru_b_x[l, 1]), r2(lru_lambda[l, 1]),
                  hf, gl, r2(lru_gate_g[l]), tt=264)

    ym = _attn(q, k, vt, tq=768, lead=768, span=768)

    wup = w_ffn_up[l].astype(bf16)
    return _mix_ffn(x, head[X0 - SUB:], yl, ym, r2(mla_out_g[l]), w_out[l].astype(bf16),
                    r2(norm_ffn_g[l]), wup[:, :D_FF], wup[:, D_FF:], conv_ffn_w[l],
                    r2(conv_ffn_b[l]), w_ffn_down[l].astype(bf16), tm=512)
```

```python
import functools
import math

import jax
import jax.numpy as jnp
from jax import lax
from jax.experimental import pallas as pl
from jax.experimental.pallas import tpu as pltpu

D_MODEL = 1024
N_META = 16
LRU_WIDTH = 512
LRU_HEADS = 8
LRU_HEAD_DIM = 64
LRU_CONV = 4
LRU_C = 8.0
MLA_HEADS = 8
QK_NOPE = 64
QK_ROPE = 32
QK_DIM = 96
V_DIM = 64
Q_LORA = 256
KV_LORA = 128
ROPE_THETA = 10000.0
D_FF = 2816
FFN_CONV = 3
EPS = 1e-6
NEG_INF = -1e30

LANE = 128
SUB = 8
X0 = 256
PAD = X0 - N_META
HP = LANE
MASK_LANE = QK_DIM
VMEM_LIMIT = 56 * 1024 * 1024


def _rms(x, g):
    return x * lax.rsqrt(jnp.mean(x * x, axis=-1, keepdims=True) + EPS) * g


def _bdot(a, b):
    return jnp.dot(a.astype(jnp.bfloat16), b, preferred_element_type=jnp.float32)


def _inproj_kernel(x_ref, head_ref, gmix_ref, win_ref, qlg_ref, wuq_ref, wuqs_ref, kvlg_ref,
                   wuk_ref, wuvt_ref, qg_ref, qgs_ref, kg_ref, kgs_ref, ones_ref, c_ref, s_ref,
                   xlru_ref, glru_ref, q_ref, k_ref, vt_ref, *, tm):
    i = pl.program_id(1)
    h = jnp.where(i == 0, head_ref[...], x_ref[...])
    hn = _rms(h, gmix_ref[...])
    proj = _bdot(hn, win_ref[...])
    for c in range(LRU_WIDTH // LANE):
        xlru_ref[c] = proj[:, c * LANE:(c + 1) * LANE]
    glru_ref[...] = proj[:, LRU_WIDTH:2 * LRU_WIDTH]
    o = 2 * LRU_WIDTH
    cq = proj[:, o:o + Q_LORA]
    ckv = proj[:, o + Q_LORA:o + Q_LORA + KV_LORA]
    o += Q_LORA + KV_LORA
    kpe = proj[:, o:o + HP]
    kpe_sw = proj[:, o + HP:]
    cqn = _rms(cq, qlg_ref[...]).astype(jnp.bfloat16)
    qraw = jnp.dot(cqn, wuq_ref[...], preferred_element_type=jnp.float32)
    qraw_sw = jnp.dot(cqn, wuqs_ref[...], preferred_element_type=jnp.float32)
    ckvn = _rms(ckv, kvlg_ref[...]).astype(jnp.bfloat16)
    kraw = jnp.dot(ckvn, wuk_ref[...], preferred_element_type=jnp.float32)
    vt = lax.dot_general(wuvt_ref[...], ckvn, (((1,), (1,)), ((), ())),
                         preferred_element_type=jnp.float32)
    vrow = lax.broadcasted_iota(jnp.int32, vt.shape, 0) % (2 * HP)
    vt_ref[...] = jnp.where((vrow == V_DIM) | (vrow == HP), 1.0, vt).astype(jnp.bfloat16)

    lane = lax.broadcasted_iota(jnp.int32, (tm, HP), 1)
    row = lax.broadcasted_iota(jnp.int32, (tm, HP), 0) + i * tm
    key_mask = jnp.where(row >= PAD, 0.0, NEG_INF)
    scale = math.log2(math.e) / math.sqrt(QK_DIM)
    cosv, sinv = c_ref[...], s_ref[...]
    q_cos = cosv * (qg_ref[...] * scale)
    q_sin = sinv * (qgs_ref[...] * scale)
    k_cos = cosv * kg_ref[...]
    k_sin_term = kpe_sw * (sinv * kgs_ref[...])

    def inv_rms(xb):
        x2 = xb * xb
        hi = x2.astype(jnp.bfloat16)
        lo = (x2 - hi.astype(jnp.float32)).astype(jnp.bfloat16)
        ss = jnp.dot(jnp.concatenate([hi, lo], axis=1), ones_ref[...],
                     preferred_element_type=jnp.float32)
        return lax.rsqrt(ss * (1.0 / QK_DIM) + EPS)

    for hd in range(MLA_HEADS):
        sl = slice(hd * HP, (hd + 1) * HP)
        xq = qraw[:, sl]
        qb = (xq * q_cos + qraw_sw[:, sl] * q_sin) * inv_rms(xq)
        q_ref[:, sl] = jnp.where(lane == MASK_LANE, 1.0, qb).astype(jnp.bfloat16)
        xk = kraw[:, sl] + kpe
        kb = (xk * k_cos + k_sin_term) * inv_rms(xk)
        k_ref[:, sl] = jnp.where(lane == MASK_LANE, key_mask, kb).astype(jnp.bfloat16)


def _x_tile_spec(tm, D):
    return pl.BlockSpec((None, tm, D), lambda b, i: (b, jnp.maximum(i - 1, 0), 0))


def _inproj(x, head, gmix, win, qlg, wuq, wuqs, kvlg, wuk, wuv, qg, qgs, kg, kgs, ones, ctab,
            stab, *, tm):
    B, S, D = x.shape
    assert tm == X0
    Tp = X0 + S
    nt = Tp // tm
    row = lambda w: pl.BlockSpec((None, tm, w), lambda b, i: (b, i, 0))
    full = lambda a: pl.BlockSpec(a.shape, lambda b, i: (0,) * a.ndim)
    tab = pl.BlockSpec((tm, HP), lambda b, i: (i, 0))
    f32, bf16 = jnp.float32, jnp.bfloat16
    return pl.pallas_call(
        functools.partial(_inproj_kernel, tm=tm),
        grid=(B, nt),
        in_specs=[_x_tile_spec(tm, D), full(head), full(gmix), full(win), full(qlg), full(wuq),
                  full(wuqs), full(kvlg), full(wuk), full(wuv), full(qg), full(qgs), full(kg),
                  full(kgs), full(ones), tab, tab],
        out_specs=[pl.BlockSpec((None, LRU_WIDTH // LANE, tm, LANE), lambda b, i: (b, 0, i, 0)),
                   row(LRU_WIDTH), row(MLA_HEADS * HP), row(MLA_HEADS * HP),
                   pl.BlockSpec((None, MLA_HEADS * HP, tm), lambda b, i: (b, 0, i))],
        out_shape=[jax.ShapeDtypeStruct((B, LRU_WIDTH // LANE, Tp, LANE), f32),
                   jax.ShapeDtypeStruct((B, Tp, LRU_WIDTH), f32),
                   jax.ShapeDtypeStruct((B, Tp, MLA_HEADS * HP), bf16),
                   jax.ShapeDtypeStruct((B, Tp, MLA_HEADS * HP), bf16),
                   jax.ShapeDtypeStruct((B, MLA_HEADS * HP, Tp), bf16)],
        compiler_params=pltpu.CompilerParams(
            dimension_semantics=("arbitrary", "arbitrary"), vmem_limit_bytes=VMEM_LIMIT),
        name="inproj",
    )(x, head, gmix, win, qlg, wuq, wuqs, kvlg, wuk, wuv, qg, qgs, kg, kgs, ones, ctab, stab)


NSLAB = LRU_WIDTH // LANE


def _lru_gates(x_ref, xprev_ref, xnext_ref, cw_ref, cb_ref, wg_ref, ba_ref, bx_ref, lam_ref,
               xw_ref, *, tt, first, last, t0):
    W = LRU_WIDTH
    seg = tt // SUB
    xw_ref[:, 0:SUB, :] = jnp.where(first, 0.0, xprev_ref[...])
    xw_ref[:, SUB:SUB + tt, :] = x_ref[...]
    xw_ref[:, SUB + tt:, :] = jnp.where(last, 0.0, xnext_ref[...])
    left = LRU_CONV // 2
    slabs = []
    for c in range(NSLAB):
        cs = slice(c * LANE, (c + 1) * LANE)
        pieces = []
        for j in range(seg):
            acc = cb_ref[:, cs]
            for k in range(LRU_CONV):
                tap = xw_ref[c, pl.ds(SUB - left + j + k, SUB, stride=seg), :]
                acc = acc + tap * cw_ref[k:k + 1, cs]
            pieces.append(acc)
        slabs.append(jnp.concatenate(pieces, axis=0))
    xc = jnp.concatenate(slabs, axis=1)
    xcb = xc.astype(jnp.bfloat16)
    half = W // 2
    g0 = jnp.dot(xcb[:, :half], wg_ref[0], preferred_element_type=jnp.float32)
    g1 = jnp.dot(xcb[:, half:], wg_ref[1], preferred_element_type=jnp.float32)
    ra = jnp.concatenate([g0[:, :half], g1[:, :half]], axis=1) + ba_ref[...]
    ri = jnp.concatenate([g0[:, half:], g1[:, half:]], axis=1) + bx_ref[...]
    r = jax.nn.sigmoid(ra)
    ig = jax.nn.sigmoid(ri)
    lam = lam_ref[...]
    sp = jnp.maximum(-lam, 0.0) + jnp.log(1.0 + jnp.exp(-jnp.abs(lam)))
    log_a = -LRU_C * r * sp
    a = jnp.exp(log_a)
    u = jnp.sqrt(1.0 - a * a) * (ig * xc)
    row = lax.broadcasted_iota(jnp.int32, (tt, W), 0)
    time = t0 + (row % SUB) * seg + row // SUB
    return a, jnp.where(time >= PAD, u, 0.0)


def _scan_block(a, u, h_ref, carry_ref, *, tt, reverse):
    W = LRU_WIDTH
    seg = tt // SUB
    order = range(seg - 1, -1, -1) if reverse else range(seg)
    h = jnp.zeros((SUB, W), jnp.float32)
    p = jnp.ones((SUB, W), jnp.float32)
    hs, ps = {}, {}
    for j in order:
        aj = a[j * SUB:(j + 1) * SUB]
        h = aj * h + u[j * SUB:(j + 1) * SUB]
        p = aj * p
        hs[j], ps[j] = h, p
    sub = lax.broadcasted_iota(jnp.int32, (SUB, W), 0)
    e, pf = h, p
    for d in (1, 2, 4):
        if reverse:
            keep, sh = sub < SUB - d, SUB - d
        else:
            keep, sh = sub >= d, d
        e_sh = jnp.where(keep, pltpu.roll(e, sh, axis=0), 0.0)
        pf_sh = jnp.where(keep, pltpu.roll(pf, sh, axis=0), 1.0)
        e = e + pf * e_sh
        pf = pf * pf_sh
    carry = carry_ref[...]
    leaving = e + pf * carry
    if reverse:
        entering = jnp.where(sub < SUB - 1, pltpu.roll(leaving, SUB - 1, axis=0), carry)
        carry_ref[...] = leaving[0:1]
    else:
        entering = jnp.where(sub >= 1, pltpu.roll(leaving, 1, axis=0), carry)
        carry_ref[...] = leaving[SUB - 1:SUB]
    for j in range(seg):
        hj = hs[j] + ps[j] * entering
        for c in range(NSLAB):
            h_ref[c, pl.ds(j, SUB, stride=seg), :] = hj[:, c * LANE:(c + 1) * LANE]


def _lru_fwd_kernel(x_ref, xprev_ref, xnext_ref, cw_ref, cb_ref, wg_ref, ba_ref, bx_ref, lam_ref,
                    hf_ref, xw_ref, carry_ref, *, tt, nt):
    i = pl.program_id(1)

    @pl.when(i == 0)
    def _():
        carry_ref[...] = jnp.zeros_like(carry_ref)

    a, u = _lru_gates(x_ref, xprev_ref, xnext_ref, cw_ref, cb_ref, wg_ref, ba_ref, bx_ref,
                      lam_ref, xw_ref, tt=tt, first=i == 0, last=i == nt - 1, t0=i * tt)
    _scan_block(a, u, hf_ref, carry_ref, tt=tt, reverse=False)


def _lru_bwd_kernel(x_ref, xprev_ref, xnext_ref, cw_ref, cb_ref, wg_ref, ba_ref, bx_ref, lam_ref,
                    hf_ref, gl_ref, gg_ref, y_ref, xw_ref, hb_ref, carry_ref, *, tt, nt):
    j = pl.program_id(1)
    i = nt - 1 - j

    @pl.when(j == 0)
    def _():
        carry_ref[...] = jnp.zeros_like(carry_ref)

    a, u = _lru_gates(x_ref, xprev_ref, xnext_ref, cw_ref, cb_ref, wg_ref, ba_ref, bx_ref,
                      lam_ref, xw_ref, tt=tt, first=i == 0, last=i == nt - 1, t0=i * tt)
    _scan_block(a, u, hb_ref, carry_ref, tt=tt, reverse=True)
    ys = [(hf_ref[c] + hb_ref[c]) * jax.nn.gelu(gl_ref[:, c * LANE:(c + 1) * LANE],
                                                approximate=True) for c in range(NSLAB)]
    y = jnp.concatenate(ys, axis=1)
    y_ref[...] = _rms(y, gg_ref[...]).astype(y_ref.dtype)


def _lru_specs(tt, nt, Tp, tmap):
    nsub = Tp // SUB
    per = tt // SUB
    slab = lambda rows, f: pl.BlockSpec((None, NSLAB, rows, LANE), lambda b, j: (b, 0, f(j), 0))
    blk = slab(tt, tmap)
    prev = slab(SUB, lambda j: jnp.maximum(tmap(j) * per - 1, 0))
    nxt = slab(SUB, lambda j: jnp.minimum((tmap(j) + 1) * per, nsub - 1))
    return blk, prev, nxt


def _lru_fwd(xl, cw, cb, wg, ba, bx, lam, *, tt):
    B, _, Tp, _ = xl.shape
    nt = Tp // tt
    blk, prev, nxt = _lru_specs(tt, nt, Tp, lambda j: j)
    full = lambda a: pl.BlockSpec(a.shape, lambda b, j: (0,) * a.ndim)
    return pl.pallas_call(
        functools.partial(_lru_fwd_kernel, tt=tt, nt=nt),
        grid=(B, nt),
        in_specs=[blk, prev, nxt, full(cw), full(cb), full(wg), full(ba), full(bx), full(lam)],
        out_specs=blk,
        out_shape=jax.ShapeDtypeStruct((B, NSLAB, Tp, LANE), jnp.float32),
        scratch_shapes=[pltpu.VMEM((NSLAB, tt + 2 * SUB, LANE), jnp.float32),
                        pltpu.VMEM((1, LRU_WIDTH), jnp.float32)],
        compiler_params=pltpu.CompilerParams(
            dimension_semantics=("arbitrary", "arbitrary"), vmem_limit_bytes=VMEM_LIMIT),
        name="lru_fwd",
    )(xl, xl, xl, cw, cb, wg, ba, bx, lam)


def _lru_bwd(xl, cw, cb, wg, ba, bx, lam, hf, gl, gg, *, tt):
    B, _, Tp, _ = xl.shape
    W = LRU_WIDTH
    nt = Tp // tt
    rev = lambda j: nt - 1 - j
    blk, prev, nxt = _lru_specs(tt, nt, Tp, rev)
    row = pl.BlockSpec((None, tt, W), lambda b, j: (b, rev(j), 0))
    full = lambda a: pl.BlockSpec(a.shape, lambda b, j: (0,) * a.ndim)
    return pl.pallas_call(
        functools.partial(_lru_bwd_kernel, tt=tt, nt=nt),
        grid=(B, nt),
        in_specs=[blk, prev, nxt, full(cw), full(cb), full(wg), full(ba), full(bx), full(lam),
                  blk, row, full(gg)],
        out_specs=row,
        out_shape=jax.ShapeDtypeStruct((B, Tp, W), jnp.bfloat16),
        scratch_shapes=[pltpu.VMEM((NSLAB, tt + 2 * SUB, LANE), jnp.float32),
                        pltpu.VMEM((NSLAB, tt, LANE), jnp.float32),
                        pltpu.VMEM((1, LRU_WIDTH), jnp.float32)],
        compiler_params=pltpu.CompilerParams(
            dimension_semantics=("arbitrary", "arbitrary"), vmem_limit_bytes=VMEM_LIMIT),
        name="lru_bwd",
    )(xl, xl, xl, cw, cb, wg, ba, bx, lam, hf, gl, gg)


def _max_over_rows(s):
    rows = s.shape[0]
    while rows % (2 * SUB) == 0 and rows > 32 * SUB:
        fold = 4 if rows % (4 * SUB) == 0 else 2
        rows //= fold
        s = jnp.max(s.reshape(fold, rows, s.shape[1]), axis=0)
    return jnp.max(s, axis=0, keepdims=True)


def _attn_kernel(q_ref, k_ref, vt_ref, o_ref, *, tq, lead, span):
    heads = [slice(hh * HP, (hh + 1) * HP) for hh in range(2)]
    nkeys = k_ref.shape[0]
    nt = (((1,), (1,)), ((), ()))
    rows = lax.broadcasted_iota(jnp.int32, (HP, tq), 0)

    def scores(hh, lo, hi):
        return lax.dot_general(k_ref[lo:hi, heads[hh]], q_ref[:, heads[hh]], nt,
                               preferred_element_type=jnp.float32)

    def pv(hh, lo, hi, s, m):
        return jnp.dot(vt_ref[heads[hh], lo:hi], jnp.exp2(s - m).astype(jnp.bfloat16),
                       preferred_element_type=jnp.float32)

    def finish(a0, a1):
        l0, l1 = a0[V_DIM:V_DIM + 1, :], a1[0:1, :]
        return jnp.where(rows < V_DIM, a0 / l0, a1 / l1).T, l0, l1

    accs = []
    for hh in range(2):
        s = scores(hh, 0, lead)
        m = _max_over_rows(s)
        acc = pv(hh, 0, lead, s, m)
        for lo in range(lead, nkeys, span):
            acc = acc + pv(hh, lo, lo + span, scores(hh, lo, lo + span), m)
        accs.append(acc)
    out, l0, l1 = finish(*accs)
    lmin = jnp.min(jnp.minimum(l0, l1))
    lmax = jnp.max(jnp.maximum(l0, l1))
    safe = (lmin > 2.0 ** -100) & (lmax < 2.0 ** 100)

    @pl.when(safe)
    def _():
        o_ref[...] = out

    @pl.when(jnp.logical_not(safe))
    def _():
        exact = []
        for hh in range(2):
            s = scores(hh, 0, nkeys)
            exact.append(pv(hh, 0, nkeys, s, _max_over_rows(s)))
        o_ref[...] = finish(*exact)[0]


def _attn(q, k, vt, *, tq, lead, span):
    B, Tp, _ = q.shape
    npair = MLA_HEADS // 2
    return pl.pallas_call(
        functools.partial(_attn_kernel, tq=tq, lead=lead, span=span),
        grid=(B, npair, Tp // tq),
        in_specs=[pl.BlockSpec((None, tq, 2 * HP), lambda b, p, i: (b, i, p)),
                  pl.BlockSpec((None, Tp, 2 * HP), lambda b, p, i: (b, 0, p)),
                  pl.BlockSpec((None, 2 * HP, Tp), lambda b, p, i: (b, p, 0))],
        out_specs=pl.BlockSpec((None, tq, HP), lambda b, p, i: (b, i, p)),
        out_shape=jax.ShapeDtypeStruct((B, Tp, npair * HP), jnp.float32),
        compiler_params=pltpu.CompilerParams(
            dimension_semantics=("arbitrary", "arbitrary", "arbitrary"),
            vmem_limit_bytes=VMEM_LIMIT),
        name="attn",
    )(q, k, vt)


def _mix_ffn_kernel(*refs, tm, nt, ns):
    x_ref, xprev_ref, xnext_ref, mtail_ref = refs[:4]
    yl_refs, (ylprev_ref, ylnext_ref) = refs[4:4 + ns], refs[4 + ns:6 + ns]
    ym_refs, (ymprev_ref, ymnext_ref) = refs[6 + ns:6 + 2 * ns], refs[6 + 2 * ns:8 + 2 * ns]
    (mg_ref, wo_ref, g_ref, wg_ref, wu_ref, cw_ref, cb_ref, wd_ref, o_ref,
     gate_ref) = refs[8 + 2 * ns:]
    i = pl.program_id(1)
    f32, bf16 = jnp.float32, jnp.bfloat16
    H = 2 * SUB
    xw = jnp.concatenate([jnp.where(i == 0, mtail_ref[...], xprev_ref[...]), x_ref[...],
                          xnext_ref[...]], axis=0)
    ylw = jnp.concatenate([ylprev_ref[...].astype(f32)[H - SUB:]]
                          + [r[...].astype(f32) for r in yl_refs]
                          + [ylnext_ref[...].astype(f32)[:SUB]], axis=0).astype(bf16)
    ymw = jnp.concatenate([ymprev_ref[...]] + [r[...] for r in ym_refs] + [ymnext_ref[...]],
                          axis=0)
    y = jnp.concatenate([ylw, _rms(ymw, mg_ref[...]).astype(bf16)], axis=1)
    h1 = xw + jnp.dot(y, wo_ref[...], preferred_element_type=f32)
    r = lax.broadcasted_iota(jnp.int32, h1.shape, 0)
    h1 = jnp.where((i == nt - 1) & (r >= tm + SUB), 0.0, h1)
    hn = _rms(h1, g_ref[...]).astype(bf16)
    gate_ref[...] = jnp.dot(hn, wg_ref[...], preferred_element_type=f32)
    up = jnp.dot(hn[SUB:SUB + tm], wu_ref[...], preferred_element_type=f32)
    left = FFN_CONV // 2
    gc = cb_ref[...] + gate_ref[SUB - left:SUB - left + tm, :] * cw_ref[0:1, :]
    for k in range(1, FFN_CONV):
        gc = gc + gate_ref[SUB - left + k:SUB - left + k + tm, :] * cw_ref[k:k + 1, :]
    act = (gc * jax.nn.sigmoid(gc) * up).astype(bf16)
    o_ref[...] = h1[SUB:SUB + tm] + jnp.dot(act, wd_ref[...], preferred_element_type=f32)


def _mix_ffn(x, mtail, yl, ym, mg, wo, g, wg, wu, cw, cb, wd, *, tm):
    B, S, D = x.shape
    Tp = yl.shape[1]
    W = yl.shape[2]
    assert tm % X0 == 0 and S % tm == 0
    nt = S // tm
    ns = tm // X0
    H = 2 * SUB
    full = lambda a: pl.BlockSpec(a.shape, lambda b, i: (0,) * a.ndim)

    def halo(w, rows, total, first_row):
        prev = pl.BlockSpec((None, rows, w),
                            lambda b, i: (b, jnp.maximum(first_row(i) // rows - 1, 0), 0))
        nxt = pl.BlockSpec((None, rows, w),
                           lambda b, i: (b, jnp.minimum((first_row(i) + tm) // rows,
                                                        total // rows - 1), 0))
        return [prev, nxt]

    def mixer_tiles(w):
        return [pl.BlockSpec((None, X0, w), lambda b, i, s=s: (b, 1 + i * ns + s, 0))
                for s in range(ns)]

    seq_row = lambda i: i * tm
    pad_row = lambda i: X0 + i * tm
    in_specs = ([pl.BlockSpec((None, tm, D), lambda b, i: (b, i, 0))] + halo(D, SUB, S, seq_row)
                + [full(mtail)]
                + mixer_tiles(W) + halo(W, H, Tp, pad_row)
                + mixer_tiles(W) + halo(W, SUB, Tp, pad_row)
                + [full(a) for a in (mg, wo, g, wg, wu, cw, cb, wd)])
    return pl.pallas_call(
        functools.partial(_mix_ffn_kernel, tm=tm, nt=nt, ns=ns),
        grid=(B, nt),
        in_specs=in_specs,
        out_specs=pl.BlockSpec((None, tm, D), lambda b, i: (b, i, 0)),
        out_shape=jax.ShapeDtypeStruct((B, S, D), jnp.float32),
        scratch_shapes=[pltpu.VMEM((tm + 2 * SUB, D_FF), jnp.float32)],
        compiler_params=pltpu.CompilerParams(
            dimension_semantics=("arbitrary", "arbitrary"), vmem_limit_bytes=VMEM_LIMIT),
        name="mix_ffn",
    )(x, x, x, mtail, *([yl] * (ns + 2)), *([ym] * (ns + 2)), mg, wo, g, wg, wu, cw, cb, wd)


def _pad_heads(w, width, offset=0):
    K = w.shape[0]
    w = w.reshape(K, MLA_HEADS, width)
    w = jnp.pad(w, ((0, 0), (0, 0), (offset, HP - width - offset)))
    return w.reshape(K, MLA_HEADS * HP)


def _block_diag(w):
    H, Dh, _ = w.shape
    eye = jnp.eye(H, dtype=w.dtype)
    return (eye[:, None, :, None] * w[:, :, None, :]).reshape(H * Dh, H * Dh)


def _gate_weights(w_a, w_x):
    half = LRU_WIDTH // 2
    wa = _block_diag(w_a)
    wx = _block_diag(w_x)
    halves = [jnp.concatenate([wa[c * half:(c + 1) * half, c * half:(c + 1) * half],
                               wx[c * half:(c + 1) * half, c * half:(c + 1) * half]], axis=1)
              for c in range(2)]
    return jnp.stack(halves).astype(jnp.bfloat16)


def kernel(x, meta_tokens, norm_mix_g, w_in, conv_lru_w, conv_lru_b, lru_w_a, lru_b_a, lru_w_x,
           lru_b_x, lru_lambda, lru_gate_g, q_latent_g, w_uq, kv_latent_g, w_ukv, q_norm_g,
           k_norm_g, mla_out_g, w_out, norm_ffn_g, w_ffn_up, conv_ffn_w, conv_ffn_b, w_ffn_down):
    B, S, D = x.shape
    f32, bf16 = jnp.float32, jnp.bfloat16
    Tp = X0 + S
    l = 0
    r2 = lambda a: a.reshape(1, -1).astype(f32)

    head = jnp.concatenate([jnp.zeros((PAD, D), x.dtype), meta_tokens.astype(x.dtype)], axis=0)

    half = QK_ROPE // 2
    lanes = jnp.arange(HP)
    swap = jnp.where((lanes >= QK_NOPE) & (lanes < QK_NOPE + half), lanes + half,
                     jnp.where((lanes >= QK_NOPE + half) & (lanes < QK_DIM), lanes - half, lanes))
    swap_heads = (jnp.arange(MLA_HEADS)[:, None] * HP + swap[None, :]).reshape(-1)

    o = 2 * LRU_WIDTH + Q_LORA + KV_LORA
    kpe_w = jnp.pad(w_in[l][:, o:], ((0, 0), (QK_NOPE, LANE - QK_DIM)))
    win = jnp.concatenate([w_in[l][:, :o], kpe_w, kpe_w[:, swap]], axis=1).astype(bf16)
    wuq = _pad_heads(w_uq[l], QK_DIM).astype(bf16)
    wuqs = wuq[:, swap_heads]
    wkv = w_ukv[l].reshape(KV_LORA, MLA_HEADS, QK_NOPE + V_DIM)
    wuk = _pad_heads(wkv[:, :, :QK_NOPE].reshape(KV_LORA, -1), QK_NOPE).astype(bf16)
    wv = jnp.pad(wkv[:, :, QK_NOPE:], ((0, 0), (0, 0), (0, HP - V_DIM)))
    wv = jnp.where((jnp.arange(MLA_HEADS) % 2 == 1)[None, :, None],
                   jnp.roll(wv, V_DIM, axis=2), wv)
    wuvt = wv.reshape(KV_LORA, MLA_HEADS * HP).T.astype(bf16)
    qg = jnp.pad(q_norm_g[l], (0, HP - QK_DIM)).reshape(1, HP)
    kg = jnp.pad(k_norm_g[l], (0, HP - QK_DIM)).reshape(1, HP)
    ones = jnp.ones((2 * HP, HP), bf16)

    pos = jnp.arange(Tp, dtype=f32) - PAD
    inv_freq = ROPE_THETA ** (-jnp.arange(0, QK_ROPE, 2, dtype=f32) / QK_ROPE)
    ang = pos[:, None] * inv_freq[None, :]
    cos, sin = jnp.cos(ang), jnp.sin(ang)
    ctab = jnp.concatenate([jnp.ones((Tp, QK_NOPE), f32), cos, cos,
                            jnp.ones((Tp, HP - QK_DIM), f32)], axis=1)
    stab = jnp.concatenate([jnp.zeros((Tp, QK_NOPE), f32), -sin, sin,
                            jnp.zeros((Tp, HP - QK_DIM), f32)], axis=1)

    xl, gl, q, k, vt = _inproj(x, head, r2(norm_mix_g[l]), win, r2(q_latent_g[l]), wuq, wuqs,
                              r2(kv_latent_g[l]), wuk, wuvt, qg, qg[:, swap], kg, kg[:, swap],
                              ones, ctab, stab, tm=X0)

    cw, cb = conv_lru_w[l], r2(conv_lru_b[l])
    wg = [_gate_weights(lru_w_a[l, d], lru_w_x[l, d]) for d in range(2)]
    hf = _lru_fwd(xl, cw, cb, wg[0], r2(lru_b_a[l, 0]), r2(lru_b_x[l, 0]), r2(lru_lambda[l, 0]),
                  tt=264)
    yl = _lru_bwd(xl, cw, cb, wg[1], r2(lru_b_a[l, 1]), r2(lru_b_x[l, 1]), r2(lru_lambda[l, 1]),
                  hf, gl, r2(lru_gate_g[l]), tt=264)

    ym = _attn(q, k, vt, tq=768, lead=768, span=768)

    wup = w_ffn_up[l].astype(bf16)
    return _mix_ffn(x, head[X0 - SUB:], yl, ym, r2(mla_out_g[l]), w_out[l].astype(bf16),
                    r2(norm_ffn_g[l]), wup[:, :D_FF], wup[:, D_FF:], conv_ffn_w[l],
                    r2(conv_ffn_b[l]), w_ffn_down[l].astype(bf16), tm=512)
```

```python
import functools
import math

import jax
import jax.numpy as jnp
from jax import lax
from jax.experimental import pallas as pl
from jax.experimental.pallas import tpu as pltpu

D_MODEL = 1024
N_META = 16
LRU_WIDTH = 512
LRU_HEADS = 8
LRU_HEAD_DIM = 64
LRU_CONV = 4
LRU_C = 8.0
MLA_HEADS = 8
QK_NOPE = 64
QK_ROPE = 32
QK_DIM = 96
V_DIM = 64
Q_LORA = 256
KV_LORA = 128
ROPE_THETA = 10000.0
D_FF = 2816
FFN_CONV = 3
EPS = 1e-6
NEG_INF = -1e30

LANE = 128
SUB = 8
X0 = 256
PAD = X0 - N_META
HP = LANE
MASK_LANE = QK_DIM
VROWS = V_DIM + 2 * SUB
VMEM_LIMIT = 56 * 1024 * 1024


def _rms(x, g):
    return x * lax.rsqrt(jnp.mean(x * x, axis=-1, keepdims=True) + EPS) * g


def _bdot(a, b):
    return jnp.dot(a.astype(jnp.bfloat16), b, preferred_element_type=jnp.float32)


def _inproj_kernel(x_ref, head_ref, gmix_ref, win_ref, qlg_ref, wuq_ref, wuqs_ref, kvlg_ref,
                   wuk_ref, wuvt_ref, qg_ref, qgs_ref, kg_ref, kgs_ref, ones_ref, c_ref, s_ref,
                   xlru_ref, glru_ref, q_ref, k_ref, vt_ref, *, tm):
    i = pl.program_id(1)
    h = jnp.where(i == 0, head_ref[...], x_ref[...])
    hn = _rms(h, gmix_ref[...])
    proj = _bdot(hn, win_ref[...])
    for c in range(LRU_WIDTH // LANE):
        xlru_ref[c] = proj[:, c * LANE:(c + 1) * LANE]
    glru_ref[...] = proj[:, LRU_WIDTH:2 * LRU_WIDTH]
    o = 2 * LRU_WIDTH
    cq = proj[:, o:o + Q_LORA]
    ckv = proj[:, o + Q_LORA:o + Q_LORA + KV_LORA]
    o += Q_LORA + KV_LORA
    kpe = proj[:, o:o + HP]
    kpe_sw = proj[:, o + HP:]
    cqn = _rms(cq, qlg_ref[...]).astype(jnp.bfloat16)
    qraw = jnp.dot(cqn, wuq_ref[...], preferred_element_type=jnp.float32)
    qraw_sw = jnp.dot(cqn, wuqs_ref[...], preferred_element_type=jnp.float32)
    ckvn = _rms(ckv, kvlg_ref[...]).astype(jnp.bfloat16)
    kraw = jnp.dot(ckvn, wuk_ref[...], preferred_element_type=jnp.float32)
    vt = lax.dot_general(wuvt_ref[...], ckvn, (((1,), (1,)), ((), ())),
                         preferred_element_type=jnp.float32)
    vrow = lax.broadcasted_iota(jnp.int32, vt.shape, 0) % VROWS
    vt_ref[...] = jnp.where(vrow == V_DIM, 1.0, vt).astype(jnp.bfloat16)

    lane = lax.broadcasted_iota(jnp.int32, (tm, HP), 1)
    row = lax.broadcasted_iota(jnp.int32, (tm, HP), 0) + i * tm
    key_mask = jnp.where(row >= PAD, 0.0, NEG_INF)
    scale = math.log2(math.e) / math.sqrt(QK_DIM)
    cosv, sinv = c_ref[...], s_ref[...]
    q_cos = cosv * (qg_ref[...] * scale)
    q_sin = sinv * (qgs_ref[...] * scale)
    k_cos = cosv * kg_ref[...]
    k_sin_term = kpe_sw * (sinv * kgs_ref[...])

    def inv_rms(xb):
        x2 = xb * xb
        hi = x2.astype(jnp.bfloat16)
        lo = (x2 - hi.astype(jnp.float32)).astype(jnp.bfloat16)
        ss = jnp.dot(jnp.concatenate([hi, lo], axis=1), ones_ref[...],
                     preferred_element_type=jnp.float32)
        return lax.rsqrt(ss * (1.0 / QK_DIM) + EPS)

    for hd in range(MLA_HEADS):
        sl = slice(hd * HP, (hd + 1) * HP)
        xq = qraw[:, sl]
        qb = (xq * q_cos + qraw_sw[:, sl] * q_sin) * inv_rms(xq)
        q_ref[:, sl] = jnp.where(lane == MASK_LANE, 1.0, qb).astype(jnp.bfloat16)
        xk = kraw[:, sl] + kpe
        kb = (xk * k_cos + k_sin_term) * inv_rms(xk)
        k_ref[:, sl] = jnp.where(lane == MASK_LANE, key_mask, kb).astype(jnp.bfloat16)


def _x_tile_spec(tm, D):
    return pl.BlockSpec((None, tm, D), lambda b, i: (b, jnp.maximum(i - 1, 0), 0))


def _inproj(x, head, gmix, win, qlg, wuq, wuqs, kvlg, wuk, wuv, qg, qgs, kg, kgs, ones, ctab,
            stab, *, tm):
    B, S, D = x.shape
    assert tm == X0
    Tp = X0 + S
    nt = Tp // tm
    row = lambda w: pl.BlockSpec((None, tm, w), lambda b, i: (b, i, 0))
    full = lambda a: pl.BlockSpec(a.shape, lambda b, i: (0,) * a.ndim)
    tab = pl.BlockSpec((tm, HP), lambda b, i: (i, 0))
    f32, bf16 = jnp.float32, jnp.bfloat16
    return pl.pallas_call(
        functools.partial(_inproj_kernel, tm=tm),
        grid=(B, nt),
        in_specs=[_x_tile_spec(tm, D), full(head), full(gmix), full(win), full(qlg), full(wuq),
                  full(wuqs), full(kvlg), full(wuk), full(wuv), full(qg), full(qgs), full(kg),
                  full(kgs), full(ones), tab, tab],
        out_specs=[pl.BlockSpec((None, LRU_WIDTH // LANE, tm, LANE), lambda b, i: (b, 0, i, 0)),
                   row(LRU_WIDTH), row(MLA_HEADS * HP), row(MLA_HEADS * HP),
                   pl.BlockSpec((None, MLA_HEADS * VROWS, tm), lambda b, i: (b, 0, i))],
        out_shape=[jax.ShapeDtypeStruct((B, LRU_WIDTH // LANE, Tp, LANE), f32),
                   jax.ShapeDtypeStruct((B, Tp, LRU_WIDTH), f32),
                   jax.ShapeDtypeStruct((B, Tp, MLA_HEADS * HP), bf16),
                   jax.ShapeDtypeStruct((B, Tp, MLA_HEADS * HP), bf16),
                   jax.ShapeDtypeStruct((B, MLA_HEADS * VROWS, Tp), bf16)],
        compiler_params=pltpu.CompilerParams(
            dimension_semantics=("arbitrary", "arbitrary"), vmem_limit_bytes=VMEM_LIMIT),
        name="inproj",
    )(x, head, gmix, win, qlg, wuq, wuqs, kvlg, wuk, wuv, qg, qgs, kg, kgs, ones, ctab, stab)


NSLAB = LRU_WIDTH // LANE


def _lru_gates(x_ref, xprev_ref, xnext_ref, cw_ref, cb_ref, wg_ref, ba_ref, bx_ref, lam_ref,
               xw_ref, *, tt, first, last, t0):
    W = LRU_WIDTH
    seg = tt // SUB
    xw_ref[:, 0:SUB, :] = jnp.where(first, 0.0, xprev_ref[...])
    xw_ref[:, SUB:SUB + tt, :] = x_ref[...]
    xw_ref[:, SUB + tt:, :] = jnp.where(last, 0.0, xnext_ref[...])
    left = LRU_CONV // 2
    slabs = []
    for c in range(NSLAB):
        cs = slice(c * LANE, (c + 1) * LANE)
        pieces = []
        for j in range(seg):
            acc = cb_ref[:, cs]
            for k in range(LRU_CONV):
                tap = xw_ref[c, pl.ds(SUB - left + j + k, SUB, stride=seg), :]
                acc = acc + tap * cw_ref[k:k + 1, cs]
            pieces.append(acc)
        slabs.append(jnp.concatenate(pieces, axis=0))
    xc = jnp.concatenate(slabs, axis=1)
    xcb = xc.astype(jnp.bfloat16)
    half = W // 2
    g0 = jnp.dot(xcb[:, :half], wg_ref[0], preferred_element_type=jnp.float32)
    g1 = jnp.dot(xcb[:, half:], wg_ref[1], preferred_element_type=jnp.float32)
    ra = jnp.concatenate([g0[:, :half], g1[:, :half]], axis=1) + ba_ref[...]
    ri = jnp.concatenate([g0[:, half:], g1[:, half:]], axis=1) + bx_ref[...]
    r = jax.nn.sigmoid(ra)
    ig = jax.nn.sigmoid(ri)
    lam = lam_ref[...]
    sp = jnp.maximum(-lam, 0.0) + jnp.log(1.0 + jnp.exp(-jnp.abs(lam)))
    log_a = -LRU_C * r * sp
    a = jnp.exp(log_a)
    u = jnp.sqrt(1.0 - a * a) * (ig * xc)
    row = lax.broadcasted_iota(jnp.int32, (tt, W), 0)
    time = t0 + (row % SUB) * seg + row // SUB
    return a, jnp.where(time >= PAD, u, 0.0)


def _scan_block(a, u, h_ref, carry_ref, *, tt, reverse):
    W = LRU_WIDTH
    seg = tt // SUB
    order = range(seg - 1, -1, -1) if reverse else range(seg)
    h = jnp.zeros((SUB, W), jnp.float32)
    p = jnp.ones((SUB, W), jnp.float32)
    hs, ps = {}, {}
    for j in order:
        aj = a[j * SUB:(j + 1) * SUB]
        h = aj * h + u[j * SUB:(j + 1) * SUB]
        p = aj * p
        hs[j], ps[j] = h, p
    sub = lax.broadcasted_iota(jnp.int32, (SUB, W), 0)
    e, pf = h, p
    for d in (1, 2, 4):
        if reverse:
            keep, sh = sub < SUB - d, SUB - d
        else:
            keep, sh = sub >= d, d
        e_sh = jnp.where(keep, pltpu.roll(e, sh, axis=0), 0.0)
        pf_sh = jnp.where(keep, pltpu.roll(pf, sh, axis=0), 1.0)
        e = e + pf * e_sh
        pf = pf * pf_sh
    carry = carry_ref[...]
    leaving = e + pf * carry
    if reverse:
        entering = jnp.where(sub < SUB - 1, pltpu.roll(leaving, SUB - 1, axis=0), carry)
        carry_ref[...] = leaving[0:1]
    else:
        entering = jnp.where(sub >= 1, pltpu.roll(leaving, 1, axis=0), carry)
        carry_ref[...] = leaving[SUB - 1:SUB]
    for j in range(seg):
        hj = hs[j] + ps[j] * entering
        for c in range(NSLAB):
            h_ref[c, pl.ds(j, SUB, stride=seg), :] = hj[:, c * LANE:(c + 1) * LANE]


def _lru_fwd_kernel(x_ref, xprev_ref, xnext_ref, cw_ref, cb_ref, wg_ref, ba_ref, bx_ref, lam_ref,
                    hf_ref, xw_ref, carry_ref, *, tt, nt):
    i = pl.program_id(1)

    @pl.when(i == 0)
    def _():
        carry_ref[...] = jnp.zeros_like(carry_ref)

    a, u = _lru_gates(x_ref, xprev_ref, xnext_ref, cw_ref, cb_ref, wg_ref, ba_ref, bx_ref,
                      lam_ref, xw_ref, tt=tt, first=i == 0, last=i == nt - 1, t0=i * tt)
    _scan_block(a, u, hf_ref, carry_ref, tt=tt, reverse=False)


def _lru_bwd_kernel(x_ref, xprev_ref, xnext_ref, cw_ref, cb_ref, wg_ref, ba_ref, bx_ref, lam_ref,
                    hf_ref, gl_ref, gg_ref, y_ref, xw_ref, hb_ref, carry_ref, *, tt, nt):
    j = pl.program_id(1)
    i = nt - 1 - j

    @pl.when(j == 0)
    def _():
        carry_ref[...] = jnp.zeros_like(carry_ref)

    a, u = _lru_gates(x_ref, xprev_ref, xnext_ref, cw_ref, cb_ref, wg_ref, ba_ref, bx_ref,
                      lam_ref, xw_ref, tt=tt, first=i == 0, last=i == nt - 1, t0=i * tt)
    _scan_block(a, u, hb_ref, carry_ref, tt=tt, reverse=True)
    ys = [(hf_ref[c] + hb_ref[c]) * jax.nn.gelu(gl_ref[:, c * LANE:(c + 1) * LANE],
                                                approximate=True) for c in range(NSLAB)]
    y = jnp.concatenate(ys, axis=1)
    y_ref[...] = _rms(y, gg_ref[...]).astype(y_ref.dtype)


def _lru_specs(tt, nt, Tp, tmap):
    nsub = Tp // SUB
    per = tt // SUB
    slab = lambda rows, f: pl.BlockSpec((None, NSLAB, rows, LANE), lambda b, j: (b, 0, f(j), 0))
    blk = slab(tt, tmap)
    prev = slab(SUB, lambda j: jnp.maximum(tmap(j) * per - 1, 0))
    nxt = slab(SUB, lambda j: jnp.minimum((tmap(j) + 1) * per, nsub - 1))
    return blk, prev, nxt


def _lru_fwd(xl, cw, cb, wg, ba, bx, lam, *, tt):
    B, _, Tp, _ = xl.shape
    nt = Tp // tt
    blk, prev, nxt = _lru_specs(tt, nt, Tp, lambda j: j)
    full = lambda a: pl.BlockSpec(a.shape, lambda b, j: (0,) * a.ndim)
    return pl.pallas_call(
        functools.partial(_lru_fwd_kernel, tt=tt, nt=nt),
        grid=(B, nt),
        in_specs=[blk, prev, nxt, full(cw), full(cb), full(wg), full(ba), full(bx), full(lam)],
        out_specs=blk,
        out_shape=jax.ShapeDtypeStruct((B, NSLAB, Tp, LANE), jnp.float32),
        scratch_shapes=[pltpu.VMEM((NSLAB, tt + 2 * SUB, LANE), jnp.float32),
                        pltpu.VMEM((1, LRU_WIDTH), jnp.float32)],
        compiler_params=pltpu.CompilerParams(
            dimension_semantics=("arbitrary", "arbitrary"), vmem_limit_bytes=VMEM_LIMIT),
        name="lru_fwd",
    )(xl, xl, xl, cw, cb, wg, ba, bx, lam)


def _lru_bwd(xl, cw, cb, wg, ba, bx, lam, hf, gl, gg, *, tt):
    B, _, Tp, _ = xl.shape
    W = LRU_WIDTH
    nt = Tp // tt
    rev = lambda j: nt - 1 - j
    blk, prev, nxt = _lru_specs(tt, nt, Tp, rev)
    row = pl.BlockSpec((None, tt, W), lambda b, j: (b, rev(j), 0))
    full = lambda a: pl.BlockSpec(a.shape, lambda b, j: (0,) * a.ndim)
    return pl.pallas_call(
        functools.partial(_lru_bwd_kernel, tt=tt, nt=nt),
        grid=(B, nt),
        in_specs=[blk, prev, nxt, full(cw), full(cb), full(wg), full(ba), full(bx), full(lam),
                  blk, row, full(gg)],
        out_specs=row,
        out_shape=jax.ShapeDtypeStruct((B, Tp, W), jnp.bfloat16),
        scratch_shapes=[pltpu.VMEM((NSLAB, tt + 2 * SUB, LANE), jnp.float32),
                        pltpu.VMEM((NSLAB, tt, LANE), jnp.float32),
                        pltpu.VMEM((1, LRU_WIDTH), jnp.float32)],
        compiler_params=pltpu.CompilerParams(
            dimension_semantics=("arbitrary", "arbitrary"), vmem_limit_bytes=VMEM_LIMIT),
        name="lru_bwd",
    )(xl, xl, xl, cw, cb, wg, ba, bx, lam, hf, gl, gg)


def _max_over_rows(s):
    rows = s.shape[0]
    while rows % (2 * SUB) == 0 and rows > 32 * SUB:
        fold = 4 if rows % (4 * SUB) == 0 else 2
        rows //= fold
        s = jnp.max(s.reshape(fold, rows, s.shape[1]), axis=0)
    return jnp.max(s, axis=0, keepdims=True)


def _attn_kernel(q_ref, k_ref, vt_ref, o_ref, *, tq, lead, span):
    heads = [slice(hh * HP, (hh + 1) * HP) for hh in range(2)]
    nkeys = k_ref.shape[0]
    nt = (((1,), (1,)), ((), ()))

    def scores(hh, lo, hi):
        return lax.dot_general(k_ref[lo:hi, heads[hh]], q_ref[:, heads[hh]], nt,
                               preferred_element_type=jnp.float32)

    def pv(hh, lo, hi, s, m):
        return jnp.dot(vt_ref[hh * VROWS:(hh + 1) * VROWS, lo:hi],
                       jnp.exp2(s - m).astype(jnp.bfloat16),
                       preferred_element_type=jnp.float32)

    def finish(a0, a1):
        l0, l1 = a0[V_DIM:V_DIM + 1, :], a1[V_DIM:V_DIM + 1, :]
        return jnp.concatenate([a0[:V_DIM] / l0, a1[:V_DIM] / l1], axis=0).T, l0, l1

    accs = []
    for hh in range(2):
        s = scores(hh, 0, lead)
        m = _max_over_rows(s)
        acc = pv(hh, 0, lead, s, m)
        for lo in range(lead, nkeys, span):
            acc = acc + pv(hh, lo, lo + span, scores(hh, lo, lo + span), m)
        accs.append(acc)
    out, l0, l1 = finish(*accs)
    lmin = jnp.min(jnp.minimum(l0, l1))
    lmax = jnp.max(jnp.maximum(l0, l1))
    safe = (lmin > 2.0 ** -100) & (lmax < 2.0 ** 100)

    @pl.when(safe)
    def _():
        o_ref[...] = out

    @pl.when(jnp.logical_not(safe))
    def _():
        exact = []
        for hh in range(2):
            s = scores(hh, 0, nkeys)
            exact.append(pv(hh, 0, nkeys, s, _max_over_rows(s)))
        o_ref[...] = finish(*exact)[0]


def _attn(q, k, vt, *, tq, lead, span):
    B, Tp, _ = q.shape
    npair = MLA_HEADS // 2
    return pl.pallas_call(
        functools.partial(_attn_kernel, tq=tq, lead=lead, span=span),
        grid=(B, npair, Tp // tq),
        in_specs=[pl.BlockSpec((None, tq, 2 * HP), lambda b, p, i: (b, i, p)),
                  pl.BlockSpec((None, Tp, 2 * HP), lambda b, p, i: (b, 0, p)),
                  pl.BlockSpec((None, 2 * VROWS, Tp), lambda b, p, i: (b, p, 0))],
        out_specs=pl.BlockSpec((None, tq, HP), lambda b, p, i: (b, i, p)),
        out_shape=jax.ShapeDtypeStruct((B, Tp, npair * HP), jnp.float32),
        compiler_params=pltpu.CompilerParams(
            dimension_semantics=("arbitrary", "arbitrary", "arbitrary"),
            vmem_limit_bytes=VMEM_LIMIT),
        name="attn",
    )(q, k, vt)


def _mix_ffn_kernel(*refs, tm, nt, ns):
    x_ref, xprev_ref, xnext_ref, mtail_ref = refs[:4]
    yl_refs, (ylprev_ref, ylnext_ref) = refs[4:4 + ns], refs[4 + ns:6 + ns]
    ym_refs, (ymprev_ref, ymnext_ref) = refs[6 + ns:6 + 2 * ns], refs[6 + 2 * ns:8 + 2 * ns]
    (mg_ref, wo_ref, g_ref, wg_ref, wu_ref, cw_ref, cb_ref, wd_ref, o_ref,
     gate_ref) = refs[8 + 2 * ns:]
    i = pl.program_id(1)
    f32, bf16 = jnp.float32, jnp.bfloat16
    H = 2 * SUB
    xw = jnp.concatenate([jnp.where(i == 0, mtail_ref[...], xprev_ref[...]), x_ref[...],
                          xnext_ref[...]], axis=0)
    ylw = jnp.concatenate([ylprev_ref[...].astype(f32)[H - SUB:]]
                          + [r[...].astype(f32) for r in yl_refs]
                          + [ylnext_ref[...].astype(f32)[:SUB]], axis=0).astype(bf16)
    ymw = jnp.concatenate([ymprev_ref[...]] + [r[...] for r in ym_refs] + [ymnext_ref[...]],
                          axis=0)
    y = jnp.concatenate([ylw, _rms(ymw, mg_ref[...]).astype(bf16)], axis=1)
    h1 = xw + jnp.dot(y, wo_ref[...], preferred_element_type=f32)
    r = lax.broadcasted_iota(jnp.int32, h1.shape, 0)
    h1 = jnp.where((i == nt - 1) & (r >= tm + SUB), 0.0, h1)
    hn = _rms(h1, g_ref[...]).astype(bf16)
    gate_ref[...] = jnp.dot(hn, wg_ref[...], preferred_element_type=f32)
    up = jnp.dot(hn[SUB:SUB + tm], wu_ref[...], preferred_element_type=f32)
    left = FFN_CONV // 2
    gc = cb_ref[...] + gate_ref[SUB - left:SUB - left + tm, :] * cw_ref[0:1, :]
    for k in range(1, FFN_CONV):
        gc = gc + gate_ref[SUB - left + k:SUB - left + k + tm, :] * cw_ref[k:k + 1, :]
    act = (gc * jax.nn.sigmoid(gc) * up).astype(bf16)
    o_ref[...] = h1[SUB:SUB + tm] + jnp.dot(act, wd_ref[...], preferred_element_type=f32)


def _mix_ffn(x, mtail, yl, ym, mg, wo, g, wg, wu, cw, cb, wd, *, tm):
    B, S, D = x.shape
    Tp = yl.shape[1]
    W = yl.shape[2]
    assert tm % X0 == 0 and S % tm == 0
    nt = S // tm
    ns = tm // X0
    H = 2 * SUB
    full = lambda a: pl.BlockSpec(a.shape, lambda b, i: (0,) * a.ndim)

    def halo(w, rows, total, first_row):
        prev = pl.BlockSpec((None, rows, w),
                            lambda b, i: (b, jnp.maximum(first_row(i) // rows - 1, 0), 0))
        nxt = pl.BlockSpec((None, rows, w),
                           lambda b, i: (b, jnp.minimum((first_row(i) + tm) // rows,
                                                        total // rows - 1), 0))
        return [prev, nxt]

    def mixer_tiles(w):
        return [pl.BlockSpec((None, X0, w), lambda b, i, s=s: (b, 1 + i * ns + s, 0))
                for s in range(ns)]

    seq_row = lambda i: i * tm
    pad_row = lambda i: X0 + i * tm
    in_specs = ([pl.BlockSpec((None, tm, D), lambda b, i: (b, i, 0))] + halo(D, SUB, S, seq_row)
                + [full(mtail)]
                + mixer_tiles(W) + halo(W, H, Tp, pad_row)
                + mixer_tiles(W) + halo(W, SUB, Tp, pad_row)
                + [full(a) for a in (mg, wo, g, wg, wu, cw, cb, wd)])
    return pl.pallas_call(
        functools.partial(_mix_ffn_kernel, tm=tm, nt=nt, ns=ns),
        grid=(B, nt),
        in_specs=in_specs,
        out_specs=pl.BlockSpec((None, tm, D), lambda b, i: (b, i, 0)),
        out_shape=jax.ShapeDtypeStruct((B, S, D), jnp.float32),
        scratch_shapes=[pltpu.VMEM((tm + 2 * SUB, D_FF), jnp.float32)],
        compiler_params=pltpu.CompilerParams(
            dimension_semantics=("arbitrary", "arbitrary"), vmem_limit_bytes=VMEM_LIMIT),
        name="mix_ffn",
    )(x, x, x, mtail, *([yl] * (ns + 2)), *([ym] * (ns + 2)), mg, wo, g, wg, wu, cw, cb, wd)


def _pad_heads(w, width, offset=0):
    K = w.shape[0]
    w = w.reshape(K, MLA_HEADS, width)
    w = jnp.pad(w, ((0, 0), (0, 0), (offset, HP - width - offset)))
    return w.reshape(K, MLA_HEADS * HP)


def _block_diag(w):
    H, Dh, _ = w.shape
    eye = jnp.eye(H, dtype=w.dtype)
    return (eye[:, None, :, None] * w[:, :, None, :]).reshape(H * Dh, H * Dh)


def _gate_weights(w_a, w_x):
    half = LRU_WIDTH // 2
    wa = _block_diag(w_a)
    wx = _block_diag(w_x)
    halves = [jnp.concatenate([wa[c * half:(c + 1) * half, c * half:(c + 1) * half],
                               wx[c * half:(c + 1) * half, c * half:(c + 1) * half]], axis=1)
              for c in range(2)]
    return jnp.stack(halves).astype(jnp.bfloat16)


def kernel(x, meta_tokens, norm_mix_g, w_in, conv_lru_w, conv_lru_b, lru_w_a, lru_b_a, lru_w_x,
           lru_b_x, lru_lambda, lru_gate_g, q_latent_g, w_uq, kv_latent_g, w_ukv, q_norm_g,
           k_norm_g, mla_out_g, w_out, norm_ffn_g, w_ffn_up, conv_ffn_w, conv_ffn_b, w_ffn_down):
    B, S, D = x.shape
    f32, bf16 = jnp.float32, jnp.bfloat16
    Tp = X0 + S
    l = 0
    r2 = lambda a: a.reshape(1, -1).astype(f32)

    head = jnp.concatenate([jnp.zeros((PAD, D), x.dtype), meta_tokens.astype(x.dtype)], axis=0)

    half = QK_ROPE // 2
    lanes = jnp.arange(HP)
    swap = jnp.where((lanes >= QK_NOPE) & (lanes < QK_NOPE + half), lanes + half,
                     jnp.where((lanes >= QK_NOPE + half) & (lanes < QK_DIM), lanes - half, lanes))
    swap_heads = (jnp.arange(MLA_HEADS)[:, None] * HP + swap[None, :]).reshape(-1)

    o = 2 * LRU_WIDTH + Q_LORA + KV_LORA
    kpe_w = jnp.pad(w_in[l][:, o:], ((0, 0), (QK_NOPE, LANE - QK_DIM)))
    win = jnp.concatenate([w_in[l][:, :o], kpe_w, kpe_w[:, swap]], axis=1).astype(bf16)
    wuq = _pad_heads(w_uq[l], QK_DIM).astype(bf16)
    wuqs = wuq[:, swap_heads]
    wkv = w_ukv[l].reshape(KV_LORA, MLA_HEADS, QK_NOPE + V_DIM)
    wuk = _pad_heads(wkv[:, :, :QK_NOPE].reshape(KV_LORA, -1), QK_NOPE).astype(bf16)
    wv = jnp.pad(wkv[:, :, QK_NOPE:], ((0, 0), (0, 0), (0, VROWS - V_DIM)))
    wuvt = wv.reshape(KV_LORA, MLA_HEADS * VROWS).T.astype(bf16)
    qg = jnp.pad(q_norm_g[l], (0, HP - QK_DIM)).reshape(1, HP)
    kg = jnp.pad(k_norm_g[l], (0, HP - QK_DIM)).reshape(1, HP)
    ones = jnp.ones((2 * HP, HP), bf16)

    pos = jnp.arange(Tp, dtype=f32) - PAD
    inv_freq = ROPE_THETA ** (-jnp.arange(0, QK_ROPE, 2, dtype=f32) / QK_ROPE)
    ang = pos[:, None] * inv_freq[None, :]
    cos, sin = jnp.cos(ang), jnp.sin(ang)
    ctab = jnp.concatenate([jnp.ones((Tp, QK_NOPE), f32), cos, cos,
                            jnp.ones((Tp, HP - QK_DIM), f32)], axis=1)
    stab = jnp.concatenate([jnp.zeros((Tp, QK_NOPE), f32), -sin, sin,
                            jnp.zeros((Tp, HP - QK_DIM), f32)], axis=1)

    xl, gl, q, k, vt = _inproj(x, head, r2(norm_mix_g[l]), win, r2(q_latent_g[l]), wuq, wuqs,
                              r2(kv_latent_g[l]), wuk, wuvt, qg, qg[:, swap], kg, kg[:, swap],
                              ones, ctab, stab, tm=X0)

    cw, cb = conv_lru_w[l], r2(conv_lru_b[l])
    wg = [_gate_weights(lru_w_a[l, d], lru_w_x[l, d]) for d in range(2)]
    hf = _lru_fwd(xl, cw, cb, wg[0], r2(lru_b_a[l, 0]), r2(lru_b_x[l, 0]), r2(lru_lambda[l, 0]),
                  tt=264)
    yl = _lru_bwd(xl, cw, cb, wg[1], r2(lru_b_a[l, 1]), r2(lru_b_x[l, 1]), r2(lru_lambda[l, 1]),
                  hf, gl, r2(lru_gate_g[l]), tt=264)

    ym = _attn(q, k, vt, tq=768, lead=768, span=768)

    wup = w_ffn_up[l].astype(bf16)
    return _mix_ffn(x, head[X0 - SUB:], yl, ym, r2(mla_out_g[l]), w_out[l].astype(bf16),
                    r2(norm_ffn_g[l]), wup[:, :D_FF], wup[:, D_FF:], conv_ffn_w[l],
                    r2(conv_ffn_b[l]), w_ffn_down[l].astype(bf16), tm=512)
```

```python
import functools
import math

import jax
import jax.numpy as jnp
from jax import lax
from jax.experimental import pallas as pl
from jax.experimental.pallas import tpu as pltpu

D_MODEL = 1024
N_META = 16
LRU_WIDTH = 512
LRU_HEADS = 8
LRU_HEAD_DIM = 64
LRU_CONV = 4
LRU_C = 8.0
MLA_HEADS = 8
QK_NOPE = 64
QK_ROPE = 32
QK_DIM = 96
V_DIM = 64
Q_LORA = 256
KV_LORA = 128
ROPE_THETA = 10000.0
D_FF = 2816
FFN_CONV = 3
EPS = 1e-6
NEG_INF = -1e30

LANE = 128
SUB = 8
X0 = 256
PAD = X0 - N_META
HP = LANE
MASK_LANE = QK_DIM
VMEM_LIMIT = 56 * 1024 * 1024


def _rms(x, g):
    return x * lax.rsqrt(jnp.mean(x * x, axis=-1, keepdims=True) + EPS) * g


def _bdot(a, b):
    return jnp.dot(a.astype(jnp.bfloat16), b, preferred_element_type=jnp.float32)


def _inproj_kernel(x_ref, head_ref, gmix_ref, win_ref, qlg_ref, wuq_ref, wuqs_ref, kvlg_ref,
                   wuk_ref, wuvt_ref, qg_ref, qgs_ref, kg_ref, kgs_ref, c_ref, s_ref,
                   xlru_ref, glru_ref, q_ref, k_ref, vt_ref, *, tm):
    i = pl.program_id(1)
    h = jnp.where(i == 0, head_ref[...], x_ref[...])
    hn = _rms(h, gmix_ref[...])
    proj = _bdot(hn, win_ref[...])
    for c in range(LRU_WIDTH // LANE):
        xlru_ref[c] = proj[:, c * LANE:(c + 1) * LANE]
    glru_ref[...] = proj[:, LRU_WIDTH:2 * LRU_WIDTH]
    o = 2 * LRU_WIDTH
    cq = proj[:, o:o + Q_LORA]
    ckv = proj[:, o + Q_LORA:o + Q_LORA + KV_LORA]
    o += Q_LORA + KV_LORA
    kpe = proj[:, o:o + HP]
    kpe_sw = proj[:, o + HP:]
    cqn = _rms(cq, qlg_ref[...]).astype(jnp.bfloat16)
    qraw = jnp.dot(cqn, wuq_ref[...], preferred_element_type=jnp.float32)
    qraw_sw = jnp.dot(cqn, wuqs_ref[...], preferred_element_type=jnp.float32)
    ckvn = _rms(ckv, kvlg_ref[...]).astype(jnp.bfloat16)
    kraw = jnp.dot(ckvn, wuk_ref[...], preferred_element_type=jnp.float32)
    vt = lax.dot_general(wuvt_ref[...], ckvn, (((1,), (1,)), ((), ())),
                         preferred_element_type=jnp.float32)
    vrow = lax.broadcasted_iota(jnp.int32, vt.shape, 0) % (2 * HP)
    vt_ref[...] = jnp.where((vrow == V_DIM) | (vrow == HP), 1.0, vt).astype(jnp.bfloat16)

    lane = lax.broadcasted_iota(jnp.int32, (tm, HP), 1)
    row = lax.broadcasted_iota(jnp.int32, (tm, HP), 0) + i * tm
    key_mask = jnp.where(row >= PAD, 0.0, NEG_INF)
    scale = math.log2(math.e) / math.sqrt(QK_DIM)
    cosv, sinv = c_ref[...], s_ref[...]
    q_cos = cosv * (qg_ref[...] * scale)
    q_sin = sinv * (qgs_ref[...] * scale)
    k_cos = cosv * kg_ref[...]
    k_sin_term = kpe_sw * (sinv * kgs_ref[...])

    def inv_rms(xb):
        return lax.rsqrt(jnp.sum(xb * xb, axis=-1, keepdims=True) * (1.0 / QK_DIM) + EPS)

    for hd in range(MLA_HEADS):
        sl = slice(hd * HP, (hd + 1) * HP)
        xq = qraw[:, sl]
        qb = (xq * q_cos + qraw_sw[:, sl] * q_sin) * inv_rms(xq)
        q_ref[:, sl] = jnp.where(lane == MASK_LANE, 1.0, qb).astype(jnp.bfloat16)
        xk = kraw[:, sl] + kpe
        kb = (xk * k_cos + k_sin_term) * inv_rms(xk)
        k_ref[:, sl] = jnp.where(lane == MASK_LANE, key_mask, kb).astype(jnp.bfloat16)


def _x_tile_spec(tm, D):
    return pl.BlockSpec((None, tm, D), lambda b, i: (b, jnp.maximum(i - 1, 0), 0))


def _inproj(x, head, gmix, win, qlg, wuq, wuqs, kvlg, wuk, wuv, qg, qgs, kg, kgs, ctab, stab,
            *, tm):
    B, S, D = x.shape
    assert tm == X0
    Tp = X0 + S
    nt = Tp // tm
    row = lambda w: pl.BlockSpec((None, tm, w), lambda b, i: (b, i, 0))
    full = lambda a: pl.BlockSpec(a.shape, lambda b, i: (0,) * a.ndim)
    tab = pl.BlockSpec((tm, HP), lambda b, i: (i, 0))
    f32, bf16 = jnp.float32, jnp.bfloat16
    return pl.pallas_call(
        functools.partial(_inproj_kernel, tm=tm),
        grid=(B, nt),
        in_specs=[_x_tile_spec(tm, D), full(head), full(gmix), full(win), full(qlg), full(wuq),
                  full(wuqs), full(kvlg), full(wuk), full(wuv), full(qg), full(qgs), full(kg),
                  full(kgs), tab, tab],
        out_specs=[pl.BlockSpec((None, LRU_WIDTH // LANE, tm, LANE), lambda b, i: (b, 0, i, 0)),
                   row(LRU_WIDTH), row(MLA_HEADS * HP), row(MLA_HEADS * HP),
                   pl.BlockSpec((None, MLA_HEADS * HP, tm), lambda b, i: (b, 0, i))],
        out_shape=[jax.ShapeDtypeStruct((B, LRU_WIDTH // LANE, Tp, LANE), f32),
                   jax.ShapeDtypeStruct((B, Tp, LRU_WIDTH), f32),
                   jax.ShapeDtypeStruct((B, Tp, MLA_HEADS * HP), bf16),
                   jax.ShapeDtypeStruct((B, Tp, MLA_HEADS * HP), bf16),
                   jax.ShapeDtypeStruct((B, MLA_HEADS * HP, Tp), bf16)],
        compiler_params=pltpu.CompilerParams(
            dimension_semantics=("arbitrary", "arbitrary"), vmem_limit_bytes=VMEM_LIMIT),
        name="inproj",
    )(x, head, gmix, win, qlg, wuq, wuqs, kvlg, wuk, wuv, qg, qgs, kg, kgs, ctab, stab)


NSLAB = LRU_WIDTH // LANE


def _lru_gates(x_ref, xprev_ref, xnext_ref, cw_ref, cb_ref, wg_ref, ba_ref, bx_ref, lam_ref,
               xw_ref, *, tt, first, last, t0):
    W = LRU_WIDTH
    seg = tt // SUB
    xw_ref[:, 0:SUB, :] = jnp.where(first, 0.0, xprev_ref[...])
    xw_ref[:, SUB:SUB + tt, :] = x_ref[...]
    xw_ref[:, SUB + tt:, :] = jnp.where(last, 0.0, xnext_ref[...])
    left = LRU_CONV // 2
    slabs = []
    for c in range(NSLAB):
        cs = slice(c * LANE, (c + 1) * LANE)
        pieces = []
        for j in range(seg):
            acc = cb_ref[:, cs]
            for k in range(LRU_CONV):
                tap = xw_ref[c, pl.ds(SUB - left + j + k, SUB, stride=seg), :]
                acc = acc + tap * cw_ref[k:k + 1, cs]
            pieces.append(acc)
        slabs.append(jnp.concatenate(pieces, axis=0))
    xc = jnp.concatenate(slabs, axis=1)
    xcb = xc.astype(jnp.bfloat16)
    half = W // 2
    g0 = jnp.dot(xcb[:, :half], wg_ref[0], preferred_element_type=jnp.float32)
    g1 = jnp.dot(xcb[:, half:], wg_ref[1], preferred_element_type=jnp.float32)
    ra = jnp.concatenate([g0[:, :half], g1[:, :half]], axis=1) + ba_ref[...]
    ri = jnp.concatenate([g0[:, half:], g1[:, half:]], axis=1) + bx_ref[...]
    r = jax.nn.sigmoid(ra)
    ig = jax.nn.sigmoid(ri)
    lam = lam_ref[...]
    sp = jnp.maximum(-lam, 0.0) + jnp.log(1.0 + jnp.exp(-jnp.abs(lam)))
    log_a = -LRU_C * r * sp
    a = jnp.exp(log_a)
    u = jnp.sqrt(1.0 - a * a) * (ig * xc)
    row = lax.broadcasted_iota(jnp.int32, (tt, W), 0)
    time = t0 + (row % SUB) * seg + row // SUB
    return a, jnp.where(time >= PAD, u, 0.0)


def _scan_block(a, u, h_ref, carry_ref, *, tt, reverse):
    W = LRU_WIDTH
    seg = tt // SUB
    order = range(seg - 1, -1, -1) if reverse else range(seg)
    h = jnp.zeros((SUB, W), jnp.float32)
    p = jnp.ones((SUB, W), jnp.float32)
    hs, ps = {}, {}
    for j in order:
        aj = a[j * SUB:(j + 1) * SUB]
        h = aj * h + u[j * SUB:(j + 1) * SUB]
        p = aj * p
        hs[j], ps[j] = h, p
    sub = lax.broadcasted_iota(jnp.int32, (SUB, W), 0)
    e, pf = h, p
    for d in (1, 2, 4):
        if reverse:
            keep, sh = sub < SUB - d, SUB - d
        else:
            keep, sh = sub >= d, d
        e_sh = jnp.where(keep, pltpu.roll(e, sh, axis=0), 0.0)
        pf_sh = jnp.where(keep, pltpu.roll(pf, sh, axis=0), 1.0)
        e = e + pf * e_sh
        pf = pf * pf_sh
    carry = carry_ref[...]
    leaving = e + pf * carry
    if reverse:
        entering = jnp.where(sub < SUB - 1, pltpu.roll(leaving, SUB - 1, axis=0), carry)
        carry_ref[...] = leaving[0:1]
    else:
        entering = jnp.where(sub >= 1, pltpu.roll(leaving, 1, axis=0), carry)
        carry_ref[...] = leaving[SUB - 1:SUB]
    for j in range(seg):
        hj = hs[j] + ps[j] * entering
        for c in range(NSLAB):
            h_ref[c, pl.ds(j, SUB, stride=seg), :] = hj[:, c * LANE:(c + 1) * LANE]


def _lru_fwd_kernel(x_ref, xprev_ref, xnext_ref, cw_ref, cb_ref, wg_ref, ba_ref, bx_ref, lam_ref,
                    hf_ref, xw_ref, carry_ref, *, tt, nt):
    i = pl.program_id(1)

    @pl.when(i == 0)
    def _():
        carry_ref[...] = jnp.zeros_like(carry_ref)

    a, u = _lru_gates(x_ref, xprev_ref, xnext_ref, cw_ref, cb_ref, wg_ref, ba_ref, bx_ref,
                      lam_ref, xw_ref, tt=tt, first=i == 0, last=i == nt - 1, t0=i * tt)
    _scan_block(a, u, hf_ref, carry_ref, tt=tt, reverse=False)


def _lru_bwd_kernel(x_ref, xprev_ref, xnext_ref, cw_ref, cb_ref, wg_ref, ba_ref, bx_ref, lam_ref,
                    hf_ref, gl_ref, gg_ref, y_ref, xw_ref, hb_ref, carry_ref, *, tt, nt):
    j = pl.program_id(1)
    i = nt - 1 - j

    @pl.when(j == 0)
    def _():
        carry_ref[...] = jnp.zeros_like(carry_ref)

    a, u = _lru_gates(x_ref, xprev_ref, xnext_ref, cw_ref, cb_ref, wg_ref, ba_ref, bx_ref,
                      lam_ref, xw_ref, tt=tt, first=i == 0, last=i == nt - 1, t0=i * tt)
    _scan_block(a, u, hb_ref, carry_ref, tt=tt, reverse=True)
    ys = [(hf_ref[c] + hb_ref[c]) * jax.nn.gelu(gl_ref[:, c * LANE:(c + 1) * LANE],
                                                approximate=True) for c in range(NSLAB)]
    y = jnp.concatenate(ys, axis=1)
    y_ref[...] = _rms(y, gg_ref[...]).astype(y_ref.dtype)


def _lru_specs(tt, nt, Tp, tmap):
    nsub = Tp // SUB
    per = tt // SUB
    slab = lambda rows, f: pl.BlockSpec((None, NSLAB, rows, LANE), lambda b, j: (b, 0, f(j), 0))
    blk = slab(tt, tmap)
    prev = slab(SUB, lambda j: jnp.maximum(tmap(j) * per - 1, 0))
    nxt = slab(SUB, lambda j: jnp.minimum((tmap(j) + 1) * per, nsub - 1))
    return blk, prev, nxt


def _lru_fwd(xl, cw, cb, wg, ba, bx, lam, *, tt):
    B, _, Tp, _ = xl.shape
    nt = Tp // tt
    blk, prev, nxt = _lru_specs(tt, nt, Tp, lambda j: j)
    full = lambda a: pl.BlockSpec(a.shape, lambda b, j: (0,) * a.ndim)
    return pl.pallas_call(
        functools.partial(_lru_fwd_kernel, tt=tt, nt=nt),
        grid=(B, nt),
        in_specs=[blk, prev, nxt, full(cw), full(cb), full(wg), full(ba), full(bx), full(lam)],
        out_specs=blk,
        out_shape=jax.ShapeDtypeStruct((B, NSLAB, Tp, LANE), jnp.float32),
        scratch_shapes=[pltpu.VMEM((NSLAB, tt + 2 * SUB, LANE), jnp.float32),
                        pltpu.VMEM((1, LRU_WIDTH), jnp.float32)],
        compiler_params=pltpu.CompilerParams(
            dimension_semantics=("arbitrary", "arbitrary"), vmem_limit_bytes=VMEM_LIMIT),
        name="lru_fwd",
    )(xl, xl, xl, cw, cb, wg, ba, bx, lam)


def _lru_bwd(xl, cw, cb, wg, ba, bx, lam, hf, gl, gg, *, tt):
    B, _, Tp, _ = xl.shape
    W = LRU_WIDTH
    nt = Tp // tt
    rev = lambda j: nt - 1 - j
    blk, prev, nxt = _lru_specs(tt, nt, Tp, rev)
    row = pl.BlockSpec((None, tt, W), lambda b, j: (b, rev(j), 0))
    full = lambda a: pl.BlockSpec(a.shape, lambda b, j: (0,) * a.ndim)
    return pl.pallas_call(
        functools.partial(_lru_bwd_kernel, tt=tt, nt=nt),
        grid=(B, nt),
        in_specs=[blk, prev, nxt, full(cw), full(cb), full(wg), full(ba), full(bx), full(lam),
                  blk, row, full(gg)],
        out_specs=row,
        out_shape=jax.ShapeDtypeStruct((B, Tp, W), jnp.bfloat16),
        scratch_shapes=[pltpu.VMEM((NSLAB, tt + 2 * SUB, LANE), jnp.float32),
                        pltpu.VMEM((NSLAB, tt, LANE), jnp.float32),
                        pltpu.VMEM((1, LRU_WIDTH), jnp.float32)],
        compiler_params=pltpu.CompilerParams(
            dimension_semantics=("arbitrary", "arbitrary"), vmem_limit_bytes=VMEM_LIMIT),
        name="lru_bwd",
    )(xl, xl, xl, cw, cb, wg, ba, bx, lam, hf, gl, gg)


def _max_over_rows(s):
    rows = s.shape[0]
    while rows % (2 * SUB) == 0 and rows > 32 * SUB:
        fold = 4 if rows % (4 * SUB) == 0 else 2
        rows //= fold
        s = jnp.max(s.reshape(fold, rows, s.shape[1]), axis=0)
    return jnp.max(s, axis=0, keepdims=True)


def _attn_kernel(q_ref, k_ref, vt_ref, o_ref, *, tq, lead, span):
    heads = [slice(hh * HP, (hh + 1) * HP) for hh in range(2)]
    nkeys = k_ref.shape[0]
    nt = (((1,), (1,)), ((), ()))
    rows = lax.broadcasted_iota(jnp.int32, (HP, tq), 0)

    def scores(hh, lo, hi):
        return lax.dot_general(k_ref[lo:hi, heads[hh]], q_ref[:, heads[hh]], nt,
                               preferred_element_type=jnp.float32)

    def pv(hh, lo, hi, s, m):
        return jnp.dot(vt_ref[heads[hh], lo:hi], jnp.exp2(s - m).astype(jnp.bfloat16),
                       preferred_element_type=jnp.float32)

    def finish(a0, a1):
        l0, l1 = a0[V_DIM:V_DIM + 1, :], a1[0:1, :]
        return jnp.where(rows < V_DIM, a0 / l0, a1 / l1).T, l0, l1

    accs = []
    for hh in range(2):
        s = scores(hh, 0, lead)
        m = _max_over_rows(s)
        acc = pv(hh, 0, lead, s, m)
        for lo in range(lead, nkeys, span):
            acc = acc + pv(hh, lo, lo + span, scores(hh, lo, lo + span), m)
        accs.append(acc)
    out, l0, l1 = finish(*accs)
    lmin = jnp.min(jnp.minimum(l0, l1))
    lmax = jnp.max(jnp.maximum(l0, l1))
    safe = (lmin > 2.0 ** -100) & (lmax < 2.0 ** 100)

    @pl.when(safe)
    def _():
        o_ref[...] = out

    @pl.when(jnp.logical_not(safe))
    def _():
        exact = []
        for hh in range(2):
            s = scores(hh, 0, nkeys)
            exact.append(pv(hh, 0, nkeys, s, _max_over_rows(s)))
        o_ref[...] = finish(*exact)[0]


def _attn(q, k, vt, *, tq, lead, span):
    B, Tp, _ = q.shape
    npair = MLA_HEADS // 2
    return pl.pallas_call(
        functools.partial(_attn_kernel, tq=tq, lead=lead, span=span),
        grid=(B, npair, Tp // tq),
        in_specs=[pl.BlockSpec((None, tq, 2 * HP), lambda b, p, i: (b, i, p)),
                  pl.BlockSpec((None, Tp, 2 * HP), lambda b, p, i: (b, 0, p)),
                  pl.BlockSpec((None, 2 * HP, Tp), lambda b, p, i: (b, p, 0))],
        out_specs=pl.BlockSpec((None, tq, HP), lambda b, p, i: (b, i, p)),
        out_shape=jax.ShapeDtypeStruct((B, Tp, npair * HP), jnp.float32),
        compiler_params=pltpu.CompilerParams(
            dimension_semantics=("arbitrary", "arbitrary", "arbitrary"),
            vmem_limit_bytes=VMEM_LIMIT),
        name="attn",
    )(q, k, vt)


def _mix_ffn_kernel(*refs, tm, nt, ns):
    x_ref, xprev_ref, xnext_ref, mtail_ref = refs[:4]
    yl_refs, (ylprev_ref, ylnext_ref) = refs[4:4 + ns], refs[4 + ns:6 + ns]
    ym_refs, (ymprev_ref, ymnext_ref) = refs[6 + ns:6 + 2 * ns], refs[6 + 2 * ns:8 + 2 * ns]
    (mg_ref, wo_ref, g_ref, wg_ref, wu_ref, cw_ref, cb_ref, wd_ref, o_ref,
     gate_ref) = refs[8 + 2 * ns:]
    i = pl.program_id(1)
    f32, bf16 = jnp.float32, jnp.bfloat16
    H = 2 * SUB
    xw = jnp.concatenate([jnp.where(i == 0, mtail_ref[...], xprev_ref[...]), x_ref[...],
                          xnext_ref[...]], axis=0)
    ylw = jnp.concatenate([ylprev_ref[...].astype(f32)[H - SUB:]]
                          + [r[...].astype(f32) for r in yl_refs]
                          + [ylnext_ref[...].astype(f32)[:SUB]], axis=0).astype(bf16)
    ymw = jnp.concatenate([ymprev_ref[...]] + [r[...] for r in ym_refs] + [ymnext_ref[...]],
                          axis=0)
    y = jnp.concatenate([ylw, _rms(ymw, mg_ref[...]).astype(bf16)], axis=1)
    h1 = xw + jnp.dot(y, wo_ref[...], preferred_element_type=f32)
    r = lax.broadcasted_iota(jnp.int32, h1.shape, 0)
    h1 = jnp.where((i == nt - 1) & (r >= tm + SUB), 0.0, h1)
    hn = _rms(h1, g_ref[...]).astype(bf16)
    gate_ref[...] = jnp.dot(hn, wg_ref[...], preferred_element_type=f32)
    up = jnp.dot(hn[SUB:SUB + tm], wu_ref[...], preferred_element_type=f32)
    left = FFN_CONV // 2
    gc = cb_ref[...] + gate_ref[SUB - left:SUB - left + tm, :] * cw_ref[0:1, :]
    for k in range(1, FFN_CONV):
        gc = gc + gate_ref[SUB - left + k:SUB - left + k + tm, :] * cw_ref[k:k + 1, :]
    act = (gc * jax.nn.sigmoid(gc) * up).astype(bf16)
    o_ref[...] = h1[SUB:SUB + tm] + jnp.dot(act, wd_ref[...], preferred_element_type=f32)


def _mix_ffn(x, mtail, yl, ym, mg, wo, g, wg, wu, cw, cb, wd, *, tm):
    B, S, D = x.shape
    Tp = yl.shape[1]
    W = yl.shape[2]
    assert tm % X0 == 0 and S % tm == 0
    nt = S // tm
    ns = tm // X0
    H = 2 * SUB
    full = lambda a: pl.BlockSpec(a.shape, lambda b, i: (0,) * a.ndim)

    def halo(w, rows, total, first_row):
        prev = pl.BlockSpec((None, rows, w),
                            lambda b, i: (b, jnp.maximum(first_row(i) // rows - 1, 0), 0))
        nxt = pl.BlockSpec((None, rows, w),
                           lambda b, i: (b, jnp.minimum((first_row(i) + tm) // rows,
                                                        total // rows - 1), 0))
        return [prev, nxt]

    def mixer_tiles(w):
        return [pl.BlockSpec((None, X0, w), lambda b, i, s=s: (b, 1 + i * ns + s, 0))
                for s in range(ns)]

    seq_row = lambda i: i * tm
    pad_row = lambda i: X0 + i * tm
    in_specs = ([pl.BlockSpec((None, tm, D), lambda b, i: (b, i, 0))] + halo(D, SUB, S, seq_row)
                + [full(mtail)]
                + mixer_tiles(W) + halo(W, H, Tp, pad_row)
                + mixer_tiles(W) + halo(W, SUB, Tp, pad_row)
                + [full(a) for a in (mg, wo, g, wg, wu, cw, cb, wd)])
    return pl.pallas_call(
        functools.partial(_mix_ffn_kernel, tm=tm, nt=nt, ns=ns),
        grid=(B, nt),
        in_specs=in_specs,
        out_specs=pl.BlockSpec((None, tm, D), lambda b, i: (b, i, 0)),
        out_shape=jax.ShapeDtypeStruct((B, S, D), jnp.float32),
        scratch_shapes=[pltpu.VMEM((tm + 2 * SUB, D_FF), jnp.float32)],
        compiler_params=pltpu.CompilerParams(
            dimension_semantics=("arbitrary", "arbitrary"), vmem_limit_bytes=VMEM_LIMIT),
        name="mix_ffn",
    )(x, x, x, mtail, *([yl] * (ns + 2)), *([ym] * (ns + 2)), mg, wo, g, wg, wu, cw, cb, wd)


def _pad_heads(w, width, offset=0):
    K = w.shape[0]
    w = w.reshape(K, MLA_HEADS, width)
    w = jnp.pad(w, ((0, 0), (0, 0), (offset, HP - width - offset)))
    return w.reshape(K, MLA_HEADS * HP)


def _block_diag(w):
    H, Dh, _ = w.shape
    eye = jnp.eye(H, dtype=w.dtype)
    return (eye[:, None, :, None] * w[:, :, None, :]).reshape(H * Dh, H * Dh)


def _gate_weights(w_a, w_x):
    half = LRU_WIDTH // 2
    wa = _block_diag(w_a)
    wx = _block_diag(w_x)
    halves = [jnp.concatenate([wa[c * half:(c + 1) * half, c * half:(c + 1) * half],
                               wx[c * half:(c + 1) * half, c * half:(c + 1) * half]], axis=1)
              for c in range(2)]
    return jnp.stack(halves).astype(jnp.bfloat16)


def kernel(x, meta_tokens, norm_mix_g, w_in, conv_lru_w, conv_lru_b, lru_w_a, lru_b_a, lru_w_x,
           lru_b_x, lru_lambda, lru_gate_g, q_latent_g, w_uq, kv_latent_g, w_ukv, q_norm_g,
           k_norm_g, mla_out_g, w_out, norm_ffn_g, w_ffn_up, conv_ffn_w, conv_ffn_b, w_ffn_down):
    B, S, D = x.shape
    f32, bf16 = jnp.float32, jnp.bfloat16
    Tp = X0 + S
    l = 0
    r2 = lambda a: a.reshape(1, -1).astype(f32)

    head = jnp.concatenate([jnp.zeros((PAD, D), x.dtype), meta_tokens.astype(x.dtype)], axis=0)

    half = QK_ROPE // 2
    lanes = jnp.arange(HP)
    swap = jnp.where((lanes >= QK_NOPE) & (lanes < QK_NOPE + half), lanes + half,
                     jnp.where((lanes >= QK_NOPE + half) & (lanes < QK_DIM), lanes - half, lanes))
    swap_heads = (jnp.arange(MLA_HEADS)[:, None] * HP + swap[None, :]).reshape(-1)

    o = 2 * LRU_WIDTH + Q_LORA + KV_LORA
    kpe_w = jnp.pad(w_in[l][:, o:], ((0, 0), (QK_NOPE, LANE - QK_DIM)))
    win = jnp.concatenate([w_in[l][:, :o], kpe_w, kpe_w[:, swap]], axis=1).astype(bf16)
    wuq = _pad_heads(w_uq[l], QK_DIM).astype(bf16)
    wuqs = wuq[:, swap_heads]
    wkv = w_ukv[l].reshape(KV_LORA, MLA_HEADS, QK_NOPE + V_DIM)
    wuk = _pad_heads(wkv[:, :, :QK_NOPE].reshape(KV_LORA, -1), QK_NOPE).astype(bf16)
    wv = jnp.pad(wkv[:, :, QK_NOPE:], ((0, 0), (0, 0), (0, HP - V_DIM)))
    wv = jnp.where((jnp.arange(MLA_HEADS) % 2 == 1)[None, :, None],
                   jnp.roll(wv, V_DIM, axis=2), wv)
    wuvt = wv.reshape(KV_LORA, MLA_HEADS * HP).T.astype(bf16)
    qg = jnp.pad(q_norm_g[l], (0, HP - QK_DIM)).reshape(1, HP)
    kg = jnp.pad(k_norm_g[l], (0, HP - QK_DIM)).reshape(1, HP)

    pos = jnp.arange(Tp, dtype=f32) - PAD
    inv_freq = ROPE_THETA ** (-jnp.arange(0, QK_ROPE, 2, dtype=f32) / QK_ROPE)
    ang = pos[:, None] * inv_freq[None, :]
    cos, sin = jnp.cos(ang), jnp.sin(ang)
    ctab = jnp.concatenate([jnp.ones((Tp, QK_NOPE), f32), cos, cos,
                            jnp.ones((Tp, HP - QK_DIM), f32)], axis=1)
    stab = jnp.concatenate([jnp.zeros((Tp, QK_NOPE), f32), -sin, sin,
                            jnp.zeros((Tp, HP - QK_DIM), f32)], axis=1)

    xl, gl, q, k, vt = _inproj(x, head, r2(norm_mix_g[l]), win, r2(q_latent_g[l]), wuq, wuqs,
                              r2(kv_latent_g[l]), wuk, wuvt, qg, qg[:, swap], kg, kg[:, swap],
                              ctab, stab, tm=X0)

    cw, cb = conv_lru_w[l], r2(conv_lru_b[l])
    wg = [_gate_weights(lru_w_a[l, d], lru_w_x[l, d]) for d in range(2)]
    hf = _lru_fwd(xl, cw, cb, wg[0], r2(lru_b_a[l, 0]), r2(lru_b_x[l, 0]), r2(lru_lambda[l, 0]),
                  tt=264)
    yl = _lru_bwd(xl, cw, cb, wg[1], r2(lru_b_a[l, 1]), r2(lru_b_x[l, 1]), r2(lru_lambda[l, 1]),
                  hf, gl, r2(lru_gate_g[l]), tt=264)

    ym = _attn(q, k, vt, tq=768, lead=768, span=768)

    wup = w_ffn_up[l].astype(bf16)
    return _mix_ffn(x, head[X0 - SUB:], yl, ym, r2(mla_out_g[l]), w_out[l].astype(bf16),
                    r2(norm_ffn_g[l]), wup[:, :D_FF], wup[:, D_FF:], conv_ffn_w[l],
                    r2(conv_ffn_b[l]), w_ffn_down[l].astype(bf16), tm=512)
```

```python
import functools
import math

import jax
import jax.numpy as jnp
from jax import lax
from jax.experimental import pallas as pl
from jax.experimental.pallas import tpu as pltpu

D_MODEL = 1024
N_META = 16
LRU_WIDTH = 512
LRU_HEADS = 8
LRU_HEAD_DIM = 64
LRU_CONV = 4
LRU_C = 8.0
MLA_HEADS = 8
QK_NOPE = 64
QK_ROPE = 32
QK_DIM = 96
V_DIM = 64
Q_LORA = 256
KV_LORA = 128
ROPE_THETA = 10000.0
D_FF = 2816
FFN_CONV = 3
EPS = 1e-6
NEG_INF = -1e30

LANE = 128
SUB = 8
X0 = 256
PAD = X0 - N_META
HP = LANE
MASK_LANE = QK_DIM
VMEM_LIMIT = 56 * 1024 * 1024


def _rms(x, g):
    return x * lax.rsqrt(jnp.mean(x * x, axis=-1, keepdims=True) + EPS) * g


def _gelu_tanh(x):
    c0 = math.sqrt(2.0 / math.pi)
    hx = 0.5 * x
    return hx + hx * jnp.tanh(x * (c0 + (c0 * 0.044715) * (x * x)))


def _bdot(a, b):
    return jnp.dot(a.astype(jnp.bfloat16), b, preferred_element_type=jnp.float32)


def _inproj_kernel(x_ref, head_ref, gmix_ref, win_ref, qlg_ref, wuq_ref, wuqs_ref, kvlg_ref,
                   wuk_ref, wuvt_ref, qg_ref, qgs_ref, kg_ref, kgs_ref, c_ref, s_ref,
                   xlru_ref, glru_ref, q_ref, k_ref, vt_ref, *, tm):
    i = pl.program_id(1)
    h = jnp.where(i == 0, head_ref[...], x_ref[...])
    hn = _rms(h, gmix_ref[...])
    proj = _bdot(hn, win_ref[...])
    for c in range(LRU_WIDTH // LANE):
        xlru_ref[c] = proj[:, c * LANE:(c + 1) * LANE]
    glru_ref[...] = proj[:, LRU_WIDTH:2 * LRU_WIDTH]
    o = 2 * LRU_WIDTH
    cq = proj[:, o:o + Q_LORA]
    ckv = proj[:, o + Q_LORA:o + Q_LORA + KV_LORA]
    o += Q_LORA + KV_LORA
    kpe = proj[:, o:o + HP]
    kpe_sw = proj[:, o + HP:]
    cqn = _rms(cq, qlg_ref[...]).astype(jnp.bfloat16)
    qraw = jnp.dot(cqn, wuq_ref[...], preferred_element_type=jnp.float32)
    qraw_sw = jnp.dot(cqn, wuqs_ref[...], preferred_element_type=jnp.float32)
    ckvn = _rms(ckv, kvlg_ref[...]).astype(jnp.bfloat16)
    kraw = jnp.dot(ckvn, wuk_ref[...], preferred_element_type=jnp.float32)
    vt = lax.dot_general(wuvt_ref[...], ckvn, (((1,), (1,)), ((), ())),
                         preferred_element_type=jnp.float32)
    vrow = lax.broadcasted_iota(jnp.int32, vt.shape, 0) % (2 * HP)
    vt_ref[...] = jnp.where((vrow == V_DIM) | (vrow == HP), 1.0, vt).astype(jnp.bfloat16)

    lane = lax.broadcasted_iota(jnp.int32, (tm, HP), 1)
    row = lax.broadcasted_iota(jnp.int32, (tm, HP), 0) + i * tm
    key_mask = jnp.where(row >= PAD, 0.0, NEG_INF)
    scale = math.log2(math.e) / math.sqrt(QK_DIM)
    cosv, sinv = c_ref[...], s_ref[...]
    q_cos = cosv * (qg_ref[...] * scale)
    q_sin = sinv * (qgs_ref[...] * scale)
    k_cos = cosv * kg_ref[...]
    k_sin_term = kpe_sw * (sinv * kgs_ref[...])

    def inv_rms(xb):
        return lax.rsqrt(jnp.sum(xb * xb, axis=-1, keepdims=True) * (1.0 / QK_DIM) + EPS)

    for hd in range(MLA_HEADS):
        sl = slice(hd * HP, (hd + 1) * HP)
        xq = qraw[:, sl]
        qb = (xq * q_cos + qraw_sw[:, sl] * q_sin) * inv_rms(xq)
        q_ref[:, sl] = jnp.where(lane == MASK_LANE, 1.0, qb).astype(jnp.bfloat16)
        xk = kraw[:, sl] + kpe
        kb = (xk * k_cos + k_sin_term) * inv_rms(xk)
        k_ref[:, sl] = jnp.where(lane == MASK_LANE, key_mask, kb).astype(jnp.bfloat16)


def _x_tile_spec(tm, D):
    return pl.BlockSpec((None, tm, D), lambda b, i: (b, jnp.maximum(i - 1, 0), 0))


def _inproj(x, head, gmix, win, qlg, wuq, wuqs, kvlg, wuk, wuv, qg, qgs, kg, kgs, ctab, stab,
            *, tm):
    B, S, D = x.shape
    assert tm == X0
    Tp = X0 + S
    nt = Tp // tm
    row = lambda w: pl.BlockSpec((None, tm, w), lambda b, i: (b, i, 0))
    full = lambda a: pl.BlockSpec(a.shape, lambda b, i: (0,) * a.ndim)
    tab = pl.BlockSpec((tm, HP), lambda b, i: (i, 0))
    f32, bf16 = jnp.float32, jnp.bfloat16
    return pl.pallas_call(
        functools.partial(_inproj_kernel, tm=tm),
        grid=(B, nt),
        in_specs=[_x_tile_spec(tm, D), full(head), full(gmix), full(win), full(qlg), full(wuq),
                  full(wuqs), full(kvlg), full(wuk), full(wuv), full(qg), full(qgs), full(kg),
                  full(kgs), tab, tab],
        out_specs=[pl.BlockSpec((None, LRU_WIDTH // LANE, tm, LANE), lambda b, i: (b, 0, i, 0)),
                   row(LRU_WIDTH), row(MLA_HEADS * HP), row(MLA_HEADS * HP),
                   pl.BlockSpec((None, MLA_HEADS * HP, tm), lambda b, i: (b, 0, i))],
        out_shape=[jax.ShapeDtypeStruct((B, LRU_WIDTH // LANE, Tp, LANE), f32),
                   jax.ShapeDtypeStruct((B, Tp, LRU_WIDTH), f32),
                   jax.ShapeDtypeStruct((B, Tp, MLA_HEADS * HP), bf16),
                   jax.ShapeDtypeStruct((B, Tp, MLA_HEADS * HP), bf16),
                   jax.ShapeDtypeStruct((B, MLA_HEADS * HP, Tp), bf16)],
        compiler_params=pltpu.CompilerParams(
            dimension_semantics=("arbitrary", "arbitrary"), vmem_limit_bytes=VMEM_LIMIT),
        name="inproj",
    )(x, head, gmix, win, qlg, wuq, wuqs, kvlg, wuk, wuv, qg, qgs, kg, kgs, ctab, stab)


NSLAB = LRU_WIDTH // LANE


def _lru_gates(x_ref, xprev_ref, xnext_ref, cw_ref, cb_ref, wg_ref, ba_ref, bx_ref, lam_ref,
               xw_ref, *, tt, first, last, t0):
    W = LRU_WIDTH
    seg = tt // SUB
    xw_ref[:, 0:SUB, :] = jnp.where(first, 0.0, xprev_ref[...])
    xw_ref[:, SUB:SUB + tt, :] = x_ref[...]
    xw_ref[:, SUB + tt:, :] = jnp.where(last, 0.0, xnext_ref[...])
    left = LRU_CONV // 2
    slabs = []
    for c in range(NSLAB):
        cs = slice(c * LANE, (c + 1) * LANE)
        pieces = []
        for j in range(seg):
            acc = cb_ref[:, cs]
            for k in range(LRU_CONV):
                tap = xw_ref[c, pl.ds(SUB - left + j + k, SUB, stride=seg), :]
                acc = acc + tap * cw_ref[k:k + 1, cs]
            pieces.append(acc)
        slabs.append(jnp.concatenate(pieces, axis=0))
    xc = jnp.concatenate(slabs, axis=1)
    xcb = xc.astype(jnp.bfloat16)
    half = W // 2
    g0 = jnp.dot(xcb[:, :half], wg_ref[0], preferred_element_type=jnp.float32)
    g1 = jnp.dot(xcb[:, half:], wg_ref[1], preferred_element_type=jnp.float32)
    ra = jnp.concatenate([g0[:, :half], g1[:, :half]], axis=1) + ba_ref[...]
    ri = jnp.concatenate([g0[:, half:], g1[:, half:]], axis=1) + bx_ref[...]
    r = jax.nn.sigmoid(ra)
    ig = jax.nn.sigmoid(ri)
    lam = lam_ref[...]
    sp = jnp.maximum(-lam, 0.0) + jnp.log(1.0 + jnp.exp(-jnp.abs(lam)))
    log_a = -LRU_C * r * sp
    a = jnp.exp(log_a)
    z = 1.0 - a * a
    u = jnp.where(z > 0.0, z * lax.rsqrt(z), 0.0) * (ig * xc)
    row = lax.broadcasted_iota(jnp.int32, (tt, W), 0)
    time = t0 + (row % SUB) * seg + row // SUB
    return a, jnp.where(time >= PAD, u, 0.0)


def _scan_block(a, u, h_ref, carry_ref, *, tt, reverse):
    W = LRU_WIDTH
    seg = tt // SUB
    order = range(seg - 1, -1, -1) if reverse else range(seg)
    h = jnp.zeros((SUB, W), jnp.float32)
    p = jnp.ones((SUB, W), jnp.float32)
    hs, ps = {}, {}
    for j in order:
        aj = a[j * SUB:(j + 1) * SUB]
        h = aj * h + u[j * SUB:(j + 1) * SUB]
        p = aj * p
        hs[j], ps[j] = h, p
    sub = lax.broadcasted_iota(jnp.int32, (SUB, W), 0)
    e, pf = h, p
    for d in (1, 2, 4):
        if reverse:
            keep, sh = sub < SUB - d, SUB - d
        else:
            keep, sh = sub >= d, d
        e_sh = jnp.where(keep, pltpu.roll(e, sh, axis=0), 0.0)
        pf_sh = jnp.where(keep, pltpu.roll(pf, sh, axis=0), 1.0)
        e = e + pf * e_sh
        pf = pf * pf_sh
    carry = carry_ref[...]
    leaving = e + pf * carry
    if reverse:
        entering = jnp.where(sub < SUB - 1, pltpu.roll(leaving, SUB - 1, axis=0), carry)
        carry_ref[...] = leaving[0:1]
    else:
        entering = jnp.where(sub >= 1, pltpu.roll(leaving, 1, axis=0), carry)
        carry_ref[...] = leaving[SUB - 1:SUB]
    for j in range(seg):
        hj = hs[j] + ps[j] * entering
        for c in range(NSLAB):
            h_ref[c, pl.ds(j, SUB, stride=seg), :] = hj[:, c * LANE:(c + 1) * LANE]


def _lru_fwd_kernel(x_ref, xprev_ref, xnext_ref, cw_ref, cb_ref, wg_ref, ba_ref, bx_ref, lam_ref,
                    hf_ref, xw_ref, carry_ref, *, tt, nt):
    i = pl.program_id(1)

    @pl.when(i == 0)
    def _():
        carry_ref[...] = jnp.zeros_like(carry_ref)

    a, u = _lru_gates(x_ref, xprev_ref, xnext_ref, cw_ref, cb_ref, wg_ref, ba_ref, bx_ref,
                      lam_ref, xw_ref, tt=tt, first=i == 0, last=i == nt - 1, t0=i * tt)
    _scan_block(a, u, hf_ref, carry_ref, tt=tt, reverse=False)


def _lru_bwd_kernel(x_ref, xprev_ref, xnext_ref, cw_ref, cb_ref, wg_ref, ba_ref, bx_ref, lam_ref,
                    hf_ref, gl_ref, gg_ref, y_ref, xw_ref, hb_ref, carry_ref, *, tt, nt):
    j = pl.program_id(1)
    i = nt - 1 - j

    @pl.when(j == 0)
    def _():
        carry_ref[...] = jnp.zeros_like(carry_ref)

    a, u = _lru_gates(x_ref, xprev_ref, xnext_ref, cw_ref, cb_ref, wg_ref, ba_ref, bx_ref,
                      lam_ref, xw_ref, tt=tt, first=i == 0, last=i == nt - 1, t0=i * tt)
    _scan_block(a, u, hb_ref, carry_ref, tt=tt, reverse=True)
    ys = [(hf_ref[c] + hb_ref[c]) * _gelu_tanh(gl_ref[:, c * LANE:(c + 1) * LANE])
          for c in range(NSLAB)]
    y = jnp.concatenate(ys, axis=1)
    y_ref[...] = _rms(y, gg_ref[...]).astype(y_ref.dtype)


def _lru_specs(tt, nt, Tp, tmap):
    nsub = Tp // SUB
    per = tt // SUB
    slab = lambda rows, f: pl.BlockSpec((None, NSLAB, rows, LANE), lambda b, j: (b, 0, f(j), 0))
    blk = slab(tt, tmap)
    prev = slab(SUB, lambda j: jnp.maximum(tmap(j) * per - 1, 0))
    nxt = slab(SUB, lambda j: jnp.minimum((tmap(j) + 1) * per, nsub - 1))
    return blk, prev, nxt


def _lru_fwd(xl, cw, cb, wg, ba, bx, lam, *, tt):
    B, _, Tp, _ = xl.shape
    nt = Tp // tt
    blk, prev, nxt = _lru_specs(tt, nt, Tp, lambda j: j)
    full = lambda a: pl.BlockSpec(a.shape, lambda b, j: (0,) * a.ndim)
    return pl.pallas_call(
        functools.partial(_lru_fwd_kernel, tt=tt, nt=nt),
        grid=(B, nt),
        in_specs=[blk, prev, nxt, full(cw), full(cb), full(wg), full(ba), full(bx), full(lam)],
        out_specs=blk,
        out_shape=jax.ShapeDtypeStruct((B, NSLAB, Tp, LANE), jnp.float32),
        scratch_shapes=[pltpu.VMEM((NSLAB, tt + 2 * SUB, LANE), jnp.float32),
                        pltpu.VMEM((1, LRU_WIDTH), jnp.float32)],
        compiler_params=pltpu.CompilerParams(
            dimension_semantics=("arbitrary", "arbitrary"), vmem_limit_bytes=VMEM_LIMIT),
        name="lru_fwd",
    )(xl, xl, xl, cw, cb, wg, ba, bx, lam)


def _lru_bwd(xl, cw, cb, wg, ba, bx, lam, hf, gl, gg, *, tt):
    B, _, Tp, _ = xl.shape
    W = LRU_WIDTH
    nt = Tp // tt
    rev = lambda j: nt - 1 - j
    blk, prev, nxt = _lru_specs(tt, nt, Tp, rev)
    row = pl.BlockSpec((None, tt, W), lambda b, j: (b, rev(j), 0))
    full = lambda a: pl.BlockSpec(a.shape, lambda b, j: (0,) * a.ndim)
    return pl.pallas_call(
        functools.partial(_lru_bwd_kernel, tt=tt, nt=nt),
        grid=(B, nt),
        in_specs=[blk, prev, nxt, full(cw), full(cb), full(wg), full(ba), full(bx), full(lam),
                  blk, row, full(gg)],
        out_specs=row,
        out_shape=jax.ShapeDtypeStruct((B, Tp, W), jnp.bfloat16),
        scratch_shapes=[pltpu.VMEM((NSLAB, tt + 2 * SUB, LANE), jnp.float32),
                        pltpu.VMEM((NSLAB, tt, LANE), jnp.float32),
                        pltpu.VMEM((1, LRU_WIDTH), jnp.float32)],
        compiler_params=pltpu.CompilerParams(
            dimension_semantics=("arbitrary", "arbitrary"), vmem_limit_bytes=VMEM_LIMIT),
        name="lru_bwd",
    )(xl, xl, xl, cw, cb, wg, ba, bx, lam, hf, gl, gg)


def _max_over_rows(s):
    rows = s.shape[0]
    while rows % (2 * SUB) == 0 and rows > 32 * SUB:
        fold = 4 if rows % (4 * SUB) == 0 else 2
        rows //= fold
        s = jnp.max(s.reshape(fold, rows, s.shape[1]), axis=0)
    return jnp.max(s, axis=0, keepdims=True)


def _attn_kernel(q_ref, k_ref, vt_ref, o_ref, *, tq, lead, span):
    heads = [slice(hh * HP, (hh + 1) * HP) for hh in range(2)]
    nkeys = k_ref.shape[0]
    nt = (((1,), (1,)), ((), ()))
    rows = lax.broadcasted_iota(jnp.int32, (HP, tq), 0)

    def scores(hh, lo, hi):
        return lax.dot_general(k_ref[lo:hi, heads[hh]], q_ref[:, heads[hh]], nt,
                               preferred_element_type=jnp.float32)

    def pv(hh, lo, hi, s, m):
        return jnp.dot(vt_ref[heads[hh], lo:hi], jnp.exp2(s - m).astype(jnp.bfloat16),
                       preferred_element_type=jnp.float32)

    def finish(a0, a1):
        l0, l1 = a0[V_DIM:V_DIM + 1, :], a1[0:1, :]
        return jnp.where(rows < V_DIM, a0 / l0, a1 / l1).T, l0, l1

    accs = []
    for hh in range(2):
        s = scores(hh, 0, lead)
        m = _max_over_rows(s)
        acc = pv(hh, 0, lead, s, m)
        for lo in range(lead, nkeys, span):
            acc = acc + pv(hh, lo, lo + span, scores(hh, lo, lo + span), m)
        accs.append(acc)
    out, l0, l1 = finish(*accs)
    lmin = jnp.min(jnp.minimum(l0, l1))
    lmax = jnp.max(jnp.maximum(l0, l1))
    safe = (lmin > 2.0 ** -100) & (lmax < 2.0 ** 100)

    @pl.when(safe)
    def _():
        o_ref[...] = out

    @pl.when(jnp.logical_not(safe))
    def _():
        exact = []
        for hh in range(2):
            s = scores(hh, 0, nkeys)
            exact.append(pv(hh, 0, nkeys, s, _max_over_rows(s)))
        o_ref[...] = finish(*exact)[0]


def _attn(q, k, vt, *, tq, lead, span):
    B, Tp, _ = q.shape
    npair = MLA_HEADS // 2
    return pl.pallas_call(
        functools.partial(_attn_kernel, tq=tq, lead=lead, span=span),
        grid=(B, npair, Tp // tq),
        in_specs=[pl.BlockSpec((None, tq, 2 * HP), lambda b, p, i: (b, i, p)),
                  pl.BlockSpec((None, Tp, 2 * HP), lambda b, p, i: (b, 0, p)),
                  pl.BlockSpec((None, 2 * HP, Tp), lambda b, p, i: (b, p, 0))],
        out_specs=pl.BlockSpec((None, tq, HP), lambda b, p, i: (b, i, p)),
        out_shape=jax.ShapeDtypeStruct((B, Tp, npair * HP), jnp.float32),
        compiler_params=pltpu.CompilerParams(
            dimension_semantics=("arbitrary", "arbitrary", "arbitrary"),
            vmem_limit_bytes=VMEM_LIMIT),
        name="attn",
    )(q, k, vt)


def _mix_ffn_kernel(*refs, tm, nt, ns):
    x_ref, xprev_ref, xnext_ref, mtail_ref = refs[:4]
    yl_refs, (ylprev_ref, ylnext_ref) = refs[4:4 + ns], refs[4 + ns:6 + ns]
    ym_refs, (ymprev_ref, ymnext_ref) = refs[6 + ns:6 + 2 * ns], refs[6 + 2 * ns:8 + 2 * ns]
    (mg_ref, wo_ref, g_ref, wg_ref, wu_ref, cw_ref, cb_ref, wd_ref, o_ref,
     gate_ref) = refs[8 + 2 * ns:]
    i = pl.program_id(1)
    f32, bf16 = jnp.float32, jnp.bfloat16
    H = 2 * SUB
    xw = jnp.concatenate([jnp.where(i == 0, mtail_ref[...], xprev_ref[...]), x_ref[...],
                          xnext_ref[...]], axis=0)
    ylw = jnp.concatenate([ylprev_ref[...].astype(f32)[H - SUB:]]
                          + [r[...].astype(f32) for r in yl_refs]
                          + [ylnext_ref[...].astype(f32)[:SUB]], axis=0).astype(bf16)
    ymw = jnp.concatenate([ymprev_ref[...]] + [r[...] for r in ym_refs] + [ymnext_ref[...]],
                          axis=0)
    y = jnp.concatenate([ylw, _rms(ymw, mg_ref[...]).astype(bf16)], axis=1)
    h1 = xw + jnp.dot(y, wo_ref[...], preferred_element_type=f32)
    r = lax.broadcasted_iota(jnp.int32, h1.shape, 0)
    h1 = jnp.where((i == nt - 1) & (r >= tm + SUB), 0.0, h1)
    hn = _rms(h1, g_ref[...]).astype(bf16)
    gate_ref[...] = jnp.dot(hn, wg_ref[...], preferred_element_type=f32)
    up = jnp.dot(hn[SUB:SUB + tm], wu_ref[...], preferred_element_type=f32)
    left = FFN_CONV // 2
    gc = cb_ref[...] + gate_ref[SUB - left:SUB - left + tm, :] * cw_ref[0:1, :]
    for k in range(1, FFN_CONV):
        gc = gc + gate_ref[SUB - left + k:SUB - left + k + tm, :] * cw_ref[k:k + 1, :]
    act = (gc * jax.nn.sigmoid(gc) * up).astype(bf16)
    o_ref[...] = h1[SUB:SUB + tm] + jnp.dot(act, wd_ref[...], preferred_element_type=f32)


def _mix_ffn(x, mtail, yl, ym, mg, wo, g, wg, wu, cw, cb, wd, *, tm):
    B, S, D = x.shape
    Tp = yl.shape[1]
    W = yl.shape[2]
    assert tm % X0 == 0 and S % tm == 0
    nt = S // tm
    ns = tm // X0
    H = 2 * SUB
    full = lambda a: pl.BlockSpec(a.shape, lambda b, i: (0,) * a.ndim)

    def halo(w, rows, total, first_row):
        prev = pl.BlockSpec((None, rows, w),
                            lambda b, i: (b, jnp.maximum(first_row(i) // rows - 1, 0), 0))
        nxt = pl.BlockSpec((None, rows, w),
                           lambda b, i: (b, jnp.minimum((first_row(i) + tm) // rows,
                                                        total // rows - 1), 0))
        return [prev, nxt]

    def mixer_tiles(w):
        return [pl.BlockSpec((None, X0, w), lambda b, i, s=s: (b, 1 + i * ns + s, 0))
                for s in range(ns)]

    seq_row = lambda i: i * tm
    pad_row = lambda i: X0 + i * tm
    in_specs = ([pl.BlockSpec((None, tm, D), lambda b, i: (b, i, 0))] + halo(D, SUB, S, seq_row)
                + [full(mtail)]
                + mixer_tiles(W) + halo(W, H, Tp, pad_row)
                + mixer_tiles(W) + halo(W, SUB, Tp, pad_row)
                + [full(a) for a in (mg, wo, g, wg, wu, cw, cb, wd)])
    return pl.pallas_call(
        functools.partial(_mix_ffn_kernel, tm=tm, nt=nt, ns=ns),
        grid=(B, nt),
        in_specs=in_specs,
        out_specs=pl.BlockSpec((None, tm, D), lambda b, i: (b, i, 0)),
        out_shape=jax.ShapeDtypeStruct((B, S, D), jnp.float32),
        scratch_shapes=[pltpu.VMEM((tm + 2 * SUB, D_FF), jnp.float32)],
        compiler_params=pltpu.CompilerParams(
            dimension_semantics=("arbitrary", "arbitrary"), vmem_limit_bytes=VMEM_LIMIT),
        name="mix_ffn",
    )(x, x, x, mtail, *([yl] * (ns + 2)), *([ym] * (ns + 2)), mg, wo, g, wg, wu, cw, cb, wd)


def _pad_heads(w, width, offset=0):
    K = w.shape[0]
    w = w.reshape(K, MLA_HEADS, width)
    w = jnp.pad(w, ((0, 0), (0, 0), (offset, HP - width - offset)))
    return w.reshape(K, MLA_HEADS * HP)


def _block_diag(w):
    H, Dh, _ = w.shape
    eye = jnp.eye(H, dtype=w.dtype)
    return (eye[:, None, :, None] * w[:, :, None, :]).reshape(H * Dh, H * Dh)


def _gate_weights(w_a, w_x):
    half = LRU_WIDTH // 2
    wa = _block_diag(w_a)
    wx = _block_diag(w_x)
    halves = [jnp.concatenate([wa[c * half:(c + 1) * half, c * half:(c + 1) * half],
                               wx[c * half:(c + 1) * half, c * half:(c + 1) * half]], axis=1)
              for c in range(2)]
    return jnp.stack(halves).astype(jnp.bfloat16)


def kernel(x, meta_tokens, norm_mix_g, w_in, conv_lru_w, conv_lru_b, lru_w_a, lru_b_a, lru_w_x,
           lru_b_x, lru_lambda, lru_gate_g, q_latent_g, w_uq, kv_latent_g, w_ukv, q_norm_g,
           k_norm_g, mla_out_g, w_out, norm_ffn_g, w_ffn_up, conv_ffn_w, conv_ffn_b, w_ffn_down):
    B, S, D = x.shape
    f32, bf16 = jnp.float32, jnp.bfloat16
    Tp = X0 + S
    l = 0
    r2 = lambda a: a.reshape(1, -1).astype(f32)

    head = jnp.concatenate([jnp.zeros((PAD, D), x.dtype), meta_tokens.astype(x.dtype)], axis=0)

    half = QK_ROPE // 2

    def swap(w):
        t = w.reshape(w.shape[:-1] + (-1, HP))
        t = jnp.concatenate([t[..., :QK_NOPE], t[..., QK_NOPE + half:QK_DIM],
                             t[..., QK_NOPE:QK_NOPE + half], t[..., QK_DIM:]], axis=-1)
        return t.reshape(w.shape)

    o = 2 * LRU_WIDTH + Q_LORA + KV_LORA
    kpe_w = jnp.pad(w_in[l][:, o:], ((0, 0), (QK_NOPE, LANE - QK_DIM)))
    win = jnp.concatenate([w_in[l][:, :o], kpe_w, swap(kpe_w)], axis=1).astype(bf16)
    wuq = _pad_heads(w_uq[l], QK_DIM).astype(bf16)
    wuqs = swap(wuq)
    wkv = w_ukv[l].reshape(KV_LORA, MLA_HEADS, QK_NOPE + V_DIM)
    wuk = _pad_heads(wkv[:, :, :QK_NOPE].reshape(KV_LORA, -1), QK_NOPE).astype(bf16)
    wv = jnp.pad(wkv[:, :, QK_NOPE:], ((0, 0), (0, 0), (0, HP - V_DIM)))
    wv = jnp.where((jnp.arange(MLA_HEADS) % 2 == 1)[None, :, None],
                   jnp.roll(wv, V_DIM, axis=2), wv)
    wuvt = wv.reshape(KV_LORA, MLA_HEADS * HP).T.astype(bf16)
    qg = jnp.pad(q_norm_g[l], (0, HP - QK_DIM)).reshape(1, HP)
    kg = jnp.pad(k_norm_g[l], (0, HP - QK_DIM)).reshape(1, HP)

    pos = jnp.arange(Tp, dtype=f32) - PAD
    inv_freq = ROPE_THETA ** (-jnp.arange(0, QK_ROPE, 2, dtype=f32) / QK_ROPE)
    ang = pos[:, None] * inv_freq[None, :]
    cos, sin = lax.optimization_barrier((jnp.cos(ang), jnp.sin(ang)))
    ctab = jnp.concatenate([jnp.ones((Tp, QK_NOPE), f32), cos, cos,
                            jnp.ones((Tp, HP - QK_DIM), f32)], axis=1)
    stab = jnp.concatenate([jnp.zeros((Tp, QK_NOPE), f32), -sin, sin,
                            jnp.zeros((Tp, HP - QK_DIM), f32)], axis=1)

    xl, gl, q, k, vt = _inproj(x, head, r2(norm_mix_g[l]), win, r2(q_latent_g[l]), wuq, wuqs,
                              r2(kv_latent_g[l]), wuk, wuvt, qg, swap(qg), kg, swap(kg),
                              ctab, stab, tm=X0)

    cw, cb = conv_lru_w[l], r2(conv_lru_b[l])
    wg = [_gate_weights(lru_w_a[l, d], lru_w_x[l, d]) for d in range(2)]
    hf = _lru_fwd(xl, cw, cb, wg[0], r2(lru_b_a[l, 0]), r2(lru_b_x[l, 0]), r2(lru_lambda[l, 0]),
                  tt=264)
    yl = _lru_bwd(xl, cw, cb, wg[1], r2(lru_b_a[l, 1]), r2(lru_b_x[l, 1]), r2(lru_lambda[l, 1]),
                  hf, gl, r2(lru_gate_g[l]), tt=264)

    ym = _attn(q, k, vt, tq=768, lead=768, span=768)

    wup = w_ffn_up[l].astype(bf16)
    return _mix_ffn(x, head[X0 - SUB:], yl, ym, r2(mla_out_g[l]), w_out[l].astype(bf16),
                    r2(norm_ffn_g[l]), wup[:, :D_FF], wup[:, D_FF:], conv_ffn_w[l],
                    r2(conv_ffn_b[l]), w_ffn_down[l].astype(bf16), tm=512)
```

```python
import functools
import math

import jax
import jax.numpy as jnp
from jax import lax
from jax.experimental import pallas as pl
from jax.experimental.pallas import tpu as pltpu

D_MODEL = 1024
N_META = 16
LRU_WIDTH = 512
LRU_HEADS = 8
LRU_HEAD_DIM = 64
LRU_CONV = 4
LRU_C = 8.0
MLA_HEADS = 8
QK_NOPE = 64
QK_ROPE = 32
QK_DIM = 96
V_DIM = 64
Q_LORA = 256
KV_LORA = 128
ROPE_THETA = 10000.0
D_FF = 2816
FFN_CONV = 3
EPS = 1e-6
NEG_INF = -1e30

LANE = 128
SUB = 8
X0 = 256
PAD = X0 - N_META
HP = LANE
MASK_LANE = QK_DIM
VMEM_LIMIT = 56 * 1024 * 1024


def _rms(x, g):
    return x * lax.rsqrt(jnp.mean(x * x, axis=-1, keepdims=True) + EPS) * g


def _gelu_tanh(x):
    c0 = math.sqrt(2.0 / math.pi)
    hx = 0.5 * x
    return hx + hx * jnp.tanh(x * (c0 + (c0 * 0.044715) * (x * x)))


def _bdot(a, b):
    return jnp.dot(a.astype(jnp.bfloat16), b, preferred_element_type=jnp.float32)


def _inproj_kernel(x_ref, head_ref, gmix_ref, win_ref, qlg_ref, wuq_ref, wuqs_ref, kvlg_ref,
                   wuk_ref, wuvt_ref, qg_ref, qgs_ref, kg_ref, kgs_ref, ca_ref, sa_ref, cb_ref, sb_ref,
                   xlru_ref, glru_ref, q_ref, k_ref, vt_ref, *, tm):
    i = pl.program_id(1)
    h = jnp.where(i == 0, head_ref[...], x_ref[...])
    hn = _rms(h, gmix_ref[...])
    proj = _bdot(hn, win_ref[...])
    for c in range(LRU_WIDTH // LANE):
        xlru_ref[c] = proj[:, c * LANE:(c + 1) * LANE]
    glru_ref[...] = proj[:, LRU_WIDTH:2 * LRU_WIDTH]
    o = 2 * LRU_WIDTH
    cq = proj[:, o:o + Q_LORA]
    ckv = proj[:, o + Q_LORA:o + Q_LORA + KV_LORA]
    o += Q_LORA + KV_LORA
    kpe = proj[:, o:o + HP]
    kpe_sw = proj[:, o + HP:]
    cqn = _rms(cq, qlg_ref[...]).astype(jnp.bfloat16)
    qraw = jnp.dot(cqn, wuq_ref[...], preferred_element_type=jnp.float32)
    qraw_sw = jnp.dot(cqn, wuqs_ref[...], preferred_element_type=jnp.float32)
    ckvn = _rms(ckv, kvlg_ref[...]).astype(jnp.bfloat16)
    kraw = jnp.dot(ckvn, wuk_ref[...], preferred_element_type=jnp.float32)
    vt = lax.dot_general(wuvt_ref[...], ckvn, (((1,), (1,)), ((), ())),
                         preferred_element_type=jnp.float32)
    vrow = lax.broadcasted_iota(jnp.int32, vt.shape, 0) % (2 * HP)
    vt_ref[...] = jnp.where((vrow == V_DIM) | (vrow == HP), 1.0, vt).astype(jnp.bfloat16)

    lane = lax.broadcasted_iota(jnp.int32, (tm, HP), 1)
    row = lax.broadcasted_iota(jnp.int32, (tm, HP), 0) + i * tm
    key_mask = jnp.where(row >= PAD, 0.0, NEG_INF)
    scale = math.log2(math.e) / math.sqrt(QK_DIM)
    ca, sa = ca_ref[pl.ds(i, 1), :], sa_ref[pl.ds(i, 1), :]
    cosv = ca * cb_ref[...] - sa * sb_ref[...]
    sinv = sa * cb_ref[...] + ca * sb_ref[...]
    q_cos = cosv * (qg_ref[...] * scale)
    q_sin = sinv * (qgs_ref[...] * scale)
    k_cos = cosv * kg_ref[...]
    k_sin_term = kpe_sw * (sinv * kgs_ref[...])

    def inv_rms(xb):
        return lax.rsqrt(jnp.sum(xb * xb, axis=-1, keepdims=True) * (1.0 / QK_DIM) + EPS)

    for hd in range(MLA_HEADS):
        sl = slice(hd * HP, (hd + 1) * HP)
        xq = qraw[:, sl]
        qb = (xq * q_cos + qraw_sw[:, sl] * q_sin) * inv_rms(xq)
        q_ref[:, sl] = jnp.where(lane == MASK_LANE, 1.0, qb).astype(jnp.bfloat16)
        xk = kraw[:, sl] + kpe
        kb = (xk * k_cos + k_sin_term) * inv_rms(xk)
        k_ref[:, sl] = jnp.where(lane == MASK_LANE, key_mask, kb).astype(jnp.bfloat16)


def _x_tile_spec(tm, D):
    return pl.BlockSpec((None, tm, D), lambda b, i: (b, jnp.maximum(i - 1, 0), 0))


def _inproj(x, head, gmix, win, qlg, wuq, wuqs, kvlg, wuk, wuv, qg, qgs, kg, kgs, ca, sa, cb, sb,
            *, tm):
    B, S, D = x.shape
    assert tm == X0
    Tp = X0 + S
    nt = Tp // tm
    row = lambda w: pl.BlockSpec((None, tm, w), lambda b, i: (b, i, 0))
    full = lambda a: pl.BlockSpec(a.shape, lambda b, i: (0,) * a.ndim)
    f32, bf16 = jnp.float32, jnp.bfloat16
    return pl.pallas_call(
        functools.partial(_inproj_kernel, tm=tm),
        grid=(B, nt),
        in_specs=[_x_tile_spec(tm, D), full(head), full(gmix), full(win), full(qlg), full(wuq),
                  full(wuqs), full(kvlg), full(wuk), full(wuv), full(qg), full(qgs), full(kg),
                  full(kgs), full(ca), full(sa), full(cb), full(sb)],
        out_specs=[pl.BlockSpec((None, LRU_WIDTH // LANE, tm, LANE), lambda b, i: (b, 0, i, 0)),
                   row(LRU_WIDTH), row(MLA_HEADS * HP), row(MLA_HEADS * HP),
                   pl.BlockSpec((None, MLA_HEADS * HP, tm), lambda b, i: (b, 0, i))],
        out_shape=[jax.ShapeDtypeStruct((B, LRU_WIDTH // LANE, Tp, LANE), f32),
                   jax.ShapeDtypeStruct((B, Tp, LRU_WIDTH), f32),
                   jax.ShapeDtypeStruct((B, Tp, MLA_HEADS * HP), bf16),
                   jax.ShapeDtypeStruct((B, Tp, MLA_HEADS * HP), bf16),
                   jax.ShapeDtypeStruct((B, MLA_HEADS * HP, Tp), bf16)],
        compiler_params=pltpu.CompilerParams(
            dimension_semantics=("arbitrary", "arbitrary"), vmem_limit_bytes=VMEM_LIMIT),
        name="inproj",
    )(x, head, gmix, win, qlg, wuq, wuqs, kvlg, wuk, wuv, qg, qgs, kg, kgs, ca, sa, cb, sb)


NSLAB = LRU_WIDTH // LANE


def _lru_gates(x_ref, xprev_ref, xnext_ref, cw_ref, cb_ref, wg_ref, ba_ref, bx_ref, lam_ref,
               xw_ref, *, tt, first, last, t0):
    W = LRU_WIDTH
    seg = tt // SUB
    xw_ref[:, 0:SUB, :] = jnp.where(first, 0.0, xprev_ref[...])
    xw_ref[:, SUB:SUB + tt, :] = x_ref[...]
    xw_ref[:, SUB + tt:, :] = jnp.where(last, 0.0, xnext_ref[...])
    left = LRU_CONV // 2
    slabs = []
    for c in range(NSLAB):
        cs = slice(c * LANE, (c + 1) * LANE)
        pieces = []
        for j in range(seg):
            acc = cb_ref[:, cs]
            for k in range(LRU_CONV):
                tap = xw_ref[c, pl.ds(SUB - left + j + k, SUB, stride=seg), :]
                acc = acc + tap * cw_ref[k:k + 1, cs]
            pieces.append(acc)
        slabs.append(jnp.concatenate(pieces, axis=0))
    xc = jnp.concatenate(slabs, axis=1)
    xcb = xc.astype(jnp.bfloat16)
    half = W // 2
    g0 = jnp.dot(xcb[:, :half], wg_ref[0], preferred_element_type=jnp.float32)
    g1 = jnp.dot(xcb[:, half:], wg_ref[1], preferred_element_type=jnp.float32)
    ra = jnp.concatenate([g0[:, :half], g1[:, :half]], axis=1) + ba_ref[...]
    ri = jnp.concatenate([g0[:, half:], g1[:, half:]], axis=1) + bx_ref[...]
    r = jax.nn.sigmoid(ra)
    ig = jax.nn.sigmoid(ri)
    lam = lam_ref[...]
    sp = jnp.maximum(-lam, 0.0) + jnp.log(1.0 + jnp.exp(-jnp.abs(lam)))
    log_a = -LRU_C * r * sp
    a = jnp.exp(log_a)
    z = 1.0 - a * a
    u = jnp.where(z > 0.0, z * lax.rsqrt(z), 0.0) * (ig * xc)
    row = lax.broadcasted_iota(jnp.int32, (tt, W), 0)
    time = t0 + (row % SUB) * seg + row // SUB
    return a, jnp.where(time >= PAD, u, 0.0)


def _scan_block(a, u, h_ref, carry_ref, *, tt, reverse):
    W = LRU_WIDTH
    seg = tt // SUB
    order = range(seg - 1, -1, -1) if reverse else range(seg)
    h = jnp.zeros((SUB, W), jnp.float32)
    p = jnp.ones((SUB, W), jnp.float32)
    hs, ps = {}, {}
    for j in order:
        aj = a[j * SUB:(j + 1) * SUB]
        h = aj * h + u[j * SUB:(j + 1) * SUB]
        p = aj * p
        hs[j], ps[j] = h, p
    sub = lax.broadcasted_iota(jnp.int32, (SUB, W), 0)
    e, pf = h, p
    for d in (1, 2, 4):
        if reverse:
            keep, sh = sub < SUB - d, SUB - d
        else:
            keep, sh = sub >= d, d
        e_sh = jnp.where(keep, pltpu.roll(e, sh, axis=0), 0.0)
        pf_sh = jnp.where(keep, pltpu.roll(pf, sh, axis=0), 1.0)
        e = e + pf * e_sh
        pf = pf * pf_sh
    carry = carry_ref[...]
    leaving = e + pf * carry
    if reverse:
        entering = jnp.where(sub < SUB - 1, pltpu.roll(leaving, SUB - 1, axis=0), carry)
        carry_ref[...] = leaving[0:1]
    else:
        entering = jnp.where(sub >= 1, pltpu.roll(leaving, 1, axis=0), carry)
        carry_ref[...] = leaving[SUB - 1:SUB]
    for j in range(seg):
        hj = hs[j] + ps[j] * entering
        for c in range(NSLAB):
            h_ref[c, pl.ds(j, SUB, stride=seg), :] = hj[:, c * LANE:(c + 1) * LANE]


def _lru_fwd_kernel(x_ref, xprev_ref, xnext_ref, cw_ref, cb_ref, wg_ref, ba_ref, bx_ref, lam_ref,
                    hf_ref, xw_ref, carry_ref, *, tt, nt):
    i = pl.program_id(1)

    @pl.when(i == 0)
    def _():
        carry_ref[...] = jnp.zeros_like(carry_ref)

    a, u = _lru_gates(x_ref, xprev_ref, xnext_ref, cw_ref, cb_ref, wg_ref, ba_ref, bx_ref,
                      lam_ref, xw_ref, tt=tt, first=i == 0, last=i == nt - 1, t0=i * tt)
    _scan_block(a, u, hf_ref, carry_ref, tt=tt, reverse=False)


def _lru_bwd_kernel(x_ref, xprev_ref, xnext_ref, cw_ref, cb_ref, wg_ref, ba_ref, bx_ref, lam_ref,
                    hf_ref, gl_ref, gg_ref, y_ref, xw_ref, hb_ref, carry_ref, *, tt, nt):
    j = pl.program_id(1)
    i = nt - 1 - j

    @pl.when(j == 0)
    def _():
        carry_ref[...] = jnp.zeros_like(carry_ref)

    a, u = _lru_gates(x_ref, xprev_ref, xnext_ref, cw_ref, cb_ref, wg_ref, ba_ref, bx_ref,
                      lam_ref, xw_ref, tt=tt, first=i == 0, last=i == nt - 1, t0=i * tt)
    _scan_block(a, u, hb_ref, carry_ref, tt=tt, reverse=True)
    ys = [(hf_ref[c] + hb_ref[c]) * _gelu_tanh(gl_ref[:, c * LANE:(c + 1) * LANE])
          for c in range(NSLAB)]
    y = jnp.concatenate(ys, axis=1)
    y_ref[...] = _rms(y, gg_ref[...]).astype(y_ref.dtype)


def _lru_specs(tt, nt, Tp, tmap):
    nsub = Tp // SUB
    per = tt // SUB
    slab = lambda rows, f: pl.BlockSpec((None, NSLAB, rows, LANE), lambda b, j: (b, 0, f(j), 0))
    blk = slab(tt, tmap)
    prev = slab(SUB, lambda j: jnp.maximum(tmap(j) * per - 1, 0))
    nxt = slab(SUB, lambda j: jnp.minimum((tmap(j) + 1) * per, nsub - 1))
    return blk, prev, nxt


def _lru_fwd(xl, cw, cb, wg, ba, bx, lam, *, tt):
    B, _, Tp, _ = xl.shape
    nt = Tp // tt
    blk, prev, nxt = _lru_specs(tt, nt, Tp, lambda j: j)
    full = lambda a: pl.BlockSpec(a.shape, lambda b, j: (0,) * a.ndim)
    return pl.pallas_call(
        functools.partial(_lru_fwd_kernel, tt=tt, nt=nt),
        grid=(B, nt),
        in_specs=[blk, prev, nxt, full(cw), full(cb), full(wg), full(ba), full(bx), full(lam)],
        out_specs=blk,
        out_shape=jax.ShapeDtypeStruct((B, NSLAB, Tp, LANE), jnp.float32),
        scratch_shapes=[pltpu.VMEM((NSLAB, tt + 2 * SUB, LANE), jnp.float32),
                        pltpu.VMEM((1, LRU_WIDTH), jnp.float32)],
        compiler_params=pltpu.CompilerParams(
            dimension_semantics=("arbitrary", "arbitrary"), vmem_limit_bytes=VMEM_LIMIT),
        name="lru_fwd",
    )(xl, xl, xl, cw, cb, wg, ba, bx, lam)


def _lru_bwd(xl, cw, cb, wg, ba, bx, lam, hf, gl, gg, *, tt):
    B, _, Tp, _ = xl.shape
    W = LRU_WIDTH
    nt = Tp // tt
    rev = lambda j: nt - 1 - j
    blk, prev, nxt = _lru_specs(tt, nt, Tp, rev)
    row = pl.BlockSpec((None, tt, W), lambda b, j: (b, rev(j), 0))
    full = lambda a: pl.BlockSpec(a.shape, lambda b, j: (0,) * a.ndim)
    return pl.pallas_call(
        functools.partial(_lru_bwd_kernel, tt=tt, nt=nt),
        grid=(B, nt),
        in_specs=[blk, prev, nxt, full(cw), full(cb), full(wg), full(ba), full(bx), full(lam),
                  blk, row, full(gg)],
        out_specs=row,
        out_shape=jax.ShapeDtypeStruct((B, Tp, W), jnp.bfloat16),
        scratch_shapes=[pltpu.VMEM((NSLAB, tt + 2 * SUB, LANE), jnp.float32),
                        pltpu.VMEM((NSLAB, tt, LANE), jnp.float32),
                        pltpu.VMEM((1, LRU_WIDTH), jnp.float32)],
        compiler_params=pltpu.CompilerParams(
            dimension_semantics=("arbitrary", "arbitrary"), vmem_limit_bytes=VMEM_LIMIT),
        name="lru_bwd",
    )(xl, xl, xl, cw, cb, wg, ba, bx, lam, hf, gl, gg)


def _max_over_rows(s):
    rows = s.shape[0]
    while rows % (2 * SUB) == 0 and rows > 32 * SUB:
        fold = 4 if rows % (4 * SUB) == 0 else 2
        rows //= fold
        s = jnp.max(s.reshape(fold, rows, s.shape[1]), axis=0)
    return jnp.max(s, axis=0, keepdims=True)


def _attn_kernel(q_ref, k_ref, vt_ref, o_ref, *, tq, lead, span):
    heads = [slice(hh * HP, (hh + 1) * HP) for hh in range(2)]
    nkeys = k_ref.shape[0]
    nt = (((1,), (1,)), ((), ()))
    rows = lax.broadcasted_iota(jnp.int32, (HP, tq), 0)

    def scores(hh, lo, hi):
        return lax.dot_general(k_ref[lo:hi, heads[hh]], q_ref[:, heads[hh]], nt,
                               preferred_element_type=jnp.float32)

    def pv(hh, lo, hi, s, m):
        return jnp.dot(vt_ref[heads[hh], lo:hi], jnp.exp2(s - m).astype(jnp.bfloat16),
                       preferred_element_type=jnp.float32)

    def finish(a0, a1):
        l0, l1 = a0[V_DIM:V_DIM + 1, :], a1[0:1, :]
        return jnp.where(rows < V_DIM, a0 / l0, a1 / l1).T, l0, l1

    accs = []
    for hh in range(2):
        s = scores(hh, 0, lead)
        m = _max_over_rows(s)
        acc = pv(hh, 0, lead, s, m)
        for lo in range(lead, nkeys, span):
            acc = acc + pv(hh, lo, lo + span, scores(hh, lo, lo + span), m)
        accs.append(acc)
    out, l0, l1 = finish(*accs)
    lmin = jnp.min(jnp.minimum(l0, l1))
    lmax = jnp.max(jnp.maximum(l0, l1))
    safe = (lmin > 2.0 ** -100) & (lmax < 2.0 ** 100)

    @pl.when(safe)
    def _():
        o_ref[...] = out

    @pl.when(jnp.logical_not(safe))
    def _():
        exact = []
        for hh in range(2):
            s = scores(hh, 0, nkeys)
            exact.append(pv(hh, 0, nkeys, s, _max_over_rows(s)))
        o_ref[...] = finish(*exact)[0]


def _attn(q, k, vt, *, tq, lead, span):
    B, Tp, _ = q.shape
    npair = MLA_HEADS // 2
    return pl.pallas_call(
        functools.partial(_attn_kernel, tq=tq, lead=lead, span=span),
        grid=(B, npair, Tp // tq),
        in_specs=[pl.BlockSpec((None, tq, 2 * HP), lambda b, p, i: (b, i, p)),
                  pl.BlockSpec((None, Tp, 2 * HP), lambda b, p, i: (b, 0, p)),
                  pl.BlockSpec((None, 2 * HP, Tp), lambda b, p, i: (b, p, 0))],
        out_specs=pl.BlockSpec((None, tq, HP), lambda b, p, i: (b, i, p)),
        out_shape=jax.ShapeDtypeStruct((B, Tp, npair * HP), jnp.float32),
        compiler_params=pltpu.CompilerParams(
            dimension_semantics=("arbitrary", "arbitrary", "arbitrary"),
            vmem_limit_bytes=VMEM_LIMIT),
        name="attn",
    )(q, k, vt)


def _mix_ffn_kernel(*refs, tm, nt, ns):
    x_ref, xprev_ref, xnext_ref, mtail_ref = refs[:4]
    yl_refs, (ylprev_ref, ylnext_ref) = refs[4:4 + ns], refs[4 + ns:6 + ns]
    ym_refs, (ymprev_ref, ymnext_ref) = refs[6 + ns:6 + 2 * ns], refs[6 + 2 * ns:8 + 2 * ns]
    (mg_ref, wo_ref, g_ref, wg_ref, wu_ref, cw_ref, cb_ref, wd_ref, o_ref,
     gate_ref) = refs[8 + 2 * ns:]
    i = pl.program_id(1)
    f32, bf16 = jnp.float32, jnp.bfloat16
    H = 2 * SUB
    xw = jnp.concatenate([jnp.where(i == 0, mtail_ref[...], xprev_ref[...]), x_ref[...],
                          xnext_ref[...]], axis=0)
    ylw = jnp.concatenate([ylprev_ref[...].astype(f32)[H - SUB:]]
                          + [r[...].astype(f32) for r in yl_refs]
                          + [ylnext_ref[...].astype(f32)[:SUB]], axis=0).astype(bf16)
    ymw = jnp.concatenate([ymprev_ref[...]] + [r[...] for r in ym_refs] + [ymnext_ref[...]],
                          axis=0)
    y = jnp.concatenate([ylw, _rms(ymw, mg_ref[...]).astype(bf16)], axis=1)
    h1 = xw + jnp.dot(y, wo_ref[...], preferred_element_type=f32)
    r = lax.broadcasted_iota(jnp.int32, h1.shape, 0)
    h1 = jnp.where((i == nt - 1) & (r >= tm + SUB), 0.0, h1)
    hn = _rms(h1, g_ref[...]).astype(bf16)
    gate_ref[...] = jnp.dot(hn, wg_ref[...], preferred_element_type=f32)
    up = jnp.dot(hn[SUB:SUB + tm], wu_ref[...], preferred_element_type=f32)
    left = FFN_CONV // 2
    gc = cb_ref[...] + gate_ref[SUB - left:SUB - left + tm, :] * cw_ref[0:1, :]
    for k in range(1, FFN_CONV):
        gc = gc + gate_ref[SUB - left + k:SUB - left + k + tm, :] * cw_ref[k:k + 1, :]
    act = (gc * jax.nn.sigmoid(gc) * up).astype(bf16)
    o_ref[...] = h1[SUB:SUB + tm] + jnp.dot(act, wd_ref[...], preferred_element_type=f32)


def _mix_ffn(x, mtail, yl, ym, mg, wo, g, wg, wu, cw, cb, wd, *, tm):
    B, S, D = x.shape
    Tp = yl.shape[1]
    W = yl.shape[2]
    assert tm % X0 == 0 and S % tm == 0
    nt = S // tm
    ns = tm // X0
    H = 2 * SUB
    full = lambda a: pl.BlockSpec(a.shape, lambda b, i: (0,) * a.ndim)

    def halo(w, rows, total, first_row):
        prev = pl.BlockSpec((None, rows, w),
                            lambda b, i: (b, jnp.maximum(first_row(i) // rows - 1, 0), 0))
        nxt = pl.BlockSpec((None, rows, w),
                           lambda b, i: (b, jnp.minimum((first_row(i) + tm) // rows,
                                                        total // rows - 1), 0))
        return [prev, nxt]

    def mixer_tiles(w):
        return [pl.BlockSpec((None, X0, w), lambda b, i, s=s: (b, 1 + i * ns + s, 0))
                for s in range(ns)]

    seq_row = lambda i: i * tm
    pad_row = lambda i: X0 + i * tm
    in_specs = ([pl.BlockSpec((None, tm, D), lambda b, i: (b, i, 0))] + halo(D, SUB, S, seq_row)
                + [full(mtail)]
                + mixer_tiles(W) + halo(W, H, Tp, pad_row)
                + mixer_tiles(W) + halo(W, SUB, Tp, pad_row)
                + [full(a) for a in (mg, wo, g, wg, wu, cw, cb, wd)])
    return pl.pallas_call(
        functools.partial(_mix_ffn_kernel, tm=tm, nt=nt, ns=ns),
        grid=(B, nt),
        in_specs=in_specs,
        out_specs=pl.BlockSpec((None, tm, D), lambda b, i: (b, i, 0)),
        out_shape=jax.ShapeDtypeStruct((B, S, D), jnp.float32),
        scratch_shapes=[pltpu.VMEM((tm + 2 * SUB, D_FF), jnp.float32)],
        compiler_params=pltpu.CompilerParams(
            dimension_semantics=("arbitrary", "arbitrary"), vmem_limit_bytes=VMEM_LIMIT),
        name="mix_ffn",
    )(x, x, x, mtail, *([yl] * (ns + 2)), *([ym] * (ns + 2)), mg, wo, g, wg, wu, cw, cb, wd)


def _pad_heads(w, width, offset=0):
    K = w.shape[0]
    w = w.reshape(K, MLA_HEADS, width)
    w = jnp.pad(w, ((0, 0), (0, 0), (offset, HP - width - offset)))
    return w.reshape(K, MLA_HEADS * HP)


def _block_diag(w):
    H, Dh, _ = w.shape
    eye = jnp.eye(H, dtype=w.dtype)
    return (eye[:, None, :, None] * w[:, :, None, :]).reshape(H * Dh, H * Dh)


def _gate_weights(w_a, w_x):
    half = LRU_WIDTH // 2
    wa = _block_diag(w_a)
    wx = _block_diag(w_x)
    halves = [jnp.concatenate([wa[c * half:(c + 1) * half, c * half:(c + 1) * half],
                               wx[c * half:(c + 1) * half, c * half:(c + 1) * half]], axis=1)
              for c in range(2)]
    return jnp.stack(halves).astype(jnp.bfloat16)


def kernel(x, meta_tokens, norm_mix_g, w_in, conv_lru_w, conv_lru_b, lru_w_a, lru_b_a, lru_w_x,
           lru_b_x, lru_lambda, lru_gate_g, q_latent_g, w_uq, kv_latent_g, w_ukv, q_norm_g,
           k_norm_g, mla_out_g, w_out, norm_ffn_g, w_ffn_up, conv_ffn_w, conv_ffn_b, w_ffn_down):
    B, S, D = x.shape
    f32, bf16 = jnp.float32, jnp.bfloat16
    Tp = X0 + S
    l = 0
    r2 = lambda a: a.reshape(1, -1).astype(f32)

    head = jnp.concatenate([jnp.zeros((PAD, D), x.dtype), meta_tokens.astype(x.dtype)], axis=0)

    half = QK_ROPE // 2

    def swap(w):
        t = w.reshape(w.shape[:-1] + (-1, HP))
        t = jnp.concatenate([t[..., :QK_NOPE], t[..., QK_NOPE + half:QK_DIM],
                             t[..., QK_NOPE:QK_NOPE + half], t[..., QK_DIM:]], axis=-1)
        return t.reshape(w.shape)

    o = 2 * LRU_WIDTH + Q_LORA + KV_LORA
    kpe_w = jnp.pad(w_in[l][:, o:], ((0, 0), (QK_NOPE, LANE - QK_DIM)))
    win = jnp.concatenate([w_in[l][:, :o], kpe_w, swap(kpe_w)], axis=1).astype(bf16)
    wuq = _pad_heads(w_uq[l], QK_DIM).astype(bf16)
    wuqs = swap(wuq)
    wkv = w_ukv[l].reshape(KV_LORA, MLA_HEADS, QK_NOPE + V_DIM)
    wuk = _pad_heads(wkv[:, :, :QK_NOPE].reshape(KV_LORA, -1), QK_NOPE).astype(bf16)
    wv = jnp.pad(wkv[:, :, QK_NOPE:], ((0, 0), (0, 0), (0, HP - V_DIM)))
    wv = jnp.where((jnp.arange(MLA_HEADS) % 2 == 1)[None, :, None],
                   jnp.roll(wv, V_DIM, axis=2), wv)
    wuvt = wv.reshape(KV_LORA, MLA_HEADS * HP).T.astype(bf16)
    qg = jnp.pad(q_norm_g[l], (0, HP - QK_DIM)).reshape(1, HP)
    kg = jnp.pad(k_norm_g[l], (0, HP - QK_DIM)).reshape(1, HP)

    inv_freq = ROPE_THETA ** (-jnp.arange(0, QK_ROPE, 2, dtype=f32) / QK_ROPE)

    def lane_tables(pos):
        ang = pos[:, None] * inv_freq[None, :]
        n = pos.shape[0]
        cos, sin = jnp.cos(ang), jnp.sin(ang)
        ctab = jnp.concatenate([jnp.ones((n, QK_NOPE), f32), cos, cos,
                                jnp.ones((n, HP - QK_DIM), f32)], axis=1)
        stab = jnp.concatenate([jnp.zeros((n, QK_NOPE), f32), -sin, sin,
                                jnp.zeros((n, HP - QK_DIM), f32)], axis=1)
        return ctab, stab

    rope = lane_tables(jnp.arange(Tp // X0, dtype=f32) * X0) + lane_tables(
        jnp.arange(X0, dtype=f32) - PAD)

    xl, gl, q, k, vt = _inproj(x, head, r2(norm_mix_g[l]), win, r2(q_latent_g[l]), wuq, wuqs,
                              r2(kv_latent_g[l]), wuk, wuvt, qg, swap(qg), kg, swap(kg),
                              *rope, tm=X0)

    cw, cb = conv_lru_w[l], r2(conv_lru_b[l])
    wg = [_gate_weights(lru_w_a[l, d], lru_w_x[l, d]) for d in range(2)]
    hf = _lru_fwd(xl, cw, cb, wg[0], r2(lru_b_a[l, 0]), r2(lru_b_x[l, 0]), r2(lru_lambda[l, 0]),
                  tt=264)
    yl = _lru_bwd(xl, cw, cb, wg[1], r2(lru_b_a[l, 1]), r2(lru_b_x[l, 1]), r2(lru_lambda[l, 1]),
                  hf, gl, r2(lru_gate_g[l]), tt=264)

    ym = _attn(q, k, vt, tq=768, lead=768, span=768)

    wup = w_ffn_up[l].astype(bf16)
    return _mix_ffn(x, head[X0 - SUB:], yl, ym, r2(mla_out_g[l]), w_out[l].astype(bf16),
                    r2(norm_ffn_g[l]), wup[:, :D_FF], wup[:, D_FF:], conv_ffn_w[l],
                    r2(conv_ffn_b[l]), w_ffn_down[l].astype(bf16), tm=512)
```

```python
import functools
import math

import jax
import jax.numpy as jnp
from jax import lax
from jax.experimental import pallas as pl
from jax.experimental.pallas import tpu as pltpu

D_MODEL = 1024
N_META = 16
LRU_WIDTH = 512
LRU_HEADS = 8
LRU_HEAD_DIM = 64
LRU_CONV = 4
LRU_C = 8.0
MLA_HEADS = 8
QK_NOPE = 64
QK_ROPE = 32
QK_DIM = 96
V_DIM = 64
Q_LORA = 256
KV_LORA = 128
ROPE_THETA = 10000.0
D_FF = 2816
FFN_CONV = 3
EPS = 1e-6
NEG_INF = -1e30

LANE = 128
SUB = 8
X0 = 256
PAD = X0 - N_META
HP = LANE
MASK_LANE = QK_DIM
VMEM_LIMIT = 56 * 1024 * 1024


def _rms(x, g):
    return x * lax.rsqrt(jnp.mean(x * x, axis=-1, keepdims=True) + EPS) * g


def _gelu_tanh(x):
    c0 = math.sqrt(2.0 / math.pi)
    hx = 0.5 * x
    return hx + hx * jnp.tanh(x * (c0 + (c0 * 0.044715) * (x * x)))


def _bdot(a, b):
    return jnp.dot(a.astype(jnp.bfloat16), b, preferred_element_type=jnp.float32)


def _inproj_kernel(x_ref, head_ref, gmix_ref, win_ref, qlg_ref, wuq_ref, wuqs_ref, kvlg_ref,
                   wuk_ref, wuvt_ref, qg_ref, qgs_ref, kg_ref, kgs_ref, ca_ref, sa_ref, cb_ref, sb_ref,
                   xlru_ref, glru_ref, q_ref, k_ref, vt_ref, *, tm):
    i = pl.program_id(1)
    h = jnp.where(i == 0, head_ref[...], x_ref[...])
    hn = _rms(h, gmix_ref[...])
    proj = _bdot(hn, win_ref[...])
    for c in range(LRU_WIDTH // LANE):
        xlru_ref[c] = proj[:, c * LANE:(c + 1) * LANE]
    glru_ref[...] = proj[:, LRU_WIDTH:2 * LRU_WIDTH]
    o = 2 * LRU_WIDTH
    cq = proj[:, o:o + Q_LORA]
    ckv = proj[:, o + Q_LORA:o + Q_LORA + KV_LORA]
    o += Q_LORA + KV_LORA
    kpe = proj[:, o:o + HP]
    kpe_sw = proj[:, o + HP:]
    cqn = _rms(cq, qlg_ref[...]).astype(jnp.bfloat16)
    qraw = jnp.dot(cqn, wuq_ref[...], preferred_element_type=jnp.float32)
    qraw_sw = jnp.dot(cqn, wuqs_ref[...], preferred_element_type=jnp.float32)
    ckvn = _rms(ckv, kvlg_ref[...]).astype(jnp.bfloat16)
    kraw = jnp.dot(ckvn, wuk_ref[...], preferred_element_type=jnp.float32)
    vt = lax.dot_general(wuvt_ref[...], ckvn, (((1,), (1,)), ((), ())),
                         preferred_element_type=jnp.float32)
    vrow = lax.broadcasted_iota(jnp.int32, vt.shape, 0) % (2 * HP)
    vt_ref[...] = jnp.where((vrow == V_DIM) | (vrow == HP), 1.0, vt).astype(jnp.bfloat16)

    lane = lax.broadcasted_iota(jnp.int32, (tm, HP), 1)
    row = lax.broadcasted_iota(jnp.int32, (tm, HP), 0) + i * tm
    key_mask = jnp.where(row >= PAD, 0.0, NEG_INF)
    scale = math.log2(math.e) / math.sqrt(QK_DIM)
    ca, sa = ca_ref[pl.ds(i, 1), :], sa_ref[pl.ds(i, 1), :]
    cosv = ca * cb_ref[...] - sa * sb_ref[...]
    sinv = sa * cb_ref[...] + ca * sb_ref[...]
    q_cos = cosv * (qg_ref[...] * scale)
    q_sin = sinv * (qgs_ref[...] * scale)
    k_cos = cosv * kg_ref[...]
    k_sin_term = kpe_sw * (sinv * kgs_ref[...])

    def inv_rms(xb):
        return lax.rsqrt(jnp.sum(xb * xb, axis=-1, keepdims=True) * (1.0 / QK_DIM) + EPS)

    for hd in range(MLA_HEADS):
        sl = slice(hd * HP, (hd + 1) * HP)
        xq = qraw[:, sl]
        qb = (xq * q_cos + qraw_sw[:, sl] * q_sin) * inv_rms(xq)
        q_ref[:, sl] = jnp.where(lane == MASK_LANE, 1.0, qb).astype(jnp.bfloat16)
        xk = kraw[:, sl] + kpe
        kb = (xk * k_cos + k_sin_term) * inv_rms(xk)
        k_ref[:, sl] = jnp.where(lane == MASK_LANE, key_mask, kb).astype(jnp.bfloat16)


def _x_tile_spec(tm, D):
    return pl.BlockSpec((None, tm, D), lambda b, i: (b, jnp.maximum(i - 1, 0), 0))


def _inproj(x, head, gmix, win, qlg, wuq, wuqs, kvlg, wuk, wuv, qg, qgs, kg, kgs, ca, sa, cb, sb,
            *, tm):
    B, S, D = x.shape
    assert tm == X0
    Tp = X0 + S
    nt = Tp // tm
    row = lambda w: pl.BlockSpec((None, tm, w), lambda b, i: (b, i, 0))
    full = lambda a: pl.BlockSpec(a.shape, lambda b, i: (0,) * a.ndim)
    f32, bf16 = jnp.float32, jnp.bfloat16
    return pl.pallas_call(
        functools.partial(_inproj_kernel, tm=tm),
        grid=(B, nt),
        in_specs=[_x_tile_spec(tm, D), full(head), full(gmix), full(win), full(qlg), full(wuq),
                  full(wuqs), full(kvlg), full(wuk), full(wuv), full(qg), full(qgs), full(kg),
                  full(kgs), full(ca), full(sa), full(cb), full(sb)],
        out_specs=[pl.BlockSpec((None, LRU_WIDTH // LANE, tm, LANE), lambda b, i: (b, 0, i, 0)),
                   row(LRU_WIDTH), row(MLA_HEADS * HP), row(MLA_HEADS * HP),
                   pl.BlockSpec((None, MLA_HEADS * HP, tm), lambda b, i: (b, 0, i))],
        out_shape=[jax.ShapeDtypeStruct((B, LRU_WIDTH // LANE, Tp, LANE), f32),
                   jax.ShapeDtypeStruct((B, Tp, LRU_WIDTH), f32),
                   jax.ShapeDtypeStruct((B, Tp, MLA_HEADS * HP), bf16),
                   jax.ShapeDtypeStruct((B, Tp, MLA_HEADS * HP), bf16),
                   jax.ShapeDtypeStruct((B, MLA_HEADS * HP, Tp), bf16)],
        compiler_params=pltpu.CompilerParams(
            dimension_semantics=("arbitrary", "arbitrary"), vmem_limit_bytes=VMEM_LIMIT),
        name="inproj",
    )(x, head, gmix, win, qlg, wuq, wuqs, kvlg, wuk, wuv, qg, qgs, kg, kgs, ca, sa, cb, sb)


NSLAB = LRU_WIDTH // LANE


def _lru_gates(x_ref, xprev_ref, xnext_ref, cw_ref, cb_ref, wg_ref, ba_ref, bx_ref, lam_ref,
               xw_ref, *, tt, first, last, t0):
    W = LRU_WIDTH
    seg = tt // SUB
    xw_ref[:, 0:SUB, :] = jnp.where(first, 0.0, xprev_ref[...])
    xw_ref[:, SUB:SUB + tt, :] = x_ref[...]
    xw_ref[:, SUB + tt:, :] = jnp.where(last, 0.0, xnext_ref[...])
    left = LRU_CONV // 2
    slabs = []
    for c in range(NSLAB):
        cs = slice(c * LANE, (c + 1) * LANE)
        pieces = []
        for j in range(seg):
            acc = cb_ref[:, cs]
            for k in range(LRU_CONV):
                tap = xw_ref[c, pl.ds(SUB - left + j + k, SUB, stride=seg), :]
                acc = acc + tap * cw_ref[k:k + 1, cs]
            pieces.append(acc)
        slabs.append(jnp.concatenate(pieces, axis=0))
    xc = jnp.concatenate(slabs, axis=1)
    xcb = xc.astype(jnp.bfloat16)
    half = W // 2
    g0 = jnp.dot(xcb[:, :half], wg_ref[0], preferred_element_type=jnp.float32)
    g1 = jnp.dot(xcb[:, half:], wg_ref[1], preferred_element_type=jnp.float32)
    ra = jnp.concatenate([g0[:, :half], g1[:, :half]], axis=1) + ba_ref[...]
    ri = jnp.concatenate([g0[:, half:], g1[:, half:]], axis=1) + bx_ref[...]
    r = jax.nn.sigmoid(ra)
    ig = jax.nn.sigmoid(ri)
    lam = lam_ref[...]
    sp = jnp.maximum(-lam, 0.0) + jnp.log(1.0 + jnp.exp(-jnp.abs(lam)))
    log_a = -LRU_C * r * sp
    a = jnp.exp(log_a)
    z = 1.0 - a * a
    u = jnp.where(z > 0.0, z * lax.rsqrt(z), 0.0) * (ig * xc)
    row = lax.broadcasted_iota(jnp.int32, (tt, W), 0)
    time = t0 + (row % SUB) * seg + row // SUB
    return a, jnp.where(time >= PAD, u, 0.0)


def _scan_block(a, u, h_ref, row0, carry_ref, *, tt, reverse):
    W = LRU_WIDTH
    seg = tt // SUB
    order = range(seg - 1, -1, -1) if reverse else range(seg)
    h = jnp.zeros((SUB, W), jnp.float32)
    p = jnp.ones((SUB, W), jnp.float32)
    hs, ps = {}, {}
    for j in order:
        aj = a[j * SUB:(j + 1) * SUB]
        h = aj * h + u[j * SUB:(j + 1) * SUB]
        p = aj * p
        hs[j], ps[j] = h, p
    sub = lax.broadcasted_iota(jnp.int32, (SUB, W), 0)
    e, pf = h, p
    for d in (1, 2, 4):
        if reverse:
            keep, sh = sub < SUB - d, SUB - d
        else:
            keep, sh = sub >= d, d
        e_sh = jnp.where(keep, pltpu.roll(e, sh, axis=0), 0.0)
        pf_sh = jnp.where(keep, pltpu.roll(pf, sh, axis=0), 1.0)
        e = e + pf * e_sh
        pf = pf * pf_sh
    carry = carry_ref[...]
    leaving = e + pf * carry
    if reverse:
        entering = jnp.where(sub < SUB - 1, pltpu.roll(leaving, SUB - 1, axis=0), carry)
        carry_ref[...] = leaving[0:1]
    else:
        entering = jnp.where(sub >= 1, pltpu.roll(leaving, 1, axis=0), carry)
        carry_ref[...] = leaving[SUB - 1:SUB]
    for j in range(seg):
        hj = hs[j] + ps[j] * entering
        for c in range(NSLAB):
            h_ref[c, pl.ds(row0 + j, SUB, stride=seg), :] = hj[:, c * LANE:(c + 1) * LANE]


def _lru_kernel(x_ref, xprev_ref, xnext_ref, cw_ref, cb_ref, wg_ref, ba_ref, bx_ref, lam_ref,
                gl_ref, gg_ref, y_ref, xw_ref, hf_ref, hb_ref, carry_ref, *, tt, nt):
    j = pl.program_id(1)
    bwd = j >= nt
    i = jnp.where(bwd, 2 * nt - 1 - j, j)

    @pl.when((j == 0) | (j == nt))
    def _():
        carry_ref[...] = jnp.zeros_like(carry_ref)

    a, u = _lru_gates(x_ref, xprev_ref, xnext_ref, cw_ref, cb_ref, wg_ref, ba_ref, bx_ref,
                      lam_ref, xw_ref, tt=tt, first=i == 0, last=i == nt - 1, t0=i * tt)
    r0 = pl.multiple_of(i * tt, SUB)

    @pl.when(jnp.logical_not(bwd))
    def _():
        _scan_block(a, u, hf_ref, r0, carry_ref, tt=tt, reverse=False)

    @pl.when(bwd)
    def _():
        _scan_block(a, u, hb_ref, 0, carry_ref, tt=tt, reverse=True)
        ys = [(hf_ref[c, pl.ds(r0, tt), :] + hb_ref[c])
              * _gelu_tanh(gl_ref[:, c * LANE:(c + 1) * LANE]) for c in range(NSLAB)]
        y = jnp.concatenate(ys, axis=1)
        y_ref[...] = _rms(y, gg_ref[...]).astype(y_ref.dtype)


def _lru(xl, cw, cb, wg, ba, bx, lam, gl, gg, *, tt):
    B, _, Tp, _ = xl.shape
    W = LRU_WIDTH
    nt = Tp // tt
    nsub = Tp // SUB
    per = tt // SUB
    tblk = lambda j: jnp.where(j >= nt, 2 * nt - 1 - j, j)
    oblk = lambda j: jnp.where(j >= nt, 2 * nt - 1 - j, nt - 1)
    slab = lambda rows, f: pl.BlockSpec((None, NSLAB, rows, LANE), lambda b, j: (b, 0, f(j), 0))
    direction = lambda a: pl.BlockSpec((None,) + a.shape[1:],
                                       lambda b, j: (j // nt,) + (0,) * (a.ndim - 1))
    full = lambda a: pl.BlockSpec(a.shape, lambda b, j: (0,) * a.ndim)
    row = pl.BlockSpec((None, tt, W), lambda b, j: (b, oblk(j), 0))
    return pl.pallas_call(
        functools.partial(_lru_kernel, tt=tt, nt=nt),
        grid=(B, 2 * nt),
        in_specs=[slab(tt, tblk),
                  slab(SUB, lambda j: jnp.maximum(tblk(j) * per - 1, 0)),
                  slab(SUB, lambda j: jnp.minimum((tblk(j) + 1) * per, nsub - 1)),
                  full(cw), full(cb), direction(wg), direction(ba), direction(bx), direction(lam),
                  row, full(gg)],
        out_specs=row,
        out_shape=jax.ShapeDtypeStruct((B, Tp, W), jnp.bfloat16),
        scratch_shapes=[pltpu.VMEM((NSLAB, tt + 2 * SUB, LANE), jnp.float32),
                        pltpu.VMEM((NSLAB, Tp, LANE), jnp.float32),
                        pltpu.VMEM((NSLAB, tt, LANE), jnp.float32),
                        pltpu.VMEM((1, LRU_WIDTH), jnp.float32)],
        compiler_params=pltpu.CompilerParams(
            dimension_semantics=("arbitrary", "arbitrary"), vmem_limit_bytes=VMEM_LIMIT),
        name="lru",
    )(xl, xl, xl, cw, cb, wg, ba, bx, lam, gl, gg)


def _max_over_rows(s):
    rows = s.shape[0]
    while rows % (2 * SUB) == 0 and rows > 32 * SUB:
        fold = 4 if rows % (4 * SUB) == 0 else 2
        rows //= fold
        s = jnp.max(s.reshape(fold, rows, s.shape[1]), axis=0)
    return jnp.max(s, axis=0, keepdims=True)


def _attn_kernel(q_ref, k_ref, vt_ref, o_ref, *, tq, lead, span):
    heads = [slice(hh * HP, (hh + 1) * HP) for hh in range(2)]
    nkeys = k_ref.shape[0]
    nt = (((1,), (1,)), ((), ()))
    rows = lax.broadcasted_iota(jnp.int32, (HP, tq), 0)

    def scores(hh, lo, hi):
        return lax.dot_general(k_ref[lo:hi, heads[hh]], q_ref[:, heads[hh]], nt,
                               preferred_element_type=jnp.float32)

    def pv(hh, lo, hi, s, m):
        return jnp.dot(vt_ref[heads[hh], lo:hi], jnp.exp2(s - m).astype(jnp.bfloat16),
                       preferred_element_type=jnp.float32)

    def finish(a0, a1):
        l0, l1 = a0[V_DIM:V_DIM + 1, :], a1[0:1, :]
        return jnp.where(rows < V_DIM, a0 / l0, a1 / l1).T, l0, l1

    accs = []
    for hh in range(2):
        s = scores(hh, 0, lead)
        m = _max_over_rows(s)
        acc = pv(hh, 0, lead, s, m)
        for lo in range(lead, nkeys, span):
            acc = acc + pv(hh, lo, lo + span, scores(hh, lo, lo + span), m)
        accs.append(acc)
    out, l0, l1 = finish(*accs)
    lmin = jnp.min(jnp.minimum(l0, l1))
    lmax = jnp.max(jnp.maximum(l0, l1))
    safe = (lmin > 2.0 ** -100) & (lmax < 2.0 ** 100)

    @pl.when(safe)
    def _():
        o_ref[...] = out

    @pl.when(jnp.logical_not(safe))
    def _():
        exact = []
        for hh in range(2):
            s = scores(hh, 0, nkeys)
            exact.append(pv(hh, 0, nkeys, s, _max_over_rows(s)))
        o_ref[...] = finish(*exact)[0]


def _attn(q, k, vt, *, tq, lead, span):
    B, Tp, _ = q.shape
    npair = MLA_HEADS // 2
    return pl.pallas_call(
        functools.partial(_attn_kernel, tq=tq, lead=lead, span=span),
        grid=(B, npair, Tp // tq),
        in_specs=[pl.BlockSpec((None, tq, 2 * HP), lambda b, p, i: (b, i, p)),
                  pl.BlockSpec((None, Tp, 2 * HP), lambda b, p, i: (b, 0, p)),
                  pl.BlockSpec((None, 2 * HP, Tp), lambda b, p, i: (b, p, 0))],
        out_specs=pl.BlockSpec((None, tq, HP), lambda b, p, i: (b, i, p)),
        out_shape=jax.ShapeDtypeStruct((B, Tp, npair * HP), jnp.float32),
        compiler_params=pltpu.CompilerParams(
            dimension_semantics=("arbitrary", "arbitrary", "arbitrary"),
            vmem_limit_bytes=VMEM_LIMIT),
        name="attn",
    )(q, k, vt)


def _mix_ffn_kernel(*refs, tm, nt, ns):
    x_ref, xprev_ref, xnext_ref, mtail_ref = refs[:4]
    yl_refs, (ylprev_ref, ylnext_ref) = refs[4:4 + ns], refs[4 + ns:6 + ns]
    ym_refs, (ymprev_ref, ymnext_ref) = refs[6 + ns:6 + 2 * ns], refs[6 + 2 * ns:8 + 2 * ns]
    (mg_ref, wo_ref, g_ref, wg_ref, wu_ref, cw_ref, cb_ref, wd_ref, o_ref,
     gate_ref) = refs[8 + 2 * ns:]
    i = pl.program_id(1)
    f32, bf16 = jnp.float32, jnp.bfloat16
    H = 2 * SUB
    xw = jnp.concatenate([jnp.where(i == 0, mtail_ref[...], xprev_ref[...]), x_ref[...],
                          xnext_ref[...]], axis=0)
    ylw = jnp.concatenate([ylprev_ref[...].astype(f32)[H - SUB:]]
                          + [r[...].astype(f32) for r in yl_refs]
                          + [ylnext_ref[...].astype(f32)[:SUB]], axis=0).astype(bf16)
    ymw = jnp.concatenate([ymprev_ref[...]] + [r[...] for r in ym_refs] + [ymnext_ref[...]],
                          axis=0)
    y = jnp.concatenate([ylw, _rms(ymw, mg_ref[...]).astype(bf16)], axis=1)
    h1 = xw + jnp.dot(y, wo_ref[...], preferred_element_type=f32)
    r = lax.broadcasted_iota(jnp.int32, h1.shape, 0)
    h1 = jnp.where((i == nt - 1) & (r >= tm + SUB), 0.0, h1)
    hn = _rms(h1, g_ref[...]).astype(bf16)
    gate_ref[...] = jnp.dot(hn, wg_ref[...], preferred_element_type=f32)
    up = jnp.dot(hn[SUB:SUB + tm], wu_ref[...], preferred_element_type=f32)
    left = FFN_CONV // 2
    gc = cb_ref[...] + gate_ref[SUB - left:SUB - left + tm, :] * cw_ref[0:1, :]
    for k in range(1, FFN_CONV):
        gc = gc + gate_ref[SUB - left + k:SUB - left + k + tm, :] * cw_ref[k:k + 1, :]
    act = (gc * jax.nn.sigmoid(gc) * up).astype(bf16)
    o_ref[...] = h1[SUB:SUB + tm] + jnp.dot(act, wd_ref[...], preferred_element_type=f32)


def _mix_ffn(x, mtail, yl, ym, mg, wo, g, wg, wu, cw, cb, wd, *, tm):
    B, S, D = x.shape
    Tp = yl.shape[1]
    W = yl.shape[2]
    assert tm % X0 == 0 and S % tm == 0
    nt = S // tm
    ns = tm // X0
    H = 2 * SUB
    full = lambda a: pl.BlockSpec(a.shape, lambda b, i: (0,) * a.ndim)

    def halo(w, rows, total, first_row):
        prev = pl.BlockSpec((None, rows, w),
                            lambda b, i: (b, jnp.maximum(first_row(i) // rows - 1, 0), 0))
        nxt = pl.BlockSpec((None, rows, w),
                           lambda b, i: (b, jnp.minimum((first_row(i) + tm) // rows,
                                                        total // rows - 1), 0))
        return [prev, nxt]

    def mixer_tiles(w):
        return [pl.BlockSpec((None, X0, w), lambda b, i, s=s: (b, 1 + i * ns + s, 0))
                for s in range(ns)]

    seq_row = lambda i: i * tm
    pad_row = lambda i: X0 + i * tm
    in_specs = ([pl.BlockSpec((None, tm, D), lambda b, i: (b, i, 0))] + halo(D, SUB, S, seq_row)
                + [full(mtail)]
                + mixer_tiles(W) + halo(W, H, Tp, pad_row)
                + mixer_tiles(W) + halo(W, SUB, Tp, pad_row)
                + [full(a) for a in (mg, wo, g, wg, wu, cw, cb, wd)])
    return pl.pallas_call(
        functools.partial(_mix_ffn_kernel, tm=tm, nt=nt, ns=ns),
        grid=(B, nt),
        in_specs=in_specs,
        out_specs=pl.BlockSpec((None, tm, D), lambda b, i: (b, i, 0)),
        out_shape=jax.ShapeDtypeStruct((B, S, D), jnp.float32),
        scratch_shapes=[pltpu.VMEM((tm + 2 * SUB, D_FF), jnp.float32)],
        compiler_params=pltpu.CompilerParams(
            dimension_semantics=("arbitrary", "arbitrary"), vmem_limit_bytes=VMEM_LIMIT),
        name="mix_ffn",
    )(x, x, x, mtail, *([yl] * (ns + 2)), *([ym] * (ns + 2)), mg, wo, g, wg, wu, cw, cb, wd)


def _pad_heads(w, width, offset=0):
    K = w.shape[0]
    w = w.reshape(K, MLA_HEADS, width)
    w = jnp.pad(w, ((0, 0), (0, 0), (offset, HP - width - offset)))
    return w.reshape(K, MLA_HEADS * HP)


def _block_diag(w):
    H, Dh, _ = w.shape
    eye = jnp.eye(H, dtype=w.dtype)
    return (eye[:, None, :, None] * w[:, :, None, :]).reshape(H * Dh, H * Dh)


def _gate_weights(w_a, w_x):
    half = LRU_WIDTH // 2
    wa = _block_diag(w_a)
    wx = _block_diag(w_x)
    halves = [jnp.concatenate([wa[c * half:(c + 1) * half, c * half:(c + 1) * half],
                               wx[c * half:(c + 1) * half, c * half:(c + 1) * half]], axis=1)
              for c in range(2)]
    return jnp.stack(halves).astype(jnp.bfloat16)


def kernel(x, meta_tokens, norm_mix_g, w_in, conv_lru_w, conv_lru_b, lru_w_a, lru_b_a, lru_w_x,
           lru_b_x, lru_lambda, lru_gate_g, q_latent_g, w_uq, kv_latent_g, w_ukv, q_norm_g,
           k_norm_g, mla_out_g, w_out, norm_ffn_g, w_ffn_up, conv_ffn_w, conv_ffn_b, w_ffn_down):
    B, S, D = x.shape
    f32, bf16 = jnp.float32, jnp.bfloat16
    Tp = X0 + S
    l = 0
    r2 = lambda a: a.reshape(1, -1).astype(f32)

    head = jnp.concatenate([jnp.zeros((PAD, D), x.dtype), meta_tokens.astype(x.dtype)], axis=0)

    half = QK_ROPE // 2

    def swap(w):
        t = w.reshape(w.shape[:-1] + (-1, HP))
        t = jnp.concatenate([t[..., :QK_NOPE], t[..., QK_NOPE + half:QK_DIM],
                             t[..., QK_NOPE:QK_NOPE + half], t[..., QK_DIM:]], axis=-1)
        return t.reshape(w.shape)

    o = 2 * LRU_WIDTH + Q_LORA + KV_LORA
    kpe_w = jnp.pad(w_in[l][:, o:], ((0, 0), (QK_NOPE, LANE - QK_DIM)))
    win = jnp.concatenate([w_in[l][:, :o], kpe_w, swap(kpe_w)], axis=1).astype(bf16)
    wuq = _pad_heads(w_uq[l], QK_DIM).astype(bf16)
    wuqs = swap(wuq)
    wkv = w_ukv[l].reshape(KV_LORA, MLA_HEADS, QK_NOPE + V_DIM)
    wuk = _pad_heads(wkv[:, :, :QK_NOPE].reshape(KV_LORA, -1), QK_NOPE).astype(bf16)
    wv = jnp.pad(wkv[:, :, QK_NOPE:], ((0, 0), (0, 0), (0, HP - V_DIM)))
    wv = jnp.where((jnp.arange(MLA_HEADS) % 2 == 1)[None, :, None],
                   jnp.roll(wv, V_DIM, axis=2), wv)
    wuvt = wv.reshape(KV_LORA, MLA_HEADS * HP).T.astype(bf16)
    qg = jnp.pad(q_norm_g[l], (0, HP - QK_DIM)).reshape(1, HP)
    kg = jnp.pad(k_norm_g[l], (0, HP - QK_DIM)).reshape(1, HP)

    inv_freq = ROPE_THETA ** (-jnp.arange(0, QK_ROPE, 2, dtype=f32) / QK_ROPE)

    def lane_tables(pos):
        ang = pos[:, None] * inv_freq[None, :]
        n = pos.shape[0]
        cos, sin = jnp.cos(ang), jnp.sin(ang)
        ctab = jnp.concatenate([jnp.ones((n, QK_NOPE), f32), cos, cos,
                                jnp.ones((n, HP - QK_DIM), f32)], axis=1)
        stab = jnp.concatenate([jnp.zeros((n, QK_NOPE), f32), -sin, sin,
                                jnp.zeros((n, HP - QK_DIM), f32)], axis=1)
        return ctab, stab

    rope = lane_tables(jnp.arange(Tp // X0, dtype=f32) * X0) + lane_tables(
        jnp.arange(X0, dtype=f32) - PAD)

    xl, gl, q, k, vt = _inproj(x, head, r2(norm_mix_g[l]), win, r2(q_latent_g[l]), wuq, wuqs,
                              r2(kv_latent_g[l]), wuk, wuvt, qg, swap(qg), kg, swap(kg),
                              *rope, tm=X0)

    cw, cb = conv_lru_w[l], r2(conv_lru_b[l])
    wg = jnp.stack([_gate_weights(lru_w_a[l, d], lru_w_x[l, d]) for d in range(2)])
    per_dir = lambda a: a.astype(f32)[:, None, :]
    yl = _lru(xl, cw, cb, wg, per_dir(lru_b_a[l]), per_dir(lru_b_x[l]), per_dir(lru_lambda[l]),
              gl, r2(lru_gate_g[l]), tt=264)

    ym = _attn(q, k, vt, tq=768, lead=768, span=768)

    wup = w_ffn_up[l].astype(bf16)
    return _mix_ffn(x, head[X0 - SUB:], yl, ym, r2(mla_out_g[l]), w_out[l].astype(bf16),
                    r2(norm_ffn_g[l]), wup[:, :D_FF], wup[:, D_FF:], conv_ffn_w[l],
                    r2(conv_ffn_b[l]), w_ffn_down[l].astype(bf16), tm=512)
```

```python
import functools
import math

import jax
import jax.numpy as jnp
from jax import lax
from jax.experimental import pallas as pl
from jax.experimental.pallas import tpu as pltpu

D_MODEL = 1024
N_META = 16
LRU_WIDTH = 512
LRU_HEADS = 8
LRU_HEAD_DIM = 64
LRU_CONV = 4
LRU_C = 8.0
MLA_HEADS = 8
QK_NOPE = 64
QK_ROPE = 32
QK_DIM = 96
V_DIM = 64
Q_LORA = 256
KV_LORA = 128
ROPE_THETA = 10000.0
D_FF = 2816
FFN_CONV = 3
EPS = 1e-6
NEG_INF = -1e30

LANE = 128
SUB = 8
X0 = 256
PAD = X0 - N_META
HP = LANE
MASK_LANE = QK_DIM

TM_IN = X0
TT_LRU = 264
TQ_ATT = 768
LEAD_ATT = 768
SPAN_ATT = 768
TM_FFN = 512
VMEM_LIMIT = 56 * 1024 * 1024


def _rms(x, g):
    return x * lax.rsqrt(jnp.mean(x * x, axis=-1, keepdims=True) + EPS) * g


def _gelu_tanh(x):
    c0 = math.sqrt(2.0 / math.pi)
    hx = 0.5 * x
    return hx + hx * jnp.tanh(x * (c0 + (c0 * 0.044715) * (x * x)))


def _bdot(a, b):
    return jnp.dot(a.astype(jnp.bfloat16), b, preferred_element_type=jnp.float32)


def _inproj_kernel(x_ref, head_ref, gmix_ref, win_ref, qlg_ref, wuq_ref, wuqs_ref, kvlg_ref,
                   wuk_ref, wuvt_ref, qg_ref, qgs_ref, kg_ref, kgs_ref, ca_ref, sa_ref, cb_ref, sb_ref,
                   xlru_ref, glru_ref, q_ref, k_ref, vt_ref, *, tm):
    i = pl.program_id(1)
    h = jnp.where(i == 0, head_ref[...], x_ref[...])
    hn = _rms(h, gmix_ref[...])
    proj = _bdot(hn, win_ref[...])
    for c in range(LRU_WIDTH // LANE):
        xlru_ref[c] = proj[:, c * LANE:(c + 1) * LANE]
    glru_ref[...] = proj[:, LRU_WIDTH:2 * LRU_WIDTH]
    o = 2 * LRU_WIDTH
    cq = proj[:, o:o + Q_LORA]
    ckv = proj[:, o + Q_LORA:o + Q_LORA + KV_LORA]
    o += Q_LORA + KV_LORA
    kpe = proj[:, o:o + HP]
    kpe_sw = proj[:, o + HP:]
    cqn = _rms(cq, qlg_ref[...]).astype(jnp.bfloat16)
    qraw = jnp.dot(cqn, wuq_ref[...], preferred_element_type=jnp.float32)
    qraw_sw = jnp.dot(cqn, wuqs_ref[...], preferred_element_type=jnp.float32)
    ckvn = _rms(ckv, kvlg_ref[...]).astype(jnp.bfloat16)
    kraw = jnp.dot(ckvn, wuk_ref[...], preferred_element_type=jnp.float32)
    vt = lax.dot_general(wuvt_ref[...], ckvn, (((1,), (1,)), ((), ())),
                         preferred_element_type=jnp.float32)
    vrow = lax.broadcasted_iota(jnp.int32, vt.shape, 0) % (2 * HP)
    vt_ref[...] = jnp.where((vrow == V_DIM) | (vrow == HP), 1.0, vt).astype(jnp.bfloat16)

    lane = lax.broadcasted_iota(jnp.int32, (tm, HP), 1)
    row = lax.broadcasted_iota(jnp.int32, (tm, HP), 0) + i * tm
    key_mask = jnp.where(row >= PAD, 0.0, NEG_INF)
    scale = math.log2(math.e) / math.sqrt(QK_DIM)
    ca, sa = ca_ref[pl.ds(i, 1), :], sa_ref[pl.ds(i, 1), :]
    cosv = ca * cb_ref[...] - sa * sb_ref[...]
    sinv = sa * cb_ref[...] + ca * sb_ref[...]
    q_cos = cosv * (qg_ref[...] * scale)
    q_sin = sinv * (qgs_ref[...] * scale)
    k_cos = cosv * kg_ref[...]
    k_sin_term = kpe_sw * (sinv * kgs_ref[...])

    def inv_rms(xb):
        return lax.rsqrt(jnp.sum(xb * xb, axis=-1, keepdims=True) * (1.0 / QK_DIM) + EPS)

    for hd in range(MLA_HEADS):
        sl = slice(hd * HP, (hd + 1) * HP)
        xq = qraw[:, sl]
        qb = (xq * q_cos + qraw_sw[:, sl] * q_sin) * inv_rms(xq)
        q_ref[:, sl] = jnp.where(lane == MASK_LANE, 1.0, qb).astype(jnp.bfloat16)
        xk = kraw[:, sl] + kpe
        kb = (xk * k_cos + k_sin_term) * inv_rms(xk)
        k_ref[:, sl] = jnp.where(lane == MASK_LANE, key_mask, kb).astype(jnp.bfloat16)


def _x_tile_spec(tm, D):
    return pl.BlockSpec((None, tm, D), lambda b, i: (b, jnp.maximum(i - 1, 0), 0))


def _inproj(x, head, gmix, win, qlg, wuq, wuqs, kvlg, wuk, wuv, qg, qgs, kg, kgs, ca, sa, cb, sb,
            *, tm):
    B, S, D = x.shape
    assert tm == X0
    Tp = X0 + S
    nt = Tp // tm
    row = lambda w: pl.BlockSpec((None, tm, w), lambda b, i: (b, i, 0))
    full = lambda a: pl.BlockSpec(a.shape, lambda b, i: (0,) * a.ndim)
    f32, bf16 = jnp.float32, jnp.bfloat16
    return pl.pallas_call(
        functools.partial(_inproj_kernel, tm=tm),
        grid=(B, nt),
        in_specs=[_x_tile_spec(tm, D), full(head), full(gmix), full(win), full(qlg), full(wuq),
                  full(wuqs), full(kvlg), full(wuk), full(wuv), full(qg), full(qgs), full(kg),
                  full(kgs), full(ca), full(sa), full(cb), full(sb)],
        out_specs=[pl.BlockSpec((None, LRU_WIDTH // LANE, tm, LANE), lambda b, i: (b, 0, i, 0)),
                   row(LRU_WIDTH), row(MLA_HEADS * HP), row(MLA_HEADS * HP),
                   pl.BlockSpec((None, MLA_HEADS * HP, tm), lambda b, i: (b, 0, i))],
        out_shape=[jax.ShapeDtypeStruct((B, LRU_WIDTH // LANE, Tp, LANE), f32),
                   jax.ShapeDtypeStruct((B, Tp, LRU_WIDTH), f32),
                   jax.ShapeDtypeStruct((B, Tp, MLA_HEADS * HP), bf16),
                   jax.ShapeDtypeStruct((B, Tp, MLA_HEADS * HP), bf16),
                   jax.ShapeDtypeStruct((B, MLA_HEADS * HP, Tp), bf16)],
        compiler_params=pltpu.CompilerParams(
            dimension_semantics=("arbitrary", "arbitrary"), vmem_limit_bytes=VMEM_LIMIT),
        name="inproj",
    )(x, head, gmix, win, qlg, wuq, wuqs, kvlg, wuk, wuv, qg, qgs, kg, kgs, ca, sa, cb, sb)


NSLAB = LRU_WIDTH // LANE


def _lru_gates(x_ref, xprev_ref, xnext_ref, cw_ref, cb_ref, wg_ref, ba_ref, bx_ref, lam_ref,
               xw_ref, *, tt, first, last, t0):
    W = LRU_WIDTH
    seg = tt // SUB
    xw_ref[:, 0:SUB, :] = jnp.where(first, 0.0, xprev_ref[...])
    xw_ref[:, SUB:SUB + tt, :] = x_ref[...]
    xw_ref[:, SUB + tt:, :] = jnp.where(last, 0.0, xnext_ref[...])
    left = LRU_CONV // 2
    slabs = []
    for c in range(NSLAB):
        cs = slice(c * LANE, (c + 1) * LANE)
        pieces = []
        for j in range(seg):
            acc = cb_ref[:, cs]
            for k in range(LRU_CONV):
                tap = xw_ref[c, pl.ds(SUB - left + j + k, SUB, stride=seg), :]
                acc = acc + tap * cw_ref[k:k + 1, cs]
            pieces.append(acc)
        slabs.append(jnp.concatenate(pieces, axis=0))
    xc = jnp.concatenate(slabs, axis=1)
    xcb = xc.astype(jnp.bfloat16)
    half = W // 2
    g0 = jnp.dot(xcb[:, :half], wg_ref[0], preferred_element_type=jnp.float32)
    g1 = jnp.dot(xcb[:, half:], wg_ref[1], preferred_element_type=jnp.float32)
    ra = jnp.concatenate([g0[:, :half], g1[:, :half]], axis=1) + ba_ref[...]
    ri = jnp.concatenate([g0[:, half:], g1[:, half:]], axis=1) + bx_ref[...]
    r = jax.nn.sigmoid(ra)
    ig = jax.nn.sigmoid(ri)
    lam = lam_ref[...]
    sp = jnp.maximum(-lam, 0.0) + jnp.log(1.0 + jnp.exp(-jnp.abs(lam)))
    log_a = -LRU_C * r * sp
    a = jnp.exp(log_a)
    z = 1.0 - a * a
    u = jnp.where(z > 0.0, z * lax.rsqrt(z), 0.0) * (ig * xc)
    row = lax.broadcasted_iota(jnp.int32, (tt, W), 0)
    time = t0 + (row % SUB) * seg + row // SUB
    return a, jnp.where(time >= PAD, u, 0.0)


def _scan_block(a, u, h_ref, row0, carry_ref, *, tt, reverse):
    W = LRU_WIDTH
    seg = tt // SUB
    order = range(seg - 1, -1, -1) if reverse else range(seg)
    h = jnp.zeros((SUB, W), jnp.float32)
    p = jnp.ones((SUB, W), jnp.float32)
    hs, ps = {}, {}
    for j in order:
        aj = a[j * SUB:(j + 1) * SUB]
        h = aj * h + u[j * SUB:(j + 1) * SUB]
        p = aj * p
        hs[j], ps[j] = h, p
    sub = lax.broadcasted_iota(jnp.int32, (SUB, W), 0)
    e, pf = h, p
    for d in (1, 2, 4):
        if reverse:
            keep, sh = sub < SUB - d, SUB - d
        else:
            keep, sh = sub >= d, d
        e_sh = jnp.where(keep, pltpu.roll(e, sh, axis=0), 0.0)
        pf_sh = jnp.where(keep, pltpu.roll(pf, sh, axis=0), 1.0)
        e = e + pf * e_sh
        pf = pf * pf_sh
    carry = carry_ref[...]
    leaving = e + pf * carry
    if reverse:
        entering = jnp.where(sub < SUB - 1, pltpu.roll(leaving, SUB - 1, axis=0), carry)
        carry_ref[...] = leaving[0:1]
    else:
        entering = jnp.where(sub >= 1, pltpu.roll(leaving, 1, axis=0), carry)
        carry_ref[...] = leaving[SUB - 1:SUB]
    for j in range(seg):
        hj = hs[j] + ps[j] * entering
        for c in range(NSLAB):
            h_ref[c, pl.ds(row0 + j, SUB, stride=seg), :] = hj[:, c * LANE:(c + 1) * LANE]


def _lru_kernel(x_ref, xprev_ref, xnext_ref, cw_ref, cb_ref, wg_ref, ba_ref, bx_ref, lam_ref,
                gl_ref, gg_ref, y_ref, xw_ref, hf_ref, hb_ref, carry_ref, *, tt, nt):
    j = pl.program_id(1)
    bwd = j >= nt
    i = jnp.where(bwd, 2 * nt - 1 - j, j)

    @pl.when((j == 0) | (j == nt))
    def _():
        carry_ref[...] = jnp.zeros_like(carry_ref)

    a, u = _lru_gates(x_ref, xprev_ref, xnext_ref, cw_ref, cb_ref, wg_ref, ba_ref, bx_ref,
                      lam_ref, xw_ref, tt=tt, first=i == 0, last=i == nt - 1, t0=i * tt)
    r0 = pl.multiple_of(i * tt, SUB)

    @pl.when(jnp.logical_not(bwd))
    def _():
        _scan_block(a, u, hf_ref, r0, carry_ref, tt=tt, reverse=False)

    @pl.when(bwd)
    def _():
        _scan_block(a, u, hb_ref, 0, carry_ref, tt=tt, reverse=True)
        ys = [(hf_ref[c, pl.ds(r0, tt), :] + hb_ref[c])
              * _gelu_tanh(gl_ref[:, c * LANE:(c + 1) * LANE]) for c in range(NSLAB)]
        y = jnp.concatenate(ys, axis=1)
        y_ref[...] = _rms(y, gg_ref[...]).astype(y_ref.dtype)


def _lru(xl, cw, cb, wg, ba, bx, lam, gl, gg, *, tt):
    B, _, Tp, _ = xl.shape
    W = LRU_WIDTH
    assert Tp % tt == 0 and tt % SUB == 0 and (tt // SUB) % 2 == 1
    nt = Tp // tt
    nsub = Tp // SUB
    per = tt // SUB
    tblk = lambda j: jnp.where(j >= nt, 2 * nt - 1 - j, j)
    oblk = lambda j: jnp.where(j >= nt, 2 * nt - 1 - j, nt - 1)
    slab = lambda rows, f: pl.BlockSpec((None, NSLAB, rows, LANE), lambda b, j: (b, 0, f(j), 0))
    direction = lambda a: pl.BlockSpec((None,) + a.shape[1:],
                                       lambda b, j: (j // nt,) + (0,) * (a.ndim - 1))
    full = lambda a: pl.BlockSpec(a.shape, lambda b, j: (0,) * a.ndim)
    row = pl.BlockSpec((None, tt, W), lambda b, j: (b, oblk(j), 0))
    return pl.pallas_call(
        functools.partial(_lru_kernel, tt=tt, nt=nt),
        grid=(B, 2 * nt),
        in_specs=[slab(tt, tblk),
                  slab(SUB, lambda j: jnp.maximum(tblk(j) * per - 1, 0)),
                  slab(SUB, lambda j: jnp.minimum((tblk(j) + 1) * per, nsub - 1)),
                  full(cw), full(cb), direction(wg), direction(ba), direction(bx), direction(lam),
                  row, full(gg)],
        out_specs=row,
        out_shape=jax.ShapeDtypeStruct((B, Tp, W), jnp.bfloat16),
        scratch_shapes=[pltpu.VMEM((NSLAB, tt + 2 * SUB, LANE), jnp.float32),
                        pltpu.VMEM((NSLAB, Tp, LANE), jnp.float32),
                        pltpu.VMEM((NSLAB, tt, LANE), jnp.float32),
                        pltpu.VMEM((1, LRU_WIDTH), jnp.float32)],
        compiler_params=pltpu.CompilerParams(
            dimension_semantics=("arbitrary", "arbitrary"), vmem_limit_bytes=VMEM_LIMIT),
        name="lru",
    )(xl, xl, xl, cw, cb, wg, ba, bx, lam, gl, gg)


def _max_over_rows(s):
    rows = s.shape[0]
    while rows % (2 * SUB) == 0 and rows > 32 * SUB:
        fold = 4 if rows % (4 * SUB) == 0 else 2
        rows //= fold
        s = jnp.max(s.reshape(fold, rows, s.shape[1]), axis=0)
    return jnp.max(s, axis=0, keepdims=True)


def _attn_kernel(q_ref, k_ref, vt_ref, o_ref, *, tq, lead, span):
    heads = [slice(hh * HP, (hh + 1) * HP) for hh in range(2)]
    nkeys = k_ref.shape[0]
    nt = (((1,), (1,)), ((), ()))
    rows = lax.broadcasted_iota(jnp.int32, (HP, tq), 0)

    def scores(hh, lo, hi):
        return lax.dot_general(k_ref[lo:hi, heads[hh]], q_ref[:, heads[hh]], nt,
                               preferred_element_type=jnp.float32)

    def pv(hh, lo, hi, s, m):
        return jnp.dot(vt_ref[heads[hh], lo:hi], jnp.exp2(s - m).astype(jnp.bfloat16),
                       preferred_element_type=jnp.float32)

    def finish(a0, a1):
        l0, l1 = a0[V_DIM:V_DIM + 1, :], a1[0:1, :]
        return jnp.where(rows < V_DIM, a0 / l0, a1 / l1).T, l0, l1

    accs = []
    for hh in range(2):
        s = scores(hh, 0, lead)
        m = _max_over_rows(s)
        acc = pv(hh, 0, lead, s, m)
        for lo in range(lead, nkeys, span):
            acc = acc + pv(hh, lo, lo + span, scores(hh, lo, lo + span), m)
        accs.append(acc)
    out, l0, l1 = finish(*accs)
    lmin = jnp.min(jnp.minimum(l0, l1))
    lmax = jnp.max(jnp.maximum(l0, l1))
    safe = (lmin > 2.0 ** -100) & (lmax < 2.0 ** 100)

    @pl.when(safe)
    def _():
        o_ref[...] = out

    @pl.when(jnp.logical_not(safe))
    def _():
        exact = []
        for hh in range(2):
            s = scores(hh, 0, nkeys)
            exact.append(pv(hh, 0, nkeys, s, _max_over_rows(s)))
        o_ref[...] = finish(*exact)[0]


def _attn(q, k, vt, *, tq, lead, span):
    B, Tp, _ = q.shape
    assert Tp % tq == 0 and tq % (2 * HP) == 0 and lead % (2 * HP) == 0 and span % (2 * HP) == 0
    npair = MLA_HEADS // 2
    return pl.pallas_call(
        functools.partial(_attn_kernel, tq=tq, lead=lead, span=span),
        grid=(B, npair, Tp // tq),
        in_specs=[pl.BlockSpec((None, tq, 2 * HP), lambda b, p, i: (b, i, p)),
                  pl.BlockSpec((None, Tp, 2 * HP), lambda b, p, i: (b, 0, p)),
                  pl.BlockSpec((None, 2 * HP, Tp), lambda b, p, i: (b, p, 0))],
        out_specs=pl.BlockSpec((None, tq, HP), lambda b, p, i: (b, i, p)),
        out_shape=jax.ShapeDtypeStruct((B, Tp, npair * HP), jnp.float32),
        compiler_params=pltpu.CompilerParams(
            dimension_semantics=("arbitrary", "arbitrary", "arbitrary"),
            vmem_limit_bytes=VMEM_LIMIT),
        name="attn",
    )(q, k, vt)


def _mix_ffn_kernel(*refs, tm, nt, ns):
    x_ref, xprev_ref, xnext_ref, mtail_ref = refs[:4]
    yl_refs, (ylprev_ref, ylnext_ref) = refs[4:4 + ns], refs[4 + ns:6 + ns]
    ym_refs, (ymprev_ref, ymnext_ref) = refs[6 + ns:6 + 2 * ns], refs[6 + 2 * ns:8 + 2 * ns]
    (mg_ref, wo_ref, g_ref, wg_ref, wu_ref, cw_ref, cb_ref, wd_ref, o_ref,
     gate_ref) = refs[8 + 2 * ns:]
    i = pl.program_id(1)
    f32, bf16 = jnp.float32, jnp.bfloat16
    H = 2 * SUB
    xw = jnp.concatenate([jnp.where(i == 0, mtail_ref[...], xprev_ref[...]), x_ref[...],
                          xnext_ref[...]], axis=0)
    ylw = jnp.concatenate([ylprev_ref[...].astype(f32)[H - SUB:]]
                          + [r[...].astype(f32) for r in yl_refs]
                          + [ylnext_ref[...].astype(f32)[:SUB]], axis=0).astype(bf16)
    ymw = jnp.concatenate([ymprev_ref[...]] + [r[...] for r in ym_refs] + [ymnext_ref[...]],
                          axis=0)
    y = jnp.concatenate([ylw, _rms(ymw, mg_ref[...]).astype(bf16)], axis=1)
    h1 = xw + jnp.dot(y, wo_ref[...], preferred_element_type=f32)
    r = lax.broadcasted_iota(jnp.int32, h1.shape, 0)
    h1 = jnp.where((i == nt - 1) & (r >= tm + SUB), 0.0, h1)
    hn = _rms(h1, g_ref[...]).astype(bf16)
    gate_ref[...] = jnp.dot(hn, wg_ref[...], preferred_element_type=f32)
    up = jnp.dot(hn[SUB:SUB + tm], wu_ref[...], preferred_element_type=f32)
    left = FFN_CONV // 2
    gc = cb_ref[...] + gate_ref[SUB - left:SUB - left + tm, :] * cw_ref[0:1, :]
    for k in range(1, FFN_CONV):
        gc = gc + gate_ref[SUB - left + k:SUB - left + k + tm, :] * cw_ref[k:k + 1, :]
    act = (gc * jax.nn.sigmoid(gc) * up).astype(bf16)
    o_ref[...] = h1[SUB:SUB + tm] + jnp.dot(act, wd_ref[...], preferred_element_type=f32)


def _mix_ffn(x, mtail, yl, ym, mg, wo, g, wg, wu, cw, cb, wd, *, tm):
    B, S, D = x.shape
    Tp = yl.shape[1]
    W = yl.shape[2]
    assert tm % X0 == 0 and S % tm == 0
    nt = S // tm
    ns = tm // X0
    H = 2 * SUB
    full = lambda a: pl.BlockSpec(a.shape, lambda b, i: (0,) * a.ndim)

    def halo(w, rows, total, first_row):
        prev = pl.BlockSpec((None, rows, w),
                            lambda b, i: (b, jnp.maximum(first_row(i) // rows - 1, 0), 0))
        nxt = pl.BlockSpec((None, rows, w),
                           lambda b, i: (b, jnp.minimum((first_row(i) + tm) // rows,
                                                        total // rows - 1), 0))
        return [prev, nxt]

    def mixer_tiles(w):
        return [pl.BlockSpec((None, X0, w), lambda b, i, s=s: (b, 1 + i * ns + s, 0))
                for s in range(ns)]

    seq_row = lambda i: i * tm
    pad_row = lambda i: X0 + i * tm
    in_specs = ([pl.BlockSpec((None, tm, D), lambda b, i: (b, i, 0))] + halo(D, SUB, S, seq_row)
                + [full(mtail)]
                + mixer_tiles(W) + halo(W, H, Tp, pad_row)
                + mixer_tiles(W) + halo(W, SUB, Tp, pad_row)
                + [full(a) for a in (mg, wo, g, wg, wu, cw, cb, wd)])
    return pl.pallas_call(
        functools.partial(_mix_ffn_kernel, tm=tm, nt=nt, ns=ns),
        grid=(B, nt),
        in_specs=in_specs,
        out_specs=pl.BlockSpec((None, tm, D), lambda b, i: (b, i, 0)),
        out_shape=jax.ShapeDtypeStruct((B, S, D), jnp.float32),
        scratch_shapes=[pltpu.VMEM((tm + 2 * SUB, D_FF), jnp.float32)],
        compiler_params=pltpu.CompilerParams(
            dimension_semantics=("arbitrary", "arbitrary"), vmem_limit_bytes=VMEM_LIMIT),
        name="mix_ffn",
    )(x, x, x, mtail, *([yl] * (ns + 2)), *([ym] * (ns + 2)), mg, wo, g, wg, wu, cw, cb, wd)


def _pad_heads(w, width, offset=0):
    K = w.shape[0]
    w = w.reshape(K, MLA_HEADS, width)
    w = jnp.pad(w, ((0, 0), (0, 0), (offset, HP - width - offset)))
    return w.reshape(K, MLA_HEADS * HP)


def _block_diag(w):
    H, Dh, _ = w.shape
    eye = jnp.eye(H, dtype=w.dtype)
    return (eye[:, None, :, None] * w[:, :, None, :]).reshape(H * Dh, H * Dh)


def _gate_weights(w_a, w_x):
    half = LRU_WIDTH // 2
    wa = _block_diag(w_a)
    wx = _block_diag(w_x)
    halves = [jnp.concatenate([wa[c * half:(c + 1) * half, c * half:(c + 1) * half],
                               wx[c * half:(c + 1) * half, c * half:(c + 1) * half]], axis=1)
              for c in range(2)]
    return jnp.stack(halves).astype(jnp.bfloat16)


def kernel(x, meta_tokens, norm_mix_g, w_in, conv_lru_w, conv_lru_b, lru_w_a, lru_b_a, lru_w_x,
           lru_b_x, lru_lambda, lru_gate_g, q_latent_g, w_uq, kv_latent_g, w_ukv, q_norm_g,
           k_norm_g, mla_out_g, w_out, norm_ffn_g, w_ffn_up, conv_ffn_w, conv_ffn_b, w_ffn_down):
    B, S, D = x.shape
    f32, bf16 = jnp.float32, jnp.bfloat16
    Tp = X0 + S
    l = 0
    r2 = lambda a: a.reshape(1, -1).astype(f32)

    head = jnp.concatenate([jnp.zeros((PAD, D), x.dtype), meta_tokens.astype(x.dtype)], axis=0)

    half = QK_ROPE // 2

    def swap(w):
        t = w.reshape(w.shape[:-1] + (-1, HP))
        t = jnp.concatenate([t[..., :QK_NOPE], t[..., QK_NOPE + half:QK_DIM],
                             t[..., QK_NOPE:QK_NOPE + half], t[..., QK_DIM:]], axis=-1)
        return t.reshape(w.shape)

    o = 2 * LRU_WIDTH + Q_LORA + KV_LORA
    kpe_w = jnp.pad(w_in[l][:, o:], ((0, 0), (QK_NOPE, LANE - QK_DIM)))
    win = jnp.concatenate([w_in[l][:, :o], kpe_w, swap(kpe_w)], axis=1).astype(bf16)
    wuq = _pad_heads(w_uq[l], QK_DIM).astype(bf16)
    wuqs = swap(wuq)
    wkv = w_ukv[l].reshape(KV_LORA, MLA_HEADS, QK_NOPE + V_DIM)
    wuk = _pad_heads(wkv[:, :, :QK_NOPE].reshape(KV_LORA, -1), QK_NOPE).astype(bf16)
    wv = jnp.pad(wkv[:, :, QK_NOPE:], ((0, 0), (0, 0), (0, HP - V_DIM)))
    wv = jnp.where((jnp.arange(MLA_HEADS) % 2 == 1)[None, :, None],
                   jnp.roll(wv, V_DIM, axis=2), wv)
    wuvt = wv.reshape(KV_LORA, MLA_HEADS * HP).T.astype(bf16)
    qg = jnp.pad(q_norm_g[l], (0, HP - QK_DIM)).reshape(1, HP)
    kg = jnp.pad(k_norm_g[l], (0, HP - QK_DIM)).reshape(1, HP)

    inv_freq = ROPE_THETA ** (-jnp.arange(0, QK_ROPE, 2, dtype=f32) / QK_ROPE)

    def lane_tables(pos):
        ang = pos[:, None] * inv_freq[None, :]
        n = pos.shape[0]
        cos, sin = jnp.cos(ang), jnp.sin(ang)
        ctab = jnp.concatenate([jnp.ones((n, QK_NOPE), f32), cos, cos,
                                jnp.ones((n, HP - QK_DIM), f32)], axis=1)
        stab = jnp.concatenate([jnp.zeros((n, QK_NOPE), f32), -sin, sin,
                                jnp.zeros((n, HP - QK_DIM), f32)], axis=1)
        return ctab, stab

    rope = lane_tables(jnp.arange(Tp // X0, dtype=f32) * X0) + lane_tables(
        jnp.arange(X0, dtype=f32) - PAD)

    xl, gl, q, k, vt = _inproj(x, head, r2(norm_mix_g[l]), win, r2(q_latent_g[l]), wuq, wuqs,
                              r2(kv_latent_g[l]), wuk, wuvt, qg, swap(qg), kg, swap(kg),
                              *rope, tm=TM_IN)

    cw, cb = conv_lru_w[l], r2(conv_lru_b[l])
    wg = jnp.stack([_gate_weights(lru_w_a[l, d], lru_w_x[l, d]) for d in range(2)])
    per_dir = lambda a: a.astype(f32)[:, None, :]
    yl = _lru(xl, cw, cb, wg, per_dir(lru_b_a[l]), per_dir(lru_b_x[l]), per_dir(lru_lambda[l]),
              gl, r2(lru_gate_g[l]), tt=TT_LRU)

    ym = _attn(q, k, vt, tq=TQ_ATT, lead=LEAD_ATT, span=SPAN_ATT)

    wup = w_ffn_up[l].astype(bf16)
    return _mix_ffn(x, head[X0 - SUB:], yl, ym, r2(mla_out_g[l]), w_out[l].astype(bf16),
                    r2(norm_ffn_g[l]), wup[:, :D_FF], wup[:, D_FF:], conv_ffn_w[l],
                    r2(conv_ffn_b[l]), w_ffn_down[l].astype(bf16), tm=TM_FFN)
```

```python
import functools
import math

import jax
import jax.numpy as jnp
from jax import lax
from jax.experimental import pallas as pl
from jax.experimental.pallas import tpu as pltpu

D_MODEL = 1024
N_META = 16
LRU_WIDTH = 512
LRU_HEADS = 8
LRU_HEAD_DIM = 64
LRU_CONV = 4
LRU_C = 8.0
MLA_HEADS = 8
QK_NOPE = 64
QK_ROPE = 32
QK_DIM = 96
V_DIM = 64
Q_LORA = 256
KV_LORA = 128
ROPE_THETA = 10000.0
D_FF = 2816
FFN_CONV = 3
EPS = 1e-6
NEG_INF = -1e30

LANE = 128
SUB = 8
X0 = 256
PAD = X0 - N_META
HP = LANE
MASK_LANE = QK_DIM

TM_IN = 768
TT_LRU = 264
TQ_ATT = 768
LEAD_ATT = 768
SPAN_ATT = 768
TM_FFN = 512
VMEM_LIMIT = 56 * 1024 * 1024


def _rms(x, g):
    return x * lax.rsqrt(jnp.mean(x * x, axis=-1, keepdims=True) + EPS) * g


def _gelu_tanh(x):
    c0 = math.sqrt(2.0 / math.pi)
    hx = 0.5 * x
    return hx + hx * jnp.tanh(x * (c0 + (c0 * 0.044715) * (x * x)))


def _bdot(a, b):
    return jnp.dot(a.astype(jnp.bfloat16), b, preferred_element_type=jnp.float32)


def _inproj_kernel(*refs, tm, ns):
    x_refs = refs[:ns]
    (head_ref, gmix_ref, win_ref, qlg_ref, wuq_ref, wuqs_ref, kvlg_ref, wuk_ref, wuvt_ref,
     qg_ref, qgs_ref, kg_ref, kgs_ref, ca_ref, sa_ref, cb_ref, sb_ref,
     xlru_ref, glru_ref, q_ref, k_ref, vt_ref) = refs[ns:]
    i = pl.program_id(1)
    h = jnp.concatenate([jnp.where(i == 0, head_ref[...], x_refs[0][...])]
                        + [r[...] for r in x_refs[1:]], axis=0)
    hn = _rms(h, gmix_ref[...])
    proj = _bdot(hn, win_ref[...])
    for c in range(LRU_WIDTH // LANE):
        xlru_ref[c] = proj[:, c * LANE:(c + 1) * LANE]
    glru_ref[...] = proj[:, LRU_WIDTH:2 * LRU_WIDTH]
    o = 2 * LRU_WIDTH
    cq = proj[:, o:o + Q_LORA]
    ckv = proj[:, o + Q_LORA:o + Q_LORA + KV_LORA]
    o += Q_LORA + KV_LORA
    kpe = proj[:, o:o + HP]
    kpe_sw = proj[:, o + HP:]
    cqn = _rms(cq, qlg_ref[...]).astype(jnp.bfloat16)
    qraw = jnp.dot(cqn, wuq_ref[...], preferred_element_type=jnp.float32)
    qraw_sw = jnp.dot(cqn, wuqs_ref[...], preferred_element_type=jnp.float32)
    ckvn = _rms(ckv, kvlg_ref[...]).astype(jnp.bfloat16)
    kraw = jnp.dot(ckvn, wuk_ref[...], preferred_element_type=jnp.float32)
    vt = lax.dot_general(wuvt_ref[...], ckvn, (((1,), (1,)), ((), ())),
                         preferred_element_type=jnp.float32)
    vrow = lax.broadcasted_iota(jnp.int32, vt.shape, 0) % (2 * HP)
    vt_ref[...] = jnp.where((vrow == V_DIM) | (vrow == HP), 1.0, vt).astype(jnp.bfloat16)

    lane = lax.broadcasted_iota(jnp.int32, (tm, HP), 1)
    row = lax.broadcasted_iota(jnp.int32, (tm, HP), 0) + i * tm
    key_mask = jnp.where(row >= PAD, 0.0, NEG_INF)
    scale = math.log2(math.e) / math.sqrt(QK_DIM)
    ca, sa = ca_ref[pl.ds(i, 1), :], sa_ref[pl.ds(i, 1), :]
    cosv = ca * cb_ref[...] - sa * sb_ref[...]
    sinv = sa * cb_ref[...] + ca * sb_ref[...]
    q_cos = cosv * (qg_ref[...] * scale)
    q_sin = sinv * (qgs_ref[...] * scale)
    k_cos = cosv * kg_ref[...]
    k_sin_term = kpe_sw * (sinv * kgs_ref[...])

    def inv_rms(xb):
        return lax.rsqrt(jnp.sum(xb * xb, axis=-1, keepdims=True) * (1.0 / QK_DIM) + EPS)

    for hd in range(MLA_HEADS):
        sl = slice(hd * HP, (hd + 1) * HP)
        xq = qraw[:, sl]
        qb = (xq * q_cos + qraw_sw[:, sl] * q_sin) * inv_rms(xq)
        q_ref[:, sl] = jnp.where(lane == MASK_LANE, 1.0, qb).astype(jnp.bfloat16)
        xk = kraw[:, sl] + kpe
        kb = (xk * k_cos + k_sin_term) * inv_rms(xk)
        k_ref[:, sl] = jnp.where(lane == MASK_LANE, key_mask, kb).astype(jnp.bfloat16)


def _inproj(x, head, gmix, win, qlg, wuq, wuqs, kvlg, wuk, wuv, qg, qgs, kg, kgs, ca, sa, cb, sb,
            *, tm):
    B, S, D = x.shape
    Tp = X0 + S
    assert tm % X0 == 0 and Tp % tm == 0
    nt = Tp // tm
    ns = tm // X0
    nx = S // X0
    x_specs = [pl.BlockSpec((None, X0, D),
                            lambda b, i, s=s: (b, jnp.clip(ns * i + s - 1, 0, nx - 1), 0))
               for s in range(ns)]
    row = lambda w: pl.BlockSpec((None, tm, w), lambda b, i: (b, i, 0))
    full = lambda a: pl.BlockSpec(a.shape, lambda b, i: (0,) * a.ndim)
    f32, bf16 = jnp.float32, jnp.bfloat16
    return pl.pallas_call(
        functools.partial(_inproj_kernel, tm=tm, ns=ns),
        grid=(B, nt),
        in_specs=x_specs + [full(head), full(gmix), full(win), full(qlg), full(wuq),
                  full(wuqs), full(kvlg), full(wuk), full(wuv), full(qg), full(qgs), full(kg),
                  full(kgs), full(ca), full(sa), full(cb), full(sb)],
        out_specs=[pl.BlockSpec((None, LRU_WIDTH // LANE, tm, LANE), lambda b, i: (b, 0, i, 0)),
                   row(LRU_WIDTH), row(MLA_HEADS * HP), row(MLA_HEADS * HP),
                   pl.BlockSpec((None, MLA_HEADS * HP, tm), lambda b, i: (b, 0, i))],
        out_shape=[jax.ShapeDtypeStruct((B, LRU_WIDTH // LANE, Tp, LANE), f32),
                   jax.ShapeDtypeStruct((B, Tp, LRU_WIDTH), f32),
                   jax.ShapeDtypeStruct((B, Tp, MLA_HEADS * HP), bf16),
                   jax.ShapeDtypeStruct((B, Tp, MLA_HEADS * HP), bf16),
                   jax.ShapeDtypeStruct((B, MLA_HEADS * HP, Tp), bf16)],
        compiler_params=pltpu.CompilerParams(
            dimension_semantics=("arbitrary", "arbitrary"), vmem_limit_bytes=VMEM_LIMIT),
        name="inproj",
    )(*([x] * ns), head, gmix, win, qlg, wuq, wuqs, kvlg, wuk, wuv, qg, qgs, kg, kgs, ca, sa, cb, sb)


NSLAB = LRU_WIDTH // LANE


def _lru_gates(x_ref, xprev_ref, xnext_ref, cw_ref, cb_ref, wg_ref, ba_ref, bx_ref, lam_ref,
               xw_ref, *, tt, first, last, t0):
    W = LRU_WIDTH
    seg = tt // SUB
    xw_ref[:, 0:SUB, :] = jnp.where(first, 0.0, xprev_ref[...])
    xw_ref[:, SUB:SUB + tt, :] = x_ref[...]
    xw_ref[:, SUB + tt:, :] = jnp.where(last, 0.0, xnext_ref[...])
    left = LRU_CONV // 2
    slabs = []
    for c in range(NSLAB):
        cs = slice(c * LANE, (c + 1) * LANE)
        pieces = []
        for j in range(seg):
            acc = cb_ref[:, cs]
            for k in range(LRU_CONV):
                tap = xw_ref[c, pl.ds(SUB - left + j + k, SUB, stride=seg), :]
                acc = acc + tap * cw_ref[k:k + 1, cs]
            pieces.append(acc)
        slabs.append(jnp.concatenate(pieces, axis=0))
    xc = jnp.concatenate(slabs, axis=1)
    xcb = xc.astype(jnp.bfloat16)
    half = W // 2
    g0 = jnp.dot(xcb[:, :half], wg_ref[0], preferred_element_type=jnp.float32)
    g1 = jnp.dot(xcb[:, half:], wg_ref[1], preferred_element_type=jnp.float32)
    ra = jnp.concatenate([g0[:, :half], g1[:, :half]], axis=1) + ba_ref[...]
    ri = jnp.concatenate([g0[:, half:], g1[:, half:]], axis=1) + bx_ref[...]
    r = jax.nn.sigmoid(ra)
    ig = jax.nn.sigmoid(ri)
    lam = lam_ref[...]
    sp = jnp.maximum(-lam, 0.0) + jnp.log(1.0 + jnp.exp(-jnp.abs(lam)))
    log_a = -LRU_C * r * sp
    a = jnp.exp(log_a)
    z = 1.0 - a * a
    u = jnp.where(z > 0.0, z * lax.rsqrt(z), 0.0) * (ig * xc)
    row = lax.broadcasted_iota(jnp.int32, (tt, W), 0)
    time = t0 + (row % SUB) * seg + row // SUB
    return a, jnp.where(time >= PAD, u, 0.0)


def _scan_block(a, u, h_ref, row0, carry_ref, *, tt, reverse):
    W = LRU_WIDTH
    seg = tt // SUB
    order = range(seg - 1, -1, -1) if reverse else range(seg)
    h = jnp.zeros((SUB, W), jnp.float32)
    p = jnp.ones((SUB, W), jnp.float32)
    hs, ps = {}, {}
    for j in order:
        aj = a[j * SUB:(j + 1) * SUB]
        h = aj * h + u[j * SUB:(j + 1) * SUB]
        p = aj * p
        hs[j], ps[j] = h, p
    sub = lax.broadcasted_iota(jnp.int32, (SUB, W), 0)
    e, pf = h, p
    for d in (1, 2, 4):
        if reverse:
            keep, sh = sub < SUB - d, SUB - d
        else:
            keep, sh = sub >= d, d
        e_sh = jnp.where(keep, pltpu.roll(e, sh, axis=0), 0.0)
        pf_sh = jnp.where(keep, pltpu.roll(pf, sh, axis=0), 1.0)
        e = e + pf * e_sh
        pf = pf * pf_sh
    carry = carry_ref[...]
    leaving = e + pf * carry
    if reverse:
        entering = jnp.where(sub < SUB - 1, pltpu.roll(leaving, SUB - 1, axis=0), carry)
        carry_ref[...] = leaving[0:1]
    else:
        entering = jnp.where(sub >= 1, pltpu.roll(leaving, 1, axis=0), carry)
        carry_ref[...] = leaving[SUB - 1:SUB]
    for j in range(seg):
        hj = hs[j] + ps[j] * entering
        for c in range(NSLAB):
            h_ref[c, pl.ds(row0 + j, SUB, stride=seg), :] = hj[:, c * LANE:(c + 1) * LANE]


def _lru_kernel(x_ref, xprev_ref, xnext_ref, cw_ref, cb_ref, wg_ref, ba_ref, bx_ref, lam_ref,
                gl_ref, gg_ref, y_ref, xw_ref, hf_ref, hb_ref, carry_ref, *, tt, nt):
    j = pl.program_id(1)
    bwd = j >= nt
    i = jnp.where(bwd, 2 * nt - 1 - j, j)

    @pl.when((j == 0) | (j == nt))
    def _():
        carry_ref[...] = jnp.zeros_like(carry_ref)

    a, u = _lru_gates(x_ref, xprev_ref, xnext_ref, cw_ref, cb_ref, wg_ref, ba_ref, bx_ref,
                      lam_ref, xw_ref, tt=tt, first=i == 0, last=i == nt - 1, t0=i * tt)
    r0 = pl.multiple_of(i * tt, SUB)

    @pl.when(jnp.logical_not(bwd))
    def _():
        _scan_block(a, u, hf_ref, r0, carry_ref, tt=tt, reverse=False)

    @pl.when(bwd)
    def _():
        _scan_block(a, u, hb_ref, 0, carry_ref, tt=tt, reverse=True)
        ys = [(hf_ref[c, pl.ds(r0, tt), :] + hb_ref[c])
              * _gelu_tanh(gl_ref[:, c * LANE:(c + 1) * LANE]) for c in range(NSLAB)]
        y = jnp.concatenate(ys, axis=1)
        y_ref[...] = _rms(y, gg_ref[...]).astype(y_ref.dtype)


def _lru(xl, cw, cb, wg, ba, bx, lam, gl, gg, *, tt):
    B, _, Tp, _ = xl.shape
    W = LRU_WIDTH
    assert Tp % tt == 0 and tt % SUB == 0 and (tt // SUB) % 2 == 1
    nt = Tp // tt
    nsub = Tp // SUB
    per = tt // SUB
    tblk = lambda j: jnp.where(j >= nt, 2 * nt - 1 - j, j)
    oblk = lambda j: jnp.where(j >= nt, 2 * nt - 1 - j, nt - 1)
    slab = lambda rows, f: pl.BlockSpec((None, NSLAB, rows, LANE), lambda b, j: (b, 0, f(j), 0))
    direction = lambda a: pl.BlockSpec((None,) + a.shape[1:],
                                       lambda b, j: (j // nt,) + (0,) * (a.ndim - 1))
    full = lambda a: pl.BlockSpec(a.shape, lambda b, j: (0,) * a.ndim)
    row = pl.BlockSpec((None, tt, W), lambda b, j: (b, oblk(j), 0))
    return pl.pallas_call(
        functools.partial(_lru_kernel, tt=tt, nt=nt),
        grid=(B, 2 * nt),
        in_specs=[slab(tt, tblk),
                  slab(SUB, lambda j: jnp.maximum(tblk(j) * per - 1, 0)),
                  slab(SUB, lambda j: jnp.minimum((tblk(j) + 1) * per, nsub - 1)),
                  full(cw), full(cb), direction(wg), direction(ba), direction(bx), direction(lam),
                  row, full(gg)],
        out_specs=row,
        out_shape=jax.ShapeDtypeStruct((B, Tp, W), jnp.bfloat16),
        scratch_shapes=[pltpu.VMEM((NSLAB, tt + 2 * SUB, LANE), jnp.float32),
                        pltpu.VMEM((NSLAB, Tp, LANE), jnp.float32),
                        pltpu.VMEM((NSLAB, tt, LANE), jnp.float32),
                        pltpu.VMEM((1, LRU_WIDTH), jnp.float32)],
        compiler_params=pltpu.CompilerParams(
            dimension_semantics=("arbitrary", "arbitrary"), vmem_limit_bytes=VMEM_LIMIT),
        name="lru",
    )(xl, xl, xl, cw, cb, wg, ba, bx, lam, gl, gg)


def _max_over_rows(s):
    rows = s.shape[0]
    while rows % (2 * SUB) == 0 and rows > 32 * SUB:
        fold = 4 if rows % (4 * SUB) == 0 else 2
        rows //= fold
        s = jnp.max(s.reshape(fold, rows, s.shape[1]), axis=0)
    return jnp.max(s, axis=0, keepdims=True)


def _attn_kernel(q_ref, k_ref, vt_ref, o_ref, *, tq, lead, span):
    heads = [slice(hh * HP, (hh + 1) * HP) for hh in range(2)]
    nkeys = k_ref.shape[0]
    nt = (((1,), (1,)), ((), ()))
    rows = lax.broadcasted_iota(jnp.int32, (HP, tq), 0)

    def scores(hh, lo, hi):
        return lax.dot_general(k_ref[lo:hi, heads[hh]], q_ref[:, heads[hh]], nt,
                               preferred_element_type=jnp.float32)

    def pv(hh, lo, hi, s, m):
        return jnp.dot(vt_ref[heads[hh], lo:hi], jnp.exp2(s - m).astype(jnp.bfloat16),
                       preferred_element_type=jnp.float32)

    def finish(a0, a1):
        l0, l1 = a0[V_DIM:V_DIM + 1, :], a1[0:1, :]
        return jnp.where(rows < V_DIM, a0 / l0, a1 / l1).T, l0, l1

    accs = []
    for hh in range(2):
        s = scores(hh, 0, lead)
        m = _max_over_rows(s)
        acc = pv(hh, 0, lead, s, m)
        for lo in range(lead, nkeys, span):
            acc = acc + pv(hh, lo, lo + span, scores(hh, lo, lo + span), m)
        accs.append(acc)
    out, l0, l1 = finish(*accs)
    lmin = jnp.min(jnp.minimum(l0, l1))
    lmax = jnp.max(jnp.maximum(l0, l1))
    safe = (lmin > 2.0 ** -100) & (lmax < 2.0 ** 100)

    @pl.when(safe)
    def _():
        o_ref[...] = out

    @pl.when(jnp.logical_not(safe))
    def _():
        exact = []
        for hh in range(2):
            s = scores(hh, 0, nkeys)
            exact.append(pv(hh, 0, nkeys, s, _max_over_rows(s)))
        o_ref[...] = finish(*exact)[0]


def _attn(q, k, vt, *, tq, lead, span):
    B, Tp, _ = q.shape
    assert Tp % tq == 0 and tq % (2 * HP) == 0 and lead % (2 * HP) == 0 and span % (2 * HP) == 0
    npair = MLA_HEADS // 2
    return pl.pallas_call(
        functools.partial(_attn_kernel, tq=tq, lead=lead, span=span),
        grid=(B, npair, Tp // tq),
        in_specs=[pl.BlockSpec((None, tq, 2 * HP), lambda b, p, i: (b, i, p)),
                  pl.BlockSpec((None, Tp, 2 * HP), lambda b, p, i: (b, 0, p)),
                  pl.BlockSpec((None, 2 * HP, Tp), lambda b, p, i: (b, p, 0))],
        out_specs=pl.BlockSpec((None, tq, HP), lambda b, p, i: (b, i, p)),
        out_shape=jax.ShapeDtypeStruct((B, Tp, npair * HP), jnp.float32),
        compiler_params=pltpu.CompilerParams(
            dimension_semantics=("arbitrary", "arbitrary", "arbitrary"),
            vmem_limit_bytes=VMEM_LIMIT),
        name="attn",
    )(q, k, vt)


def _mix_ffn_kernel(*refs, tm, nt, ns):
    x_ref, xprev_ref, xnext_ref, mtail_ref = refs[:4]
    yl_refs, (ylprev_ref, ylnext_ref) = refs[4:4 + ns], refs[4 + ns:6 + ns]
    ym_refs, (ymprev_ref, ymnext_ref) = refs[6 + ns:6 + 2 * ns], refs[6 + 2 * ns:8 + 2 * ns]
    (mg_ref, wo_ref, g_ref, wg_ref, wu_ref, cw_ref, cb_ref, wd_ref, o_ref,
     gate_ref) = refs[8 + 2 * ns:]
    i = pl.program_id(1)
    f32, bf16 = jnp.float32, jnp.bfloat16
    H = 2 * SUB
    xw = jnp.concatenate([jnp.where(i == 0, mtail_ref[...], xprev_ref[...]), x_ref[...],
                          xnext_ref[...]], axis=0)
    ylw = jnp.concatenate([ylprev_ref[...].astype(f32)[H - SUB:]]
                          + [r[...].astype(f32) for r in yl_refs]
                          + [ylnext_ref[...].astype(f32)[:SUB]], axis=0).astype(bf16)
    ymw = jnp.concatenate([ymprev_ref[...]] + [r[...] for r in ym_refs] + [ymnext_ref[...]],
                          axis=0)
    y = jnp.concatenate([ylw, _rms(ymw, mg_ref[...]).astype(bf16)], axis=1)
    h1 = xw + jnp.dot(y, wo_ref[...], preferred_element_type=f32)
    r = lax.broadcasted_iota(jnp.int32, h1.shape, 0)
    h1 = jnp.where((i == nt - 1) & (r >= tm + SUB), 0.0, h1)
    hn = _rms(h1, g_ref[...]).astype(bf16)
    gate_ref[...] = jnp.dot(hn, wg_ref[...], preferred_element_type=f32)
    up = jnp.dot(hn[SUB:SUB + tm], wu_ref[...], preferred_element_type=f32)
    left = FFN_CONV // 2
    gc = cb_ref[...] + gate_ref[SUB - left:SUB - left + tm, :] * cw_ref[0:1, :]
    for k in range(1, FFN_CONV):
        gc = gc + gate_ref[SUB - left + k:SUB - left + k + tm, :] * cw_ref[k:k + 1, :]
    act = (gc * jax.nn.sigmoid(gc) * up).astype(bf16)
    o_ref[...] = h1[SUB:SUB + tm] + jnp.dot(act, wd_ref[...], preferred_element_type=f32)


def _mix_ffn(x, mtail, yl, ym, mg, wo, g, wg, wu, cw, cb, wd, *, tm):
    B, S, D = x.shape
    Tp = yl.shape[1]
    W = yl.shape[2]
    assert tm % X0 == 0 and S % tm == 0
    nt = S // tm
    ns = tm // X0
    H = 2 * SUB
    full = lambda a: pl.BlockSpec(a.shape, lambda b, i: (0,) * a.ndim)

    def halo(w, rows, total, first_row):
        prev = pl.BlockSpec((None, rows, w),
                            lambda b, i: (b, jnp.maximum(first_row(i) // rows - 1, 0), 0))
        nxt = pl.BlockSpec((None, rows, w),
                           lambda b, i: (b, jnp.minimum((first_row(i) + tm) // rows,
                                                        total // rows - 1), 0))
        return [prev, nxt]

    def mixer_tiles(w):
        return [pl.BlockSpec((None, X0, w), lambda b, i, s=s: (b, 1 + i * ns + s, 0))
                for s in range(ns)]

    seq_row = lambda i: i * tm
    pad_row = lambda i: X0 + i * tm
    in_specs = ([pl.BlockSpec((None, tm, D), lambda b, i: (b, i, 0))] + halo(D, SUB, S, seq_row)
                + [full(mtail)]
                + mixer_tiles(W) + halo(W, H, Tp, pad_row)
                + mixer_tiles(W) + halo(W, SUB, Tp, pad_row)
                + [full(a) for a in (mg, wo, g, wg, wu, cw, cb, wd)])
    return pl.pallas_call(
        functools.partial(_mix_ffn_kernel, tm=tm, nt=nt, ns=ns),
        grid=(B, nt),
        in_specs=in_specs,
        out_specs=pl.BlockSpec((None, tm, D), lambda b, i: (b, i, 0)),
        out_shape=jax.ShapeDtypeStruct((B, S, D), jnp.float32),
        scratch_shapes=[pltpu.VMEM((tm + 2 * SUB, D_FF), jnp.float32)],
        compiler_params=pltpu.CompilerParams(
            dimension_semantics=("arbitrary", "arbitrary"), vmem_limit_bytes=VMEM_LIMIT),
        name="mix_ffn",
    )(x, x, x, mtail, *([yl] * (ns + 2)), *([ym] * (ns + 2)), mg, wo, g, wg, wu, cw, cb, wd)


def _pad_heads(w, width, offset=0):
    K = w.shape[0]
    w = w.reshape(K, MLA_HEADS, width)
    w = jnp.pad(w, ((0, 0), (0, 0), (offset, HP - width - offset)))
    return w.reshape(K, MLA_HEADS * HP)


def _block_diag(w):
    H, Dh, _ = w.shape
    eye = jnp.eye(H, dtype=w.dtype)
    return (eye[:, None, :, None] * w[:, :, None, :]).reshape(H * Dh, H * Dh)


def _gate_weights(w_a, w_x):
    half = LRU_WIDTH // 2
    wa = _block_diag(w_a)
    wx = _block_diag(w_x)
    halves = [jnp.concatenate([wa[c * half:(c + 1) * half, c * half:(c + 1) * half],
                               wx[c * half:(c + 1) * half, c * half:(c + 1) * half]], axis=1)
              for c in range(2)]
    return jnp.stack(halves).astype(jnp.bfloat16)


def kernel(x, meta_tokens, norm_mix_g, w_in, conv_lru_w, conv_lru_b, lru_w_a, lru_b_a, lru_w_x,
           lru_b_x, lru_lambda, lru_gate_g, q_latent_g, w_uq, kv_latent_g, w_ukv, q_norm_g,
           k_norm_g, mla_out_g, w_out, norm_ffn_g, w_ffn_up, conv_ffn_w, conv_ffn_b, w_ffn_down):
    B, S, D = x.shape
    f32, bf16 = jnp.float32, jnp.bfloat16
    Tp = X0 + S
    l = 0
    r2 = lambda a: a.reshape(1, -1).astype(f32)

    head = jnp.concatenate([jnp.zeros((PAD, D), x.dtype), meta_tokens.astype(x.dtype)], axis=0)

    half = QK_ROPE // 2

    def swap(w):
        t = w.reshape(w.shape[:-1] + (-1, HP))
        t = jnp.concatenate([t[..., :QK_NOPE], t[..., QK_NOPE + half:QK_DIM],
                             t[..., QK_NOPE:QK_NOPE + half], t[..., QK_DIM:]], axis=-1)
        return t.reshape(w.shape)

    o = 2 * LRU_WIDTH + Q_LORA + KV_LORA
    kpe_w = jnp.pad(w_in[l][:, o:], ((0, 0), (QK_NOPE, LANE - QK_DIM)))
    win = jnp.concatenate([w_in[l][:, :o], kpe_w, swap(kpe_w)], axis=1).astype(bf16)
    wuq = _pad_heads(w_uq[l], QK_DIM).astype(bf16)
    wuqs = swap(wuq)
    wkv = w_ukv[l].reshape(KV_LORA, MLA_HEADS, QK_NOPE + V_DIM)
    wuk = _pad_heads(wkv[:, :, :QK_NOPE].reshape(KV_LORA, -1), QK_NOPE).astype(bf16)
    wv = jnp.pad(wkv[:, :, QK_NOPE:], ((0, 0), (0, 0), (0, HP - V_DIM)))
    wv = jnp.where((jnp.arange(MLA_HEADS) % 2 == 1)[None, :, None],
                   jnp.roll(wv, V_DIM, axis=2), wv)
    wuvt = wv.reshape(KV_LORA, MLA_HEADS * HP).T.astype(bf16)
    qg = jnp.pad(q_norm_g[l], (0, HP - QK_DIM)).reshape(1, HP)
    kg = jnp.pad(k_norm_g[l], (0, HP - QK_DIM)).reshape(1, HP)

    inv_freq = ROPE_THETA ** (-jnp.arange(0, QK_ROPE, 2, dtype=f32) / QK_ROPE)

    def lane_tables(pos):
        ang = pos[:, None] * inv_freq[None, :]
        n = pos.shape[0]
        cos, sin = jnp.cos(ang), jnp.sin(ang)
        ctab = jnp.concatenate([jnp.ones((n, QK_NOPE), f32), cos, cos,
                                jnp.ones((n, HP - QK_DIM), f32)], axis=1)
        stab = jnp.concatenate([jnp.zeros((n, QK_NOPE), f32), -sin, sin,
                                jnp.zeros((n, HP - QK_DIM), f32)], axis=1)
        return ctab, stab

    rope = lane_tables(jnp.arange(Tp // TM_IN, dtype=f32) * TM_IN) + lane_tables(
        jnp.arange(TM_IN, dtype=f32) - PAD)

    xl, gl, q, k, vt = _inproj(x, head, r2(norm_mix_g[l]), win, r2(q_latent_g[l]), wuq, wuqs,
                              r2(kv_latent_g[l]), wuk, wuvt, qg, swap(qg), kg, swap(kg),
                              *rope, tm=TM_IN)

    cw, cb = conv_lru_w[l], r2(conv_lru_b[l])
    wg = jnp.stack([_gate_weights(lru_w_a[l, d], lru_w_x[l, d]) for d in range(2)])
    per_dir = lambda a: a.astype(f32)[:, None, :]
    yl = _lru(xl, cw, cb, wg, per_dir(lru_b_a[l]), per_dir(lru_b_x[l]), per_dir(lru_lambda[l]),
              gl, r2(lru_gate_g[l]), tt=TT_LRU)

    ym = _attn(q, k, vt, tq=TQ_ATT, lead=LEAD_ATT, span=SPAN_ATT)

    wup = w_ffn_up[l].astype(bf16)
    return _mix_ffn(x, head[X0 - SUB:], yl, ym, r2(mla_out_g[l]), w_out[l].astype(bf16),
                    r2(norm_ffn_g[l]), wup[:, :D_FF], wup[:, D_FF:], conv_ffn_w[l],
                    r2(conv_ffn_b[l]), w_ffn_down[l].astype(bf16), tm=TM_FFN)
```
